```python
import math
import jax, jax.numpy as jnp
from jax import lax
import numpy as np

D_MODEL = 2048
BATCH = 16
SEQ = 2048
DEPTH = 4

CHUNK = 64
N_META = 16
D_A = D_MODEL // 2
CONV_WIDTH = 31
H_B = 8
DH_B = D_MODEL // (2 * H_B)
D_C = D_MODEL // 8
H_IDX = 16
D_IDX = 64
TOPK_MAX = 256
Q_BLOCK = 128
N_BUCKETS = 32
MAX_DISTANCE = 128
D_MIX_EVEN = D_A + H_B * DH_B
EVEN_SPLITS = (D_A, D_A, D_A, H_B * DH_B, D_C, H_B * DH_B, H_IDX * D_IDX, D_IDX, H_IDX)
P_EVEN = sum(EVEN_SPLITS)
H_C = 16
DK_C = D_MODEL // H_C
DV_C = D_MODEL // H_C
D_REC = H_C * DK_C
N_EVEN = (DEPTH + 1) // 2
N_ODD = DEPTH // 2
RMS_EPS = 1e-6
LN_EPS = 1e-5
NEG_INF = -1e30

kernel_name = 'chunk_causal_hybrid_conv_dsa_hgrn2'


def rms_norm(x, g):
    xf = x.astype(jnp.float32)
    y = xf * lax.rsqrt(jnp.mean(jnp.square(xf), -1, keepdims=True) + RMS_EPS)
    return (y * g.astype(jnp.float32)).astype(x.dtype)


def layer_norm(x, g, b):
    xf = x.astype(jnp.float32)
    mu = jnp.mean(xf, -1, keepdims=True)
    var = jnp.mean(jnp.square(xf - mu), -1, keepdims=True)
    y = (xf - mu) * lax.rsqrt(var + LN_EPS) * g.astype(jnp.float32) + b.astype(jnp.float32)
    return y.astype(x.dtype)


def split_last(a, sizes):
    out, start = [], 0
    for s in sizes:
        out.append(a[..., start:start + s])
        start += s
    return out


def chunk_ids(pos):
    return jnp.where(pos < N_META, 0, 1 + (pos - N_META) // CHUNK)


def t5_bucket(rel):
    nb = N_BUCKETS // 2
    ret = jnp.where(rel > 0, nb, 0)
    n = jnp.abs(rel)
    max_exact = nb // 2
    nf = jnp.maximum(n, 1).astype(jnp.float32)
    large = max_exact + (jnp.log(nf / max_exact) / math.log(MAX_DISTANCE / max_exact)
                         * (nb - max_exact)).astype(jnp.int32)
    large = jnp.minimum(large, nb - 1)
    return ret + jnp.where(n < max_exact, n, large)


def dsa_attention(q_lat, c, qi, ki, wi, chunk_id, rel_bias_table, topk):
    bsz, t_len = c.shape[:2]
    n_blk = -(-t_len // Q_BLOCK)
    t_pad = n_blk * Q_BLOCK

    def blocks(a):
        a = jnp.pad(a, [(0, 0), (0, t_pad - t_len)] + [(0, 0)] * (a.ndim - 2))
        return jnp.moveaxis(a.reshape((bsz, n_blk, Q_BLOCK) + a.shape[2:]), 1, 0)

    qpos = jnp.arange(t_pad, dtype=jnp.int32).reshape(n_blk, Q_BLOCK)
    qchunk = chunk_ids(qpos)

    def one_block(args):
        q_b, qi_b, wi_b, qpos_b, qchunk_b = args
        dots = jnp.einsum('bqhd,bsd->bqhs', qi_b, ki)
        score = jnp.einsum('bqh,bqhs->bqs', wi_b, jax.nn.relu(dots)).astype(jnp.float32)
        admissible = chunk_id[None, :] <= qchunk_b[:, None]
        score = jnp.where(admissible[None], score, NEG_INF)
        _, idx = lax.top_k(score, topk)
        c_sel = jax.vmap(lambda cb, ib: cb[ib])(c, idx)
        valid = chunk_id[idx] <= qchunk_b[None, :, None]
        bias = rel_bias_table[t5_bucket(idx - qpos_b[None, :, None])]
        logits = (jnp.einsum('bqhc,bqkc->bhqk', q_b, c_sel).astype(jnp.float32)
                  + jnp.moveaxis(bias, -1, 1).astype(jnp.float32))
        logits = jnp.where(valid[:, None], logits, NEG_INF)
        p = jax.nn.softmax(logits, axis=-1).astype(c.dtype)
        return jnp.einsum('bhqk,bqkc->bqhc', p, c_sel)

    out = lax.map(one_block, (blocks(q_lat), blocks(qi), blocks(wi), qpos, qchunk))
    out = jnp.moveaxis(out, 0, 1).reshape((bsz, t_pad) + out.shape[3:])
    return out[:, :t_len]


def even_mixer(hn, w_in, conv_w, conv_b, ln_g, ln_b, kv_g, w_uk, w_uv, w_out,
               rel_bias_table, chunk_id, topk):
    bsz, t_len, _ = hn.shape
    glu_v, glu_g, gate_a, q, c, gate_b, qi, ki, wi = split_last(hn @ w_in, EVEN_SPLITS)
    u = glu_v * jax.nn.sigmoid(glu_g)
    u = lax.conv_general_dilated(u, conv_w[:, None, :], (1,), [(CONV_WIDTH - 1, 0)],
                                 dimension_numbers=('NWC', 'WIO', 'NWC'),
                                 feature_group_count=D_A) + conv_b
    a_out = jax.nn.silu(layer_norm(u, ln_g, ln_b)) * jax.nn.silu(gate_a)
    c = rms_norm(c, kv_g)
    q_lat = jnp.einsum('bthd,hcd->bthc', q.reshape(bsz, t_len, H_B, DH_B), w_uk) * (DH_B ** -0.5)
    qi = qi.reshape(bsz, t_len, H_IDX, D_IDX) * (D_IDX ** -0.5)
    wi = wi * (H_IDX ** -0.5)
    o_lat = dsa_attention(q_lat, c, qi, ki, wi, chunk_id, rel_bias_table, topk)
    b_out = jnp.einsum('bthc,hcd->bthd', o_lat, w_uv).reshape(bsz, t_len, H_B * DH_B)
    b_out = b_out * jax.nn.silu(gate_b)
    return jnp.concatenate([a_out, b_out], axis=-1) @ w_out


def hgrn2_chunkwise(q, log_f, k, v):
    bsz, t_len = q.shape[:2]
    lpad = (-N_META) % CHUNK
    rpad = (-(t_len + lpad)) % CHUNK
    n_chunks = (t_len + lpad + rpad) // CHUNK

    def chunks(a):
        a = jnp.pad(a.astype(jnp.float32), ((0, 0), (lpad, rpad), (0, 0), (0, 0)))
        return a.reshape(bsz, n_chunks, CHUNK, H_C, a.shape[-1]).transpose(1, 0, 3, 2, 4)

    causal = jnp.tril(jnp.ones((CHUNK, CHUNK), dtype=bool))

    def step(state, inp):
        q_c, g_c, k_c, v_c = inp
        b = jnp.cumsum(g_c, axis=2)
        o_inter = jnp.einsum('bhtk,bhkv->bhtv', q_c * jnp.exp(b), state)
        diff = jnp.where(causal[:, :, None], b[:, :, :, None, :] - b[:, :, None, :, :], -jnp.inf)
        attn = jnp.einsum('bhtk,bhtsk,bhsk->bhts', q_c, jnp.exp(diff), k_c)
        o_intra = jnp.einsum('bhts,bhsv->bhtv', attn, v_c)
        b_last = b[:, :, -1]
        state = (jnp.exp(b_last)[..., None] * state
                 + jnp.einsum('bhsk,bhsv->bhkv', k_c * jnp.exp(b_last[:, :, None] - b), v_c))
        return state, o_inter + o_intra

    s0 = jnp.zeros((bsz, H_C, DK_C, DV_C), jnp.float32)
    _, out = lax.scan(step, s0, (chunks(q), chunks(log_f), chunks(k), chunks(v)))
    out = out.transpose(1, 0, 3, 2, 4).reshape(bsz, n_chunks * CHUNK, H_C, DV_C)
    return out[:, lpad:lpad + t_len]


def odd_mixer(hn, w_in, lb, norm_g, w_out):
    bsz, t_len, _ = hn.shape
    q, fz, i, g = split_last(hn @ w_in, (D_REC, D_REC, D_REC, D_REC))
    heads = lambda a: a.reshape(bsz, t_len, H_C, -1)
    fz = fz.astype(jnp.float32)
    log_f = jnp.logaddexp(jnp.log(lb), jnp.log1p(-lb) + jax.nn.log_sigmoid(fz))
    k = (1.0 - lb) * jax.nn.sigmoid(-fz)
    o = hgrn2_chunkwise(heads(jax.nn.silu(q)), heads(log_f), heads(k), heads(i))
    o = o * lax.rsqrt(jnp.mean(jnp.square(o), -1, keepdims=True) + RMS_EPS)
    o = o * norm_g.reshape(H_C, DV_C).astype(jnp.float32)
    o = o.reshape(bsz, t_len, D_REC).astype(hn.dtype) * jax.nn.silu(g)
    return o @ w_out


def setup_inputs(seed: int = 0) -> dict:
    key = jax.random.key(seed)
    ks = jax.random.split(key, 20)
    nrm = lambda k, shape, s: jax.random.normal(k, shape, jnp.float32) * s
    return {
        'x': nrm(ks[0], (BATCH, SEQ, D_MODEL), 1.0),
        'meta_tokens': nrm(ks[1], (N_META, D_MODEL), 1.0),
        'norm_gain': 1.0 + nrm(ks[2], (DEPTH, D_MODEL), 0.02),
        'final_norm_gain': 1.0 + nrm(ks[3], (D_MODEL,), 0.02),
        'rel_bias_table': nrm(ks[4], (N_BUCKETS, H_B), 0.5),
        'w_in_even': nrm(ks[5], (N_EVEN, D_MODEL, P_EVEN), D_MODEL ** -0.5),
        'conv_w': nrm(ks[6], (N_EVEN, CONV_WIDTH, D_A), CONV_WIDTH ** -0.5),
        'conv_b': nrm(ks[7], (N_EVEN, D_A), 0.01),
        'conv_ln_gain': 1.0 + nrm(ks[8], (N_EVEN, D_A), 0.02),
        'conv_ln_bias': nrm(ks[9], (N_EVEN, D_A), 0.01),
        'kv_norm_gain': 1.0 + nrm(ks[10], (N_EVEN, D_C), 0.02),
        'w_uk': nrm(ks[11], (N_EVEN, H_B, D_C, DH_B), D_C ** -0.5),
        'w_uv': nrm(ks[12], (N_EVEN, H_B, D_C, DH_B), D_C ** -0.5),
        'w_out_even': nrm(ks[13], (N_EVEN, D_MIX_EVEN, D_MODEL), D_MIX_EVEN ** -0.5),
        'w_in_odd': nrm(ks[14], (N_ODD, D_MODEL, 4 * D_REC), D_MODEL ** -0.5),
        'lb_logits': nrm(ks[15], (DEPTH, D_REC), 0.1),
        'rec_norm_gain': 1.0 + nrm(ks[16], (N_ODD, D_REC), 0.02),
        'w_out_odd': nrm(ks[17], (N_ODD, D_REC, D_MODEL), D_REC ** -0.5),
    }


def reference(x, meta_tokens, norm_gain, final_norm_gain, rel_bias_table, w_in_even, conv_w,
              conv_b, conv_ln_gain, conv_ln_bias, kv_norm_gain, w_uk, w_uv, w_out_even,
              w_in_odd, lb_logits, rec_norm_gain, w_out_odd):
    bsz, seq, d = x.shape
    t_len = seq + N_META
    topk = min(TOPK_MAX, seq // 4)
    h = jnp.concatenate([jnp.broadcast_to(meta_tokens.astype(x.dtype), (bsz, N_META, d)), x], axis=1)
    chunk_id = chunk_ids(jnp.arange(t_len, dtype=jnp.int32))
    lb_soft = jax.nn.softmax(lb_logits.astype(jnp.float32), axis=0)
    lower_bounds = jnp.cumsum(lb_soft, axis=0) - lb_soft[0]
    for layer in range(DEPTH):
        hn = rms_norm(h, norm_gain[layer])
        if layer % 2 == 0:
            e = layer // 2
            y = even_mixer(hn, w_in_even[e], conv_w[e], conv_b[e], conv_ln_gain[e], conv_ln_bias[e],
                           kv_norm_gain[e], w_uk[e], w_uv[e], w_out_even[e], rel_bias_table,
                           chunk_id, topk)
        else:
            o = layer // 2
            y = odd_mixer(hn, w_in_odd[o], lower_bounds[layer], rec_norm_gain[o], w_out_odd[o])
        h = h + y
    h = rms_norm(h, final_norm_gain)
    return h[:, N_META:]
```

```python
import functools
import math

import numpy as np
import jax
import jax.numpy as jnp
from jax import lax
from jax.experimental import pallas as pl
from jax.experimental.pallas import tpu as pltpu

F32 = jnp.float32
BF16 = jnp.bfloat16
I32 = jnp.int32

D_MODEL = 2048
CHUNK = 64
N_META = 16
D_A = 1024
CONV_WIDTH = 31
H_B = 8
DH_B = 128
D_C = 256
H_IDX = 16
D_IDX = 64
TOPK_MAX = 256
N_BUCKETS = 32
MAX_DISTANCE = 128
H_C = 16
DK_C = 128
RMS_EPS = 1e-6
LN_EPS = 1e-5
NEG_INF = -1e30
INT_MIN = -(2 ** 31)

LANES = 128
SUBLANES = 8
MXU_N = 256

FRONT_PAD = LANES - N_META
QB = LANES
KB = MXU_N
HC = 128
N_LEVELS = 7

E_NPAD = 6656
E_COL_Q, E_COL_GB, E_COL_QI = 3072, 4096, 5120
E_COL_C, E_COL_KK, E_COL_WI = 6144, 6400, 6528

MIB = 1024 * 1024


def _cparams(sem, vmem_mib):
    return pltpu.CompilerParams(dimension_semantics=sem, vmem_limit_bytes=vmem_mib * MIB)


def _sigmoid(x):
    return 1.0 / (1.0 + jnp.exp(-x))


_NN = (((1,), (0,)), ((), ()))
_NT = (((1,), (1,)), ((), ()))


def _dot(a, b, dims=_NN):
    return lax.dot_general(a, b, dims, preferred_element_type=F32)


def _norm_matmul_kernel(x_ref, g_ref, w_ref, o_ref, hn_ref):
    @pl.when(pl.program_id(2) == 0)
    def _():
        x = x_ref[...]
        ms = jnp.mean(x * x, axis=-1, keepdims=True)
        hn_ref[...] = (x * lax.rsqrt(ms + RMS_EPS) * g_ref[...]).astype(BF16)

    o_ref[...] = _dot(hn_ref[...], w_ref[...]).astype(o_ref.dtype)


def norm_matmul(h, gain, w, *, tt, tn):
    bsz, tp, d = h.shape
    n = w.shape[1]
    return pl.pallas_call(
        _norm_matmul_kernel,
        grid=(bsz, tp // tt, n // tn),
        in_specs=[
            pl.BlockSpec((None, tt, d), lambda b, i, j: (b, i, 0)),
            pl.BlockSpec((1, d), lambda b, i, j: (0, 0)),
            pl.BlockSpec((d, tn), lambda b, i, j: (0, j)),
        ],
        out_specs=pl.BlockSpec((None, tt, tn), lambda b, i, j: (b, i, j)),
        out_shape=jax.ShapeDtypeStruct((bsz, tp, n), BF16),
        scratch_shapes=[pltpu.VMEM((tt, d), BF16)],
        compiler_params=_cparams(("parallel", "parallel", "arbitrary"), 48),
        name="norm_matmul",
    )(h, gain.reshape(1, d).astype(F32), w)


def _out_proj_kernel(*refs, n_in, tt, final):
    xs = refs[:n_in]
    w_ref, h_ref = refs[n_in], refs[n_in + 1]
    g_ref = refs[n_in + 2] if final else None
    o_ref = refs[-1]
    y = None
    off = 0
    for x_ref in xs:
        kdim = x_ref.shape[-1]
        part = _dot(x_ref[...], w_ref[off:off + kdim, :])
        y = part if y is None else y + part
        off += kdim
    row = pl.program_id(1) * tt + lax.broadcasted_iota(I32, (tt, 1), 0)
    hn = jnp.where(row >= FRONT_PAD, h_ref[...] + y, 0.0)
    if final:
        ms = jnp.mean(hn * hn, axis=-1, keepdims=True)
        hn = hn * lax.rsqrt(ms + RMS_EPS) * g_ref[...]
    o_ref[...] = hn


def out_proj(xs, w, h, final_gain=None, *, tt):
    bsz, tp, d = h.shape
    final = final_gain is not None
    in_specs = [pl.BlockSpec((None, tt, x.shape[-1]), lambda b, i: (b, i, 0)) for x in xs]
    in_specs += [
        pl.BlockSpec(w.shape, lambda b, i: (0, 0)),
        pl.BlockSpec((None, tt, d), lambda b, i: (b, i, 0)),
    ]
    args = list(xs) + [w, h]
    if final:
        in_specs.append(pl.BlockSpec((1, d), lambda b, i: (0, 0)))
        args.append(final_gain.reshape(1, d).astype(F32))
    return pl.pallas_call(
        functools.partial(_out_proj_kernel, n_in=len(xs), tt=tt, final=final),
        grid=(bsz, tp // tt),
        in_specs=in_specs,
        out_specs=pl.BlockSpec((None, tt, d), lambda b, i: (b, i, 0)),
        out_shape=jax.ShapeDtypeStruct((bsz, tp, d), F32),
        compiler_params=_cparams(("parallel", "parallel"), 56),
        name="out_proj_final" if final else "out_proj",
    )(*args)


CONV_HALO = 32
CONV_RB = 16


def _conv_kernel(v_ref, g_ref, ga_ref, w_ref, cb_ref, lg_ref, lb_ref, o_ref, us_ref, cv_ref, *, tt):
    j = pl.program_id(1)
    u_new = v_ref[...].astype(F32) * _sigmoid(g_ref[...].astype(F32))
    @pl.when(j == 0)
    def _():
        us_ref[0, 0:CONV_HALO, :] = jnp.zeros((CONV_HALO, D_A), F32)

    @pl.when(j > 0)
    def _():
        us_ref[0, 0:CONV_HALO, :] = us_ref[0, tt:tt + CONV_HALO, :]

    us_ref[0, CONV_HALO:CONV_HALO + tt, :] = u_new
    span = tt + CONV_HALO - SUBLANES
    for r in range(1, SUBLANES):
        us_ref[r, 0:span, :] = us_ref[0, r:r + span, :]

    n_rb = tt // CONV_RB
    for cb in range(D_A // LANES):
        lanes = slice(cb * LANES, (cb + 1) * LANES)
        wb = [jnp.broadcast_to(w_ref[k:k + 1, lanes], (CONV_RB, LANES)) for k in range(CONV_WIDTH)]
        bias = jnp.broadcast_to(cb_ref[0:1, lanes], (CONV_RB, LANES))

        def rb_body(rb, carry, lanes=lanes, wb=wb, bias=bias):
            base = pl.multiple_of(rb * CONV_RB, CONV_RB)
            acc = bias
            for k in range(CONV_WIDTH):
                a, r = divmod(k + CONV_HALO - (CONV_WIDTH - 1), SUBLANES)
                acc = acc + wb[k] * us_ref[r, pl.ds(base + SUBLANES * a, CONV_RB), lanes]
            cv_ref[pl.ds(base, CONV_RB), lanes] = acc
            return carry

        lax.fori_loop(0, n_rb, rb_body, 0)

    x = cv_ref[...]
    mu = jnp.mean(x, axis=-1, keepdims=True)
    xc = x - mu
    var = jnp.mean(xc * xc, axis=-1, keepdims=True)
    y = xc * lax.rsqrt(var + LN_EPS) * lg_ref[...] + lb_ref[...]
    ga = ga_ref[...].astype(F32)
    o_ref[...] = (y * _sigmoid(y) * (ga * _sigmoid(ga))).astype(o_ref.dtype)


def conv_module(proj, conv_w, conv_b, ln_g, ln_b, *, tt):
    bsz, tp, _ = proj.shape
    row = lambda a: a.reshape(1, D_A).astype(F32)
    return pl.pallas_call(
        functools.partial(_conv_kernel, tt=tt),
        grid=(bsz, tp // tt),
        in_specs=[
            pl.BlockSpec((None, tt, D_A), lambda b, j: (b, j, 0)),
            pl.BlockSpec((None, tt, D_A), lambda b, j: (b, j, 1)),
            pl.BlockSpec((None, tt, D_A), lambda b, j: (b, j, 2)),
            pl.BlockSpec((CONV_WIDTH, D_A), lambda b, j: (0, 0)),
            pl.BlockSpec((1, D_A), lambda b, j: (0, 0)),
            pl.BlockSpec((1, D_A), lambda b, j: (0, 0)),
            pl.BlockSpec((1, D_A), lambda b, j: (0, 0)),
        ],
        out_specs=pl.BlockSpec((None, tt, D_A), lambda b, j: (b, j, 0)),
        out_shape=jax.ShapeDtypeStruct((bsz, tp, D_A), BF16),
        scratch_shapes=[
            pltpu.VMEM((SUBLANES, tt + CONV_HALO, D_A), F32),
            pltpu.VMEM((tt, D_A), F32),
        ],
        compiler_params=_cparams(("parallel", "arbitrary"), 40),
        name="conv_module",
    )(proj, proj, proj, conv_w.astype(F32), row(conv_b), row(ln_g), row(ln_b))


BIAS_ROWS = 640
BIAS_ORIGIN = 384


def _attn_kernel(c_ref, kk_ref, wi_ref, q_ref, qi_ref, gb_ref, kvg_ref, wuk_ref, wuvt_ref, bias_ref,
                 o_ref, cn_s, cnt_s, kia_s, kib_s, st_s, qlat_s, pt_s, acc_s, m_s, l_s, *, tp, topk):
    i = pl.program_id(1)
    n_kb = (i + 2) // 2
    n_kb_all = cn_s.shape[0]

    @pl.when(i == 0)
    def _prepare_keys():
        gain = kvg_ref[...]
        for kb in range(n_kb_all):
            lo = kb * KB
            nrow = min(KB, tp - lo)
            c = c_ref[lo:lo + nrow, :].astype(F32)
            cn = c * lax.rsqrt(jnp.mean(c * c, axis=-1, keepdims=True) + RMS_EPS) * gain
            kk = kk_ref[lo:lo + nrow, :]
            lane = lax.broadcasted_iota(I32, kk.shape, 1)
            zero = jnp.zeros_like(kk)
            cn_s[kb, 0:nrow, :] = cn.astype(BF16)
            kia_s[kb, 0:nrow, :] = jnp.where(lane < D_IDX, kk, zero)
            kib_s[kb, 0:nrow, :] = jnp.where(lane >= D_IDX, kk, zero)
            for rb in range(nrow // LANES):
                for cb in range(D_C // LANES):
                    tile = cn[rb * LANES:(rb + 1) * LANES, cb * LANES:(cb + 1) * LANES]
                    cnt_s[kb, cb * LANES:(cb + 1) * LANES, rb * LANES:(rb + 1) * LANES] = tile.T.astype(BF16)
            if nrow < KB:
                cn_s[kb, nrow:KB, :] = jnp.zeros((KB - nrow, D_C), BF16)
                kia_s[kb, nrow:KB, :] = jnp.zeros((KB - nrow, LANES), BF16)
                kib_s[kb, nrow:KB, :] = jnp.zeros((KB - nrow, LANES), BF16)
                cnt_s[kb, :, nrow:KB] = jnp.zeros((D_C, KB - nrow), BF16)

    for h in range(H_B):
        qh = q_ref[:, h * DH_B:(h + 1) * DH_B]
        ql = _dot(qh, wuk_ref[h]) * (DH_B ** -0.5)
        qlat_s[h * QB:(h + 1) * QB, :] = ql.astype(BF16)

    wit = wi_ref[...].astype(F32).T

    def score_body(kb, carry):
        ka = kia_s[kb]
        kb_ = kib_s[kb]
        acc = jnp.zeros((KB, QB), F32)
        for p in range(H_IDX // 2):
            qp = qi_ref[:, p * LANES:(p + 1) * LANES]
            da = _dot(ka, qp, _NT)
            db = _dot(kb_, qp, _NT)
            acc = acc + jnp.maximum(da, 0.0) * wit[2 * p:2 * p + 1, :]
            acc = acc + jnp.maximum(db, 0.0) * wit[2 * p + 1:2 * p + 2, :]
        bits = lax.bitcast_convert_type(acc, I32)
        key = bits ^ ((bits >> 31) & 0x7FFFFFFF)
        s_idx = kb * KB + lax.broadcasted_iota(I32, (KB, QB), 0)
        t_idx = i * QB + lax.broadcasted_iota(I32, (KB, QB), 1)
        adm = (s_idx >= FRONT_PAD) & (jnp.maximum(s_idx >> 6, 1) <= jnp.maximum(t_idx >> 6, 1))
        st_s[kb] = jnp.where(adm, key, INT_MIN)
        return carry

    lax.fori_loop(0, n_kb, score_body, 0)

    def count_ge(cand):
        def body(kb, c8):
            ind = jnp.where(st_s[kb] >= cand, 1, 0).astype(I32)
            return c8 + jnp.sum(ind.reshape(KB // SUBLANES, SUBLANES, QB), axis=0)
        c8 = lax.fori_loop(0, n_kb, body, jnp.zeros((SUBLANES, QB), I32))
        return jnp.sum(c8, axis=0, keepdims=True)

    def bit_body(it, tau):
        step = lax.shift_left(jnp.int32(1), jnp.int32(31) - it)
        cand = tau + step
        return jnp.where(count_ge(cand) >= topk, cand, tau)

    tau = lax.fori_loop(0, 32, bit_body, jnp.full((1, QB), INT_MIN, I32))
    tau = jnp.maximum(tau, INT_MIN + 1)

    m_s[...] = jnp.full(m_s.shape, NEG_INF, F32)
    l_s[...] = jnp.zeros(l_s.shape, F32)
    acc_s[...] = jnp.zeros(acc_s.shape, F32)

    def att_body(kb, carry):
        lg = _dot(cn_s[kb], qlat_s[...], _NT)
        sel = st_s[kb] >= tau
        v0 = pl.multiple_of(jnp.maximum(KB * kb - QB * i + BIAS_ORIGIN, 0), LANES)
        for h in range(H_B):
            cols = slice(h * QB, (h + 1) * QB)
            lh = lg[:, cols] + bias_ref[h, pl.ds(v0, KB), :]
            lh = jnp.where(sel, lh, NEG_INF)
            m_old = m_s[h:h + 1, :]
            m_new = jnp.maximum(m_old, jnp.max(lh, axis=0, keepdims=True))
            p = jnp.where(sel, jnp.exp(lh - m_new), 0.0)
            alpha = jnp.exp(m_old - m_new)
            l_s[h:h + 1, :] = alpha * l_s[h:h + 1, :] + jnp.sum(p, axis=0, keepdims=True)
            m_s[h:h + 1, :] = m_new
            pt_s[:, cols] = p.astype(BF16)
            acc_s[:, cols] = acc_s[:, cols] * alpha
        acc_s[...] += _dot(cnt_s[kb], pt_s[...])
        return carry

    lax.fori_loop(0, n_kb, att_body, 0)

    for h in range(H_B):
        cols = slice(h * QB, (h + 1) * QB)
        ot = (acc_s[:, cols] * (1.0 / l_s[h:h + 1, :])).astype(BF16)
        bt = _dot(wuvt_ref[h], ot)
        gate = gb_ref[:, cols].astype(F32)
        o_ref[:, cols] = (bt.T * (gate * _sigmoid(gate))).astype(o_ref.dtype)


def dsa_attention(proj, kv_gain, wuk_t, wuv_t, bias_tab, *, topk):
    bsz, tp, _ = proj.shape
    n_qb = tp // QB
    n_kb_all = -(-tp // KB)
    return pl.pallas_call(
        functools.partial(_attn_kernel, tp=tp, topk=topk),
        grid=(bsz, n_qb),
        in_specs=[
            pl.BlockSpec((None, tp, D_C), lambda b, i: (b, 0, E_COL_C // D_C)),
            pl.BlockSpec((None, tp, LANES), lambda b, i: (b, 0, E_COL_KK // LANES)),
            pl.BlockSpec((None, QB, LANES), lambda b, i: (b, i, E_COL_WI // LANES)),
            pl.BlockSpec((None, QB, H_B * DH_B), lambda b, i: (b, i, E_COL_Q // (H_B * DH_B))),
            pl.BlockSpec((None, QB, H_IDX * D_IDX), lambda b, i: (b, i, E_COL_QI // (H_IDX * D_IDX))),
            pl.BlockSpec((None, QB, H_B * DH_B), lambda b, i: (b, i, E_COL_GB // (H_B * DH_B))),
            pl.BlockSpec((1, D_C), lambda b, i: (0, 0)),
            pl.BlockSpec((H_B, DH_B, D_C), lambda b, i: (0, 0, 0)),
            pl.BlockSpec((H_B, DH_B, D_C), lambda b, i: (0, 0, 0)),
            pl.BlockSpec((H_B, BIAS_ROWS, LANES), lambda b, i: (0, 0, 0)),
        ],
        out_specs=pl.BlockSpec((None, QB, H_B * DH_B), lambda b, i: (b, i, 0)),
        out_shape=jax.ShapeDtypeStruct((bsz, tp, H_B * DH_B), BF16),
        scratch_shapes=[
            pltpu.VMEM((n_kb_all, KB, D_C), BF16),
            pltpu.VMEM((n_kb_all, D_C, KB), BF16),
            pltpu.VMEM((n_kb_all, KB, LANES), BF16),
            pltpu.VMEM((n_kb_all, KB, LANES), BF16),
            pltpu.VMEM((n_kb_all, KB, QB), I32),
            pltpu.VMEM((H_B * QB, D_C), BF16),
            pltpu.VMEM((KB, H_B * QB), BF16),
            pltpu.VMEM((D_C, H_B * QB), F32),
            pltpu.VMEM((H_B, QB), F32),
            pltpu.VMEM((H_B, QB), F32),
        ],
        compiler_params=_cparams(("parallel", "arbitrary"), 40),
        name="dsa_attention",
    )(proj, proj, proj, proj, proj, proj, kv_gain.reshape(1, D_C).astype(F32), wuk_t, wuv_t, bias_tab)


def _hgrn_constants():
    n = HC
    t = np.arange(n)[:, None]
    j = np.arange(n)[None, :]
    mats = [(j <= t), (j > t)]
    for lvl in range(N_LEVELS):
        upper = ((t >> lvl) & 1) == 1
        m_up = (t >> lvl) << lvl
        m_lo = ((t >> lvl) + 1) << lvl
        mats.append(np.where(upper, (j >= m_up) & (j <= t), (j > t) & (j < m_lo)))
    m = np.concatenate(mats, axis=0).astype(np.float32)
    mcat = np.concatenate([m, m], axis=1)
    x = t ^ j
    lev = np.where(j < t, np.floor(np.log2(np.maximum(x, 1))).astype(np.int32), np.where(j == t, -1, -2))
    return mcat, lev.astype(np.int32)


def _hgrn_kernel(q_ref, fz_ref, i_ref, g_ref, lbp_ref, ng_ref, mcat_ref, lev_ref, o_ref, s_ref, *, n_steps):
    log_lb = lbp_ref[0:1, :]
    log1m_lb = lbp_ref[1:2, :]
    one_m_lb = lbp_ref[2:3, :]
    ng = ng_ref[...]
    lev = lev_ref[...]
    row = lax.broadcasted_iota(I32, (HC, DK_C), 0)
    s_ref[...] = jnp.zeros(s_ref.shape, F32)

    def step(c, carry):
        rows = pl.ds(pl.multiple_of(c * HC, HC), HC)
        qx = q_ref[rows, :].astype(F32)
        qf = qx * _sigmoid(qx)
        z = fz_ref[rows, :].astype(F32)
        v = i_ref[rows, :]
        e = jnp.exp(-jnp.abs(z))
        r = 1.0 / (1.0 + e)
        k = one_m_lb * jnp.where(z >= 0, e * r, r)
        log_sig = jnp.minimum(z, 0.0) - jnp.log1p(e)
        ct = log1m_lb + log_sig
        lf = jnp.maximum(log_lb, ct) + jnp.log1p(jnp.exp(-jnp.abs(log_lb - ct)))
        hi = lf.astype(BF16)
        lo = (lf - hi.astype(F32)).astype(BF16)
        gsum = _dot(mcat_ref[...], jnp.concatenate([hi, lo], axis=0))
        ex = jnp.exp(gsum)
        e_b = ex[0:HC]
        e_s = ex[HC:2 * HC]
        st = s_ref[...]
        o = _dot((qf * e_b).astype(BF16), st.astype(BF16), _NT)
        a = jnp.where(lev == -1, _dot(qf.astype(BF16), k.astype(BF16), _NT), 0.0)
        for lvl in range(N_LEVELS):
            up = ((row >> lvl) & 1) == 1
            y = (jnp.where(up, qf, k) * ex[(2 + lvl) * HC:(3 + lvl) * HC]).astype(BF16)
            a = jnp.where(lev == lvl, _dot(y, y, _NT), a)
        o = o + _dot(a.astype(BF16), v)
        kt = (k * e_s).astype(BF16)
        vt = v.astype(F32).T.astype(BF16)
        s_ref[...] = st * e_b[HC - 1:HC, :] + _dot(vt, kt)
        on = o * lax.rsqrt(jnp.mean(o * o, axis=-1, keepdims=True) + RMS_EPS) * ng
        gx = g_ref[rows, :].astype(F32)
        o_ref[rows, :] = (on * (gx * _sigmoid(gx))).astype(o_ref.dtype)
        return carry

    lax.fori_loop(0, n_steps, step, 0)


def hgrn2(proj, lb_params, norm_g):
    bsz, tp, _ = proj.shape
    mcat, lev = _hgrn_constants()
    col = lambda off: pl.BlockSpec((None, tp, DK_C), lambda b, h, off=off: (b, 0, off + h))
    return pl.pallas_call(
        functools.partial(_hgrn_kernel, n_steps=tp // HC),
        grid=(bsz, H_C),
        in_specs=[
            col(0), col(H_C), col(2 * H_C), col(3 * H_C),
            pl.BlockSpec((3, DK_C), lambda b, h: (0, h)),
            pl.BlockSpec((1, DK_C), lambda b, h: (0, h)),
            pl.BlockSpec(mcat.shape, lambda b, h: (0, 0)),
            pl.BlockSpec(lev.shape, lambda b, h: (0, 0)),
        ],
        out_specs=pl.BlockSpec((None, tp, DK_C), lambda b, h: (b, 0, h)),
        out_shape=jax.ShapeDtypeStruct((bsz, tp, H_C * DK_C), BF16),
        scratch_shapes=[pltpu.VMEM((DK_C, DK_C), F32)],
        compiler_params=_cparams(("parallel", "parallel"), 32),
        name="hgrn2",
    )(proj, proj, proj, proj, lb_params, norm_g.reshape(1, -1).astype(F32),
      jnp.asarray(mcat, BF16), jnp.asarray(lev, I32))


def _t5_bucket(rel):
    nb = N_BUCKETS // 2
    ret = jnp.where(rel > 0, nb, 0)
    n = jnp.abs(rel)
    max_exact = nb // 2
    nf = jnp.maximum(n, 1).astype(F32)
    large = max_exact + (jnp.log(nf / max_exact) / math.log(MAX_DISTANCE / max_exact)
                         * (nb - max_exact)).astype(I32)
    large = jnp.minimum(large, nb - 1)
    return ret + jnp.where(n < max_exact, n, large)


def _bias_table(rel_bias_table):
    v = jnp.arange(BIAS_ROWS, dtype=I32)[:, None]
    q = jnp.arange(LANES, dtype=I32)[None, :]
    tab = rel_bias_table.astype(F32)[_t5_bucket(v - BIAS_ORIGIN - q)]
    return jnp.transpose(tab, (2, 0, 1))


def _even_in_weight(w):
    glu_v, glu_g, gate_a, q, c, gate_b, qi, ki, wi = jnp.split(
        w, np.cumsum([D_A, D_A, D_A, H_B * DH_B, D_C, H_B * DH_B, H_IDX * D_IDX, D_IDX])[:8].tolist(), axis=1)
    pad = jnp.zeros((w.shape[0], E_NPAD - E_COL_WI - H_IDX), w.dtype)
    cols = [glu_v, glu_g, gate_a, q, gate_b, qi * (D_IDX ** -0.5), c, ki, ki, wi * (H_IDX ** -0.5), pad]
    return jnp.concatenate(cols, axis=1).astype(BF16)


def kernel(x, meta_tokens, norm_gain, final_norm_gain, rel_bias_table, w_in_even, conv_w, conv_b,
           conv_ln_gain, conv_ln_bias, kv_norm_gain, w_uk, w_uv, w_out_even, w_in_odd, lb_logits,
           rec_norm_gain, w_out_odd):
    bsz, seq, d = x.shape
    assert d == D_MODEL and seq % LANES == 0
    depth = norm_gain.shape[0]
    tp = FRONT_PAD + N_META + seq
    topk = min(TOPK_MAX, seq // 4)
    tt_mm = tp // 2 if (tp // 2) % 16 == 0 else tp
    tt_out = tp // 4 if (tp // 4) % 16 == 0 else tp
    tt_conv = tp // 8 if (tp // 8) % CONV_RB == 0 else tp // 2 if (tp // 2) % CONV_RB == 0 else tp
    h = jnp.concatenate([
        jnp.zeros((bsz, FRONT_PAD, d), F32),
        jnp.broadcast_to(meta_tokens.astype(F32), (bsz, N_META, d)),
        x.astype(F32)], axis=1)

    lb_soft = jax.nn.softmax(lb_logits.astype(F32), axis=0)
    lower_bounds = jnp.cumsum(lb_soft, axis=0) - lb_soft[0]
    bias_tab = _bias_table(rel_bias_table)

    for layer in range(depth):
        last = layer == depth - 1
        fg = final_norm_gain if last else None
        if layer % 2 == 0:
            e = layer // 2
            proj = norm_matmul(h, norm_gain[layer], _even_in_weight(w_in_even[e]), tt=tt_mm, tn=512)
            a_out = conv_module(proj, conv_w[e], conv_b[e], conv_ln_gain[e], conv_ln_bias[e], tt=tt_conv)
            wuk_t = jnp.transpose(w_uk[e], (0, 2, 1)).astype(BF16)
            wuv_t = jnp.transpose(w_uv[e], (0, 2, 1)).astype(BF16)
            b_out = dsa_attention(proj, kv_norm_gain[e], wuk_t, wuv_t, bias_tab, topk=topk)
            h = out_proj([a_out, b_out], w_out_even[e].astype(BF16), h, fg, tt=tt_out)
        else:
            o = layer // 2
            lb = lower_bounds[layer]
            lb_params = jnp.stack([jnp.log(lb), jnp.log1p(-lb), 1.0 - lb], axis=0)
            proj = norm_matmul(h, norm_gain[layer], w_in_odd[o].astype(BF16), tt=tt_mm, tn=1024)
            rec = hgrn2(proj, lb_params, rec_norm_gain[o])
            h = out_proj([rec], w_out_odd[o].astype(BF16), h, fg, tt=tt_out)
    return h[:, FRONT_PAD + N_META:]
```

```python
import functools
import math

import numpy as np
import jax
import jax.numpy as jnp
from jax import lax
from jax.experimental import pallas as pl
from jax.experimental.pallas import tpu as pltpu

F32 = jnp.float32
BF16 = jnp.bfloat16
I32 = jnp.int32

D_MODEL = 2048
CHUNK = 64
N_META = 16
D_A = 1024
CONV_WIDTH = 31
H_B = 8
DH_B = 128
D_C = 256
H_IDX = 16
D_IDX = 64
TOPK_MAX = 256
N_BUCKETS = 32
MAX_DISTANCE = 128
H_C = 16
DK_C = 128
RMS_EPS = 1e-6
LN_EPS = 1e-5
NEG_INF = -1e30
INT_MIN = -(2 ** 31)
LOG2_E = 1.4426950408889634

LANES = 128
SUBLANES = 8
MXU_N = 256

FRONT_PAD = LANES - N_META
QB = LANES
KB = MXU_N
HC = 128
N_LEVELS = 7

E_NPAD = 6656
E_COL_Q, E_COL_GB, E_COL_QI = 3072, 4096, 5120
E_COL_C, E_COL_KK, E_COL_WI = 6144, 6400, 6528

MIB = 1024 * 1024


def _cparams(sem, vmem_mib):
    return pltpu.CompilerParams(dimension_semantics=sem, vmem_limit_bytes=vmem_mib * MIB)


def _sigmoid(x):
    return 1.0 / (1.0 + jnp.exp(-x))


_NN = (((1,), (0,)), ((), ()))
_NT = (((1,), (1,)), ((), ()))


def _dot(a, b, dims=_NN):
    return lax.dot_general(a, b, dims, preferred_element_type=F32)


def _norm_matmul_kernel(x_ref, g_ref, w_ref, o_ref, hn_ref):
    @pl.when(pl.program_id(2) == 0)
    def _():
        x = x_ref[...]
        ms = jnp.mean(x * x, axis=-1, keepdims=True)
        hn_ref[...] = (x * lax.rsqrt(ms + RMS_EPS) * g_ref[...]).astype(BF16)

    o_ref[...] = _dot(hn_ref[...], w_ref[...]).astype(o_ref.dtype)


def norm_matmul(h, gain, w, *, tt, tn):
    bsz, tp, d = h.shape
    n = w.shape[1]
    return pl.pallas_call(
        _norm_matmul_kernel,
        grid=(bsz, tp // tt, n // tn),
        in_specs=[
            pl.BlockSpec((None, tt, d), lambda b, i, j: (b, i, 0)),
            pl.BlockSpec((1, d), lambda b, i, j: (0, 0)),
            pl.BlockSpec((d, tn), lambda b, i, j: (0, j)),
        ],
        out_specs=pl.BlockSpec((None, tt, tn), lambda b, i, j: (b, i, j)),
        out_shape=jax.ShapeDtypeStruct((bsz, tp, n), BF16),
        scratch_shapes=[pltpu.VMEM((tt, d), BF16)],
        compiler_params=_cparams(("parallel", "parallel", "arbitrary"), 56),
        name="norm_matmul",
    )(h, gain.reshape(1, d).astype(F32), w)


def _out_proj_kernel(*refs, n_in, tt, final):
    xs = refs[:n_in]
    w_ref, h_ref = refs[n_in], refs[n_in + 1]
    g_ref = refs[n_in + 2] if final else None
    o_ref = refs[-1]
    y = None
    off = 0
    for x_ref in xs:
        kdim = x_ref.shape[-1]
        part = _dot(x_ref[...], w_ref[off:off + kdim, :])
        y = part if y is None else y + part
        off += kdim
    row = pl.program_id(1) * tt + lax.broadcasted_iota(I32, (tt, 1), 0)
    hn = jnp.where(row >= FRONT_PAD, h_ref[...] + y, 0.0)
    if final:
        ms = jnp.mean(hn * hn, axis=-1, keepdims=True)
        hn = hn * lax.rsqrt(ms + RMS_EPS) * g_ref[...]
    o_ref[...] = hn


def out_proj(xs, w, h, final_gain=None, *, tt):
    bsz, tp, d = h.shape
    final = final_gain is not None
    in_specs = [pl.BlockSpec((None, tt, x.shape[-1]), lambda b, i: (b, i, 0)) for x in xs]
    in_specs += [
        pl.BlockSpec(w.shape, lambda b, i: (0, 0)),
        pl.BlockSpec((None, tt, d), lambda b, i: (b, i, 0)),
    ]
    args = list(xs) + [w, h]
    if final:
        in_specs.append(pl.BlockSpec((1, d), lambda b, i: (0, 0)))
        args.append(final_gain.reshape(1, d).astype(F32))
    return pl.pallas_call(
        functools.partial(_out_proj_kernel, n_in=len(xs), tt=tt, final=final),
        grid=(bsz, tp // tt),
        in_specs=in_specs,
        out_specs=pl.BlockSpec((None, tt, d), lambda b, i: (b, i, 0)),
        out_shape=jax.ShapeDtypeStruct((bsz, tp, d), F32),
        compiler_params=_cparams(("parallel", "parallel"), 56),
        name="out_proj_final" if final else "out_proj",
    )(*args)


CONV_HALO = 32
CONV_RB = 16
CONV_CHAINS = 4


def _conv_kernel(v_ref, g_ref, ga_ref, w_ref, cb_ref, lg_ref, lb_ref, o_ref, us_ref, cv_ref, *, tt):
    j = pl.program_id(1)
    u_new = v_ref[...].astype(F32) * _sigmoid(g_ref[...].astype(F32))
    span = tt + CONV_HALO - SUBLANES
    n_rb = tt // CONV_RB
    for cb in range(D_A // LANES):
        lanes = slice(cb * LANES, (cb + 1) * LANES)

        @pl.when(j == 0)
        def _(cb=cb):
            us_ref[0, cb, 0:CONV_HALO, :] = jnp.zeros((CONV_HALO, LANES), F32)

        @pl.when(j > 0)
        def _(cb=cb):
            us_ref[0, cb, 0:CONV_HALO, :] = us_ref[0, cb, tt:tt + CONV_HALO, :]

        us_ref[0, cb, CONV_HALO:CONV_HALO + tt, :] = u_new[:, lanes]
        for r in range(1, SUBLANES):
            us_ref[r, cb, 0:span, :] = us_ref[0, cb, r:r + span, :]

        wb = [jnp.broadcast_to(w_ref[k:k + 1, lanes], (CONV_RB, LANES)) for k in range(CONV_WIDTH)]
        bias = jnp.broadcast_to(cb_ref[0:1, lanes], (CONV_RB, LANES))

        def rb_body(rb, carry, cb=cb, lanes=lanes, wb=wb, bias=bias):
            base = pl.multiple_of(rb * CONV_RB, CONV_RB)
            parts = [bias] + [None] * (CONV_CHAINS - 1)
            for k in range(CONV_WIDTH):
                a, r = divmod(k + CONV_HALO - (CONV_WIDTH - 1), SUBLANES)
                term = wb[k] * us_ref[r, cb, pl.ds(base + SUBLANES * a, CONV_RB), :]
                c = k % CONV_CHAINS
                parts[c] = term if parts[c] is None else parts[c] + term
            cv_ref[pl.ds(base, CONV_RB), lanes] = (parts[0] + parts[1]) + (parts[2] + parts[3])
            return carry

        lax.fori_loop(0, n_rb, rb_body, 0)

    x = cv_ref[...]
    mu = jnp.mean(x, axis=-1, keepdims=True)
    xc = x - mu
    var = jnp.mean(xc * xc, axis=-1, keepdims=True)
    y = xc * lax.rsqrt(var + LN_EPS) * lg_ref[...] + lb_ref[...]
    ga = ga_ref[...].astype(F32)
    o_ref[...] = (y * _sigmoid(y) * (ga * _sigmoid(ga))).astype(o_ref.dtype)


def conv_module(proj, conv_w, conv_b, ln_g, ln_b, *, tt):
    bsz, tp, _ = proj.shape
    row = lambda a: a.reshape(1, D_A).astype(F32)
    return pl.pallas_call(
        functools.partial(_conv_kernel, tt=tt),
        grid=(bsz, tp // tt),
        in_specs=[
            pl.BlockSpec((None, tt, D_A), lambda b, j: (b, j, 0)),
            pl.BlockSpec((None, tt, D_A), lambda b, j: (b, j, 1)),
            pl.BlockSpec((None, tt, D_A), lambda b, j: (b, j, 2)),
            pl.BlockSpec((CONV_WIDTH, D_A), lambda b, j: (0, 0)),
            pl.BlockSpec((1, D_A), lambda b, j: (0, 0)),
            pl.BlockSpec((1, D_A), lambda b, j: (0, 0)),
            pl.BlockSpec((1, D_A), lambda b, j: (0, 0)),
        ],
        out_specs=pl.BlockSpec((None, tt, D_A), lambda b, j: (b, j, 0)),
        out_shape=jax.ShapeDtypeStruct((bsz, tp, D_A), BF16),
        scratch_shapes=[
            pltpu.VMEM((SUBLANES, D_A // LANES, tt + CONV_HALO, LANES), F32),
            pltpu.VMEM((tt, D_A), F32),
        ],
        compiler_params=_cparams(("parallel", "arbitrary"), 40),
        name="conv_module",
    )(proj, proj, proj, conv_w.astype(F32), row(conv_b), row(ln_g), row(ln_b))


BIAS_ROWS = 640
BIAS_ORIGIN = 384


def _attn_kernel(c_ref, kk_ref, wi_ref, q_ref, qi_ref, gb_ref, kvg_ref, wuk_ref, wuvt_ref, bias_ref,
                 o_ref, cn_s, cnt_s, kk_s, st_s, qm_s, qlat_s, acc_s, m_s, l_s, *, tp, topk):
    i = pl.program_id(1)
    n_kb = (i + 2) // 2
    n_kb_all = cn_s.shape[0]

    @pl.when(i == 0)
    def _prepare_keys():
        gain = kvg_ref[...]
        for kb in range(n_kb_all):
            lo = kb * KB
            nrow = min(KB, tp - lo)
            c = c_ref[lo:lo + nrow, :].astype(F32)
            cn = c * lax.rsqrt(jnp.mean(c * c, axis=-1, keepdims=True) + RMS_EPS) * gain
            cn_s[kb, 0:nrow, :] = cn.astype(BF16)
            kk_s[kb, 0:nrow, :] = kk_ref[lo:lo + nrow, :]
            for rb in range(nrow // LANES):
                for cb in range(D_C // LANES):
                    tile = cn[rb * LANES:(rb + 1) * LANES, cb * LANES:(cb + 1) * LANES]
                    cnt_s[kb, cb * LANES:(cb + 1) * LANES, rb * LANES:(rb + 1) * LANES] = tile.T.astype(BF16)
            if nrow < KB:
                cn_s[kb, nrow:KB, :] = jnp.zeros((KB - nrow, D_C), BF16)
                kk_s[kb, nrow:KB, :] = jnp.zeros((KB - nrow, LANES), BF16)
                cnt_s[kb, :, nrow:KB] = jnp.zeros((D_C, KB - nrow), BF16)

    for h in range(H_B):
        qh = q_ref[:, h * DH_B:(h + 1) * DH_B]
        ql = _dot(qh, wuk_ref[h]) * (DH_B ** -0.5 * LOG2_E)
        qlat_s[h * QB:(h + 1) * QB, :] = ql.astype(BF16)

    lane = lax.broadcasted_iota(I32, (QB, LANES), 1)
    for p in range(H_IDX // 2):
        qp = qi_ref[:, p * LANES:(p + 1) * LANES]
        zero = jnp.zeros_like(qp)
        qm_s[p, 0:QB, :] = jnp.where(lane < D_IDX, qp, zero)
        qm_s[p, QB:2 * QB, :] = jnp.where(lane >= D_IDX, qp, zero)

    wit = wi_ref[...].astype(F32).T

    def score_body(kb, carry):
        kk = kk_s[kb]
        acc = jnp.zeros((KB, QB), F32)
        for p in range(H_IDX // 2):
            d2 = _dot(kk, qm_s[p], _NT)
            acc = acc + jnp.maximum(d2[:, 0:QB], 0.0) * wit[2 * p:2 * p + 1, :]
            acc = acc + jnp.maximum(d2[:, QB:2 * QB], 0.0) * wit[2 * p + 1:2 * p + 2, :]
        bits = lax.bitcast_convert_type(acc, I32)
        key = bits ^ ((bits >> 31) & 0x7FFFFFFF)
        s_idx = kb * KB + lax.broadcasted_iota(I32, (KB, QB), 0)
        t_idx = i * QB + lax.broadcasted_iota(I32, (KB, QB), 1)
        adm = (s_idx >= FRONT_PAD) & (jnp.maximum(s_idx >> 6, 1) <= jnp.maximum(t_idx >> 6, 1))
        st_s[kb] = jnp.where(adm, key, INT_MIN)
        return carry

    lax.fori_loop(0, n_kb, score_body, 0)

    def count_ge(cand):
        def body(kb, c8):
            ind = jnp.where(st_s[kb] >= cand, 1, 0).astype(I32)
            return c8 + jnp.sum(ind.reshape(KB // SUBLANES, SUBLANES, QB), axis=0)
        c8 = lax.fori_loop(0, n_kb, body, jnp.zeros((SUBLANES, QB), I32))
        return jnp.sum(c8, axis=0, keepdims=True)

    def bit_body(it, tau):
        step = lax.shift_left(jnp.int32(1), jnp.int32(31) - it)
        cand = tau + step
        return jnp.where(count_ge(cand) >= topk, cand, tau)

    tau = lax.fori_loop(0, 32, bit_body, jnp.full((1, QB), INT_MIN, I32))
    tau = jnp.maximum(tau, INT_MIN + 1)

    m_s[...] = jnp.full(m_s.shape, NEG_INF, F32)
    l_s[...] = jnp.zeros(l_s.shape, F32)
    acc_s[...] = jnp.zeros(acc_s.shape, F32)

    def att_body(kb, carry):
        neg = jnp.where(st_s[kb] >= tau, 0.0, NEG_INF)
        v0 = pl.multiple_of(jnp.maximum(KB * kb - QB * i + BIAS_ORIGIN, 0), LANES)
        lg = _dot(cn_s[kb], qlat_s[...], _NT)
        ps, alphas = [], []
        for h in range(H_B):
            lh = lg[:, h * QB:(h + 1) * QB] + bias_ref[h, pl.ds(v0, KB), :] + neg
            m_old = m_s[h:h + 1, :]
            m_new = jnp.maximum(m_old, jnp.max(lh, axis=0, keepdims=True))
            p = jnp.exp2(lh - m_new)
            alpha = jnp.exp2(m_old - m_new)
            l_s[h:h + 1, :] = alpha * l_s[h:h + 1, :] + jnp.sum(p, axis=0, keepdims=True)
            m_s[h:h + 1, :] = m_new
            ps.append(p.astype(BF16))
            alphas.append(alpha)
        pv = _dot(cnt_s[kb], jnp.concatenate(ps, axis=1))
        acc_s[...] = acc_s[...] * jnp.concatenate(alphas, axis=1) + pv
        return carry

    lax.fori_loop(0, n_kb, att_body, 0)

    for h in range(H_B):
        cols = slice(h * QB, (h + 1) * QB)
        ot = (acc_s[:, cols] * (1.0 / l_s[h:h + 1, :])).astype(BF16)
        bt = _dot(wuvt_ref[h], ot)
        gate = gb_ref[:, cols].astype(F32)
        o_ref[:, cols] = (bt.T * (gate * _sigmoid(gate))).astype(o_ref.dtype)


def dsa_attention(proj, kv_gain, wuk_t, wuv_t, bias_tab, *, topk):
    bsz, tp, _ = proj.shape
    n_qb = tp // QB
    n_kb_all = -(-tp // KB)
    return pl.pallas_call(
        functools.partial(_attn_kernel, tp=tp, topk=topk),
        grid=(bsz, n_qb),
        in_specs=[
            pl.BlockSpec((None, tp, D_C), lambda b, i: (b, 0, E_COL_C // D_C)),
            pl.BlockSpec((None, tp, LANES), lambda b, i: (b, 0, E_COL_KK // LANES)),
            pl.BlockSpec((None, QB, LANES), lambda b, i: (b, i, E_COL_WI // LANES)),
            pl.BlockSpec((None, QB, H_B * DH_B), lambda b, i: (b, i, E_COL_Q // (H_B * DH_B))),
            pl.BlockSpec((None, QB, H_IDX * D_IDX), lambda b, i: (b, i, E_COL_QI // (H_IDX * D_IDX))),
            pl.BlockSpec((None, QB, H_B * DH_B), lambda b, i: (b, i, E_COL_GB // (H_B * DH_B))),
            pl.BlockSpec((1, D_C), lambda b, i: (0, 0)),
            pl.BlockSpec((H_B, DH_B, D_C), lambda b, i: (0, 0, 0)),
            pl.BlockSpec((H_B, DH_B, D_C), lambda b, i: (0, 0, 0)),
            pl.BlockSpec((H_B, BIAS_ROWS, LANES), lambda b, i: (0, 0, 0)),
        ],
        out_specs=pl.BlockSpec((None, QB, H_B * DH_B), lambda b, i: (b, i, 0)),
        out_shape=jax.ShapeDtypeStruct((bsz, tp, H_B * DH_B), BF16),
        scratch_shapes=[
            pltpu.VMEM((n_kb_all, KB, D_C), BF16),
            pltpu.VMEM((n_kb_all, D_C, KB), BF16),
            pltpu.VMEM((n_kb_all, KB, LANES), BF16),
            pltpu.VMEM((n_kb_all, KB, QB), I32),
            pltpu.VMEM((H_IDX // 2, 2 * QB, LANES), BF16),
            pltpu.VMEM((H_B * QB, D_C), BF16),
            pltpu.VMEM((D_C, H_B * QB), F32),
            pltpu.VMEM((H_B, QB), F32),
            pltpu.VMEM((H_B, QB), F32),
        ],
        compiler_params=_cparams(("parallel", "arbitrary"), 40),
        name="dsa_attention",
    )(proj, proj, proj, proj, proj, proj, kv_gain.reshape(1, D_C).astype(F32), wuk_t, wuv_t, bias_tab)


def _hgrn_constants():
    n = HC
    t = np.arange(n)[:, None]
    j = np.arange(n)[None, :]
    mats = [(j <= t), (j > t)]
    for lvl in range(N_LEVELS):
        upper = ((t >> lvl) & 1) == 1
        m_up = (t >> lvl) << lvl
        m_lo = ((t >> lvl) + 1) << lvl
        mats.append(np.where(upper, (j >= m_up) & (j <= t), (j > t) & (j < m_lo)))
    m = np.concatenate(mats, axis=0).astype(np.float32)
    mcat = np.concatenate([m, m], axis=1)
    x = t ^ j
    lev = np.where(j < t, np.floor(np.log2(np.maximum(x, 1))).astype(np.int32), np.where(j == t, -1, -2))
    return mcat, lev.astype(np.int32)


HGRN_HEADS_PER_STEP = 4


def _role_select(qf, k, lvl, up_small):
    half = 1 << lvl
    if half >= SUBLANES:
        pieces = []
        for blk in range(HC // (2 * half)):
            lo = 2 * half * blk
            pieces += [k[lo:lo + half], qf[lo + half:lo + 2 * half]]
        return jnp.concatenate(pieces, axis=0)
    g, w = HC // SUBLANES, qf.shape[-1]
    sel = jnp.where(up_small[lvl], qf.reshape(g, SUBLANES, w), k.reshape(g, SUBLANES, w))
    return sel.reshape(HC, w)


_SLOT_QE, _SLOT_KT, _SLOT_QB, _SLOT_KB, _SLOT_Y0 = 0, 1, 2, 3, 4
_N_SLOTS = _SLOT_Y0 + N_LEVELS


def _hgrn_kernel(q_ref, fz_ref, i_ref, g_ref, lbp_ref, ng_ref, mcat_ref, lev_ref, o_ref,
                 s_ref, stage_a, stage_b, dec_a, dec_b, *, n_steps, nh):
    wide = nh * DK_C
    lev = lev_ref[...]
    row8 = lax.broadcasted_iota(I32, (1, SUBLANES, wide), 1)
    up_small = [((row8 >> lvl) & 1) == 1 for lvl in range(3)]
    s_ref[...] = jnp.zeros(s_ref.shape, F32)

    def elementwise_stage(c, stage, dec_ref):
        rows = pl.ds(pl.multiple_of(c * HC, HC), HC)
        log_lb = lbp_ref[0:1, :]
        log1m_lb = lbp_ref[1:2, :]
        one_m_lb = lbp_ref[2:3, :]
        qx = q_ref[rows, :].astype(F32)
        qf = qx * _sigmoid(qx)
        z = fz_ref[rows, :].astype(F32)
        e = jnp.exp(-jnp.abs(z))
        r = 1.0 / (1.0 + e)
        k = one_m_lb * jnp.where(z >= 0, e * r, r)
        log_sig = jnp.minimum(z, 0.0) - jnp.log(1.0 + e)
        ct = log1m_lb + log_sig
        lf = jnp.maximum(log_lb, ct) + jnp.log(1.0 + jnp.exp(-jnp.abs(log_lb - ct)))
        lf2 = lf * LOG2_E
        hi = lf2.astype(BF16)
        lo = (lf2 - hi.astype(F32)).astype(BF16)
        gsum = _dot(mcat_ref[...], jnp.concatenate([hi, lo], axis=0))
        ex = jnp.exp2(gsum)
        e_b = ex[0:HC]
        stage[_SLOT_QE] = (qf * e_b).astype(BF16)
        stage[_SLOT_KT] = (k * ex[HC:2 * HC]).astype(BF16)
        stage[_SLOT_QB] = qf.astype(BF16)
        stage[_SLOT_KB] = k.astype(BF16)
        for lvl in range(N_LEVELS):
            y = _role_select(qf, k, lvl, up_small) * ex[(2 + lvl) * HC:(3 + lvl) * HC]
            stage[_SLOT_Y0 + lvl] = y.astype(BF16)
        dec_ref[...] = e_b[HC - 1:HC, :]

    def matmul_stage(c, stage, dec_ref):
        rows = pl.ds(pl.multiple_of(c * HC, HC), HC)
        outs = []
        for hh in range(nh):
            ln = slice(hh * DK_C, (hh + 1) * DK_C)
            st = s_ref[hh]
            o = _dot(stage[_SLOT_QE, :, ln], st.astype(BF16), _NT)
            a = jnp.where(lev == -1, _dot(stage[_SLOT_QB, :, ln], stage[_SLOT_KB, :, ln], _NT), 0.0)
            for lvl in range(N_LEVELS):
                y = stage[_SLOT_Y0 + lvl, :, ln]
                a = jnp.where(lev == lvl, _dot(y, y, _NT), a)
            vh = i_ref[rows, ln]
            o = o + _dot(a.astype(BF16), vh)
            vt = vh.astype(F32).T.astype(BF16)
            s_ref[hh] = st * dec_ref[:, ln] + _dot(vt, stage[_SLOT_KT, :, ln])
            outs.append(o * lax.rsqrt(jnp.mean(o * o, axis=-1, keepdims=True) + RMS_EPS))
        on = jnp.concatenate(outs, axis=1) * ng_ref[...]
        gx = g_ref[rows, :].astype(F32)
        o_ref[rows, :] = (on * (gx * _sigmoid(gx))).astype(o_ref.dtype)

    last = n_steps - 1
    elementwise_stage(0, stage_a, dec_a)

    def pair(j, carry):
        c = 2 * j
        matmul_stage(c, stage_a, dec_a)
        elementwise_stage(c + 1, stage_b, dec_b)
        matmul_stage(c + 1, stage_b, dec_b)
        elementwise_stage(jnp.minimum(c + 2, last), stage_a, dec_a)
        return carry

    lax.fori_loop(0, n_steps // 2, pair, 0)
    if n_steps % 2:
        matmul_stage(last, stage_a, dec_a)


def hgrn2(proj, lb_params, norm_g):
    bsz, tp, _ = proj.shape
    mcat, lev = _hgrn_constants()
    nh = HGRN_HEADS_PER_STEP
    wide = nh * DK_C
    n_groups = H_C // nh
    col = lambda off: pl.BlockSpec((None, tp, wide), lambda b, h, off=off: (b, 0, off + h))
    return pl.pallas_call(
        functools.partial(_hgrn_kernel, n_steps=tp // HC, nh=nh),
        grid=(bsz, n_groups),
        in_specs=[
            col(0), col(n_groups), col(2 * n_groups), col(3 * n_groups),
            pl.BlockSpec((3, wide), lambda b, h: (0, h)),
            pl.BlockSpec((1, wide), lambda b, h: (0, h)),
            pl.BlockSpec(mcat.shape, lambda b, h: (0, 0)),
            pl.BlockSpec(lev.shape, lambda b, h: (0, 0)),
        ],
        out_specs=pl.BlockSpec((None, tp, wide), lambda b, h: (b, 0, h)),
        out_shape=jax.ShapeDtypeStruct((bsz, tp, H_C * DK_C), BF16),
        scratch_shapes=[
            pltpu.VMEM((nh, DK_C, DK_C), F32),
            pltpu.VMEM((_N_SLOTS, HC, wide), BF16),
            pltpu.VMEM((_N_SLOTS, HC, wide), BF16),
            pltpu.VMEM((1, wide), F32),
            pltpu.VMEM((1, wide), F32),
        ],
        compiler_params=_cparams(("parallel", "parallel"), 40),
        name="hgrn2",
    )(proj, proj, proj, proj, lb_params, norm_g.reshape(1, -1).astype(F32),
      jnp.asarray(mcat, BF16), jnp.asarray(lev, I32))


def _t5_bucket(rel):
    nb = N_BUCKETS // 2
    ret = jnp.where(rel > 0, nb, 0)
    n = jnp.abs(rel)
    max_exact = nb // 2
    nf = jnp.maximum(n, 1).astype(F32)
    large = max_exact + (jnp.log(nf / max_exact) / math.log(MAX_DISTANCE / max_exact)
                         * (nb - max_exact)).astype(I32)
    large = jnp.minimum(large, nb - 1)
    return ret + jnp.where(n < max_exact, n, large)


def _bias_table(rel_bias_table):
    v = jnp.arange(BIAS_ROWS, dtype=I32)[:, None]
    q = jnp.arange(LANES, dtype=I32)[None, :]
    onehot = jax.nn.one_hot(_t5_bucket(v - BIAS_ORIGIN - q), N_BUCKETS, dtype=F32)
    tab = jnp.einsum("vqb,bh->hvq", onehot, rel_bias_table.astype(F32), precision=lax.Precision.HIGHEST)
    return tab * LOG2_E


def _even_in_weight(w):
    glu_v, glu_g, gate_a, q, c, gate_b, qi, ki, wi = jnp.split(
        w, np.cumsum([D_A, D_A, D_A, H_B * DH_B, D_C, H_B * DH_B, H_IDX * D_IDX, D_IDX])[:8].tolist(), axis=1)
    pad = jnp.zeros((w.shape[0], E_NPAD - E_COL_WI - H_IDX), w.dtype)
    cols = [glu_v, glu_g, gate_a, q, gate_b, qi * (D_IDX ** -0.5), c, ki, ki, wi * (H_IDX ** -0.5), pad]
    return jnp.concatenate(cols, axis=1).astype(BF16)


def kernel(x, meta_tokens, norm_gain, final_norm_gain, rel_bias_table, w_in_even, conv_w, conv_b,
           conv_ln_gain, conv_ln_bias, kv_norm_gain, w_uk, w_uv, w_out_even, w_in_odd, lb_logits,
           rec_norm_gain, w_out_odd):
    bsz, seq, d = x.shape
    assert d == D_MODEL and seq % LANES == 0
    depth = norm_gain.shape[0]
    tp = FRONT_PAD + N_META + seq
    topk = min(TOPK_MAX, seq // 4)
    tt_mm = tp // 2 if (tp // 2) % 16 == 0 else tp
    tt_out = tp // 4 if (tp // 4) % 16 == 0 else tp
    tt_conv = tp // 8 if (tp // 8) % CONV_RB == 0 else tp // 2 if (tp // 2) % CONV_RB == 0 else tp
    h = jnp.concatenate([
        jnp.zeros((bsz, FRONT_PAD, d), F32),
        jnp.broadcast_to(meta_tokens.astype(F32), (bsz, N_META, d)),
        x.astype(F32)], axis=1)

    lb_soft = jax.nn.softmax(lb_logits.astype(F32), axis=0)
    lower_bounds = jnp.cumsum(lb_soft, axis=0) - lb_soft[0]
    bias_tab = _bias_table(rel_bias_table)

    for layer in range(depth):
        last = layer == depth - 1
        fg = final_norm_gain if last else None
        if layer % 2 == 0:
            e = layer // 2
            proj = norm_matmul(h, norm_gain[layer], _even_in_weight(w_in_even[e]), tt=tt_mm, tn=E_NPAD // 4)
            a_out = conv_module(proj, conv_w[e], conv_b[e], conv_ln_gain[e], conv_ln_bias[e], tt=tt_conv)
            wuk_t = jnp.transpose(w_uk[e], (0, 2, 1)).astype(BF16)
            wuv_t = jnp.transpose(w_uv[e], (0, 2, 1)).astype(BF16)
            b_out = dsa_attention(proj, kv_norm_gain[e], wuk_t, wuv_t, bias_tab, topk=topk)
            h = out_proj([a_out, b_out], w_out_even[e].astype(BF16), h, fg, tt=tt_out)
        else:
            o = layer // 2
            lb = lower_bounds[layer]
            lb_params = jnp.stack([jnp.log(lb), jnp.log1p(-lb), 1.0 - lb], axis=0)
            proj = norm_matmul(h, norm_gain[layer], w_in_odd[o].astype(BF16), tt=tt_mm, tn=2048)
            rec = hgrn2(proj, lb_params, rec_norm_gain[o])
            h = out_proj([rec], w_out_odd[o].astype(BF16), h, fg, tt=tt_out)
    return h[:, FRONT_PAD + N_META:]
```

```python
import functools
import math

import numpy as np
import jax
import jax.numpy as jnp
from jax import lax
from jax.experimental import pallas as pl
from jax.experimental.pallas import tpu as pltpu

F32 = jnp.float32
BF16 = jnp.bfloat16
I32 = jnp.int32
I16 = jnp.int16

D_MODEL = 2048
CHUNK = 64
N_META = 16
D_A = 1024
CONV_WIDTH = 31
H_B = 8
DH_B = 128
D_C = 256
H_IDX = 16
D_IDX = 64
TOPK_MAX = 256
N_BUCKETS = 32
MAX_DISTANCE = 128
H_C = 16
DK_C = 128
RMS_EPS = 1e-6
LN_EPS = 1e-5
NEG_INF = -1e30
INT_MIN = -(2 ** 31)
LOG2_E = 1.4426950408889634

LANES = 128
SUBLANES = 8
PACK16 = 16
I16_MIN = -(2 ** 15)
MXU_N = 256

FRONT_PAD = LANES - N_META
QB = LANES
KB = MXU_N
HC = 128
N_LEVELS = 7

E_NPAD = 6656
E_COL_Q, E_COL_GB, E_COL_QI = 3072, 4096, 5120
E_COL_C, E_COL_KK, E_COL_WI = 6144, 6400, 6528

MIB = 1024 * 1024


def _cparams(sem, vmem_mib, flags=None):
    return pltpu.CompilerParams(dimension_semantics=sem, vmem_limit_bytes=vmem_mib * MIB, flags=flags)


def _sigmoid(x):
    return 1.0 / (1.0 + jnp.exp(-x))


def _silu(x):
    h = 0.5 * x
    return h + h * jnp.tanh(h)


_NN = (((1,), (0,)), ((), ()))
_NT = (((1,), (1,)), ((), ()))


def _dot(a, b, dims=_NN):
    return lax.dot_general(a, b, dims, preferred_element_type=F32)


def _norm_matmul_kernel(x_ref, g_ref, w_ref, o_ref, hn_ref):
    @pl.when(pl.program_id(2) == 0)
    def _():
        x = x_ref[...]
        ms = jnp.mean(x * x, axis=-1, keepdims=True)
        hn_ref[...] = (x * lax.rsqrt(ms + RMS_EPS) * g_ref[...]).astype(BF16)

    o_ref[...] = _dot(hn_ref[...], w_ref[...]).astype(o_ref.dtype)


def norm_matmul(h, gain, w, *, tt, tn):
    bsz, tp, d = h.shape
    n = w.shape[1]
    return pl.pallas_call(
        _norm_matmul_kernel,
        grid=(bsz, tp // tt, n // tn),
        in_specs=[
            pl.BlockSpec((None, tt, d), lambda b, i, j: (b, i, 0)),
            pl.BlockSpec((1, d), lambda b, i, j: (0, 0)),
            pl.BlockSpec((d, tn), lambda b, i, j: (0, j)),
        ],
        out_specs=pl.BlockSpec((None, tt, tn), lambda b, i, j: (b, i, j)),
        out_shape=jax.ShapeDtypeStruct((bsz, tp, n), BF16),
        scratch_shapes=[pltpu.VMEM((tt, d), BF16)],
        compiler_params=_cparams(("parallel", "parallel", "arbitrary"), 56),
        name="norm_matmul",
    )(h, gain.reshape(1, d).astype(F32), w)


def _out_proj_kernel(*refs, n_in, tt, final):
    xs = refs[:n_in]
    w_ref, h_ref = refs[n_in], refs[n_in + 1]
    g_ref = refs[n_in + 2] if final else None
    o_ref = refs[-1]
    y = None
    off = 0
    for x_ref in xs:
        kdim = x_ref.shape[-1]
        part = _dot(x_ref[...], w_ref[off:off + kdim, :])
        y = part if y is None else y + part
        off += kdim
    row = pl.program_id(1) * tt + lax.broadcasted_iota(I32, (tt, 1), 0)
    hn = jnp.where(row >= FRONT_PAD, h_ref[...] + y, 0.0)
    if final:
        ms = jnp.mean(hn * hn, axis=-1, keepdims=True)
        hn = hn * lax.rsqrt(ms + RMS_EPS) * g_ref[...]
    o_ref[...] = hn


def out_proj(xs, w, h, final_gain=None, *, tt):
    bsz, tp, d = h.shape
    final = final_gain is not None
    in_specs = [pl.BlockSpec((None, tt, x.shape[-1]), lambda b, i: (b, i, 0)) for x in xs]
    in_specs += [
        pl.BlockSpec(w.shape, lambda b, i: (0, 0)),
        pl.BlockSpec((None, tt, d), lambda b, i: (b, i, 0)),
    ]
    args = list(xs) + [w, h]
    if final:
        in_specs.append(pl.BlockSpec((1, d), lambda b, i: (0, 0)))
        args.append(final_gain.reshape(1, d).astype(F32))
    return pl.pallas_call(
        functools.partial(_out_proj_kernel, n_in=len(xs), tt=tt, final=final),
        grid=(bsz, tp // tt),
        in_specs=in_specs,
        out_specs=pl.BlockSpec((None, tt, d), lambda b, i: (b, i, 0)),
        out_shape=jax.ShapeDtypeStruct((bsz, tp, d), F32),
        compiler_params=_cparams(("parallel", "parallel"), 56),
        name="out_proj_final" if final else "out_proj",
    )(*args)


CONV_HALO = 32
CONV_RB = 16
CONV_CHAINS = 4


def _conv_kernel(v_ref, g_ref, ga_ref, w_ref, cb_ref, lg_ref, lb_ref, o_ref, us_ref, cv_ref, *, tt):
    j = pl.program_id(1)
    u_new = v_ref[...].astype(F32) * _sigmoid(g_ref[...].astype(F32))
    span = tt + CONV_HALO - SUBLANES
    n_rb = tt // CONV_RB
    for cb in range(D_A // LANES):
        lanes = slice(cb * LANES, (cb + 1) * LANES)

        @pl.when(j == 0)
        def _(cb=cb):
            us_ref[0, cb, 0:CONV_HALO, :] = jnp.zeros((CONV_HALO, LANES), F32)

        @pl.when(j > 0)
        def _(cb=cb):
            us_ref[0, cb, 0:CONV_HALO, :] = us_ref[0, cb, tt:tt + CONV_HALO, :]

        us_ref[0, cb, CONV_HALO:CONV_HALO + tt, :] = u_new[:, lanes]
        for r in range(1, SUBLANES):
            us_ref[r, cb, 0:span, :] = us_ref[0, cb, r:r + span, :]

        wb = [jnp.broadcast_to(w_ref[k:k + 1, lanes], (CONV_RB, LANES)) for k in range(CONV_WIDTH)]
        bias = jnp.broadcast_to(cb_ref[0:1, lanes], (CONV_RB, LANES))

        def rb_body(rb, carry, cb=cb, lanes=lanes, wb=wb, bias=bias):
            base = pl.multiple_of(rb * CONV_RB, CONV_RB)
            parts = [bias] + [None] * (CONV_CHAINS - 1)
            for k in range(CONV_WIDTH):
                a, r = divmod(k + CONV_HALO - (CONV_WIDTH - 1), SUBLANES)
                term = wb[k] * us_ref[r, cb, pl.ds(base + SUBLANES * a, CONV_RB), :]
                c = k % CONV_CHAINS
                parts[c] = term if parts[c] is None else parts[c] + term
            cv_ref[pl.ds(base, CONV_RB), lanes] = (parts[0] + parts[1]) + (parts[2] + parts[3])
            return carry

        lax.fori_loop(0, n_rb, rb_body, 0)

    x = cv_ref[...]
    mu = jnp.mean(x, axis=-1, keepdims=True)
    xc = x - mu
    var = jnp.mean(xc * xc, axis=-1, keepdims=True)
    y = xc * lax.rsqrt(var + LN_EPS) * lg_ref[...] + lb_ref[...]
    ga = ga_ref[...].astype(F32)
    o_ref[...] = (_silu(y) * _silu(ga)).astype(o_ref.dtype)


def conv_module(proj, conv_w, conv_b, ln_g, ln_b, *, tt):
    bsz, tp, _ = proj.shape
    row = lambda a: a.reshape(1, D_A).astype(F32)
    return pl.pallas_call(
        functools.partial(_conv_kernel, tt=tt),
        grid=(bsz, tp // tt),
        in_specs=[
            pl.BlockSpec((None, tt, D_A), lambda b, j: (b, j, 0)),
            pl.BlockSpec((None, tt, D_A), lambda b, j: (b, j, 1)),
            pl.BlockSpec((None, tt, D_A), lambda b, j: (b, j, 2)),
            pl.BlockSpec((CONV_WIDTH, D_A), lambda b, j: (0, 0)),
            pl.BlockSpec((1, D_A), lambda b, j: (0, 0)),
            pl.BlockSpec((1, D_A), lambda b, j: (0, 0)),
            pl.BlockSpec((1, D_A), lambda b, j: (0, 0)),
        ],
        out_specs=pl.BlockSpec((None, tt, D_A), lambda b, j: (b, j, 0)),
        out_shape=jax.ShapeDtypeStruct((bsz, tp, D_A), BF16),
        scratch_shapes=[
            pltpu.VMEM((SUBLANES, D_A // LANES, tt + CONV_HALO, LANES), F32),
            pltpu.VMEM((tt, D_A), F32),
        ],
        compiler_params=_cparams(("parallel", "arbitrary"), 40),
        name="conv_module",
    )(proj, proj, proj, conv_w.astype(F32), row(conv_b), row(ln_g), row(ln_b))


BIAS_ROWS = 640
BIAS_ORIGIN = 384


def _attn_kernel(c_ref, kk_ref, wi_ref, q_ref, qi_ref, gb_ref, kvg_ref, wuk_ref, wuvt_ref, bias_ref,
                 o_ref, cn_s, cnt_s, kk_s, st_s, hi_s, lo_s, qm_s, qlat_s, lg_a, lg_b, acc_s, m_s, l_s,
                 *, tp, topk):
    i = pl.program_id(1)
    n_kb = (i + 2) // 2
    n_kb_all = cn_s.shape[0]

    @pl.when(i == 0)
    def _prepare_keys():
        gain = kvg_ref[...]
        for kb in range(n_kb_all):
            lo = kb * KB
            nrow = min(KB, tp - lo)
            c = c_ref[lo:lo + nrow, :].astype(F32)
            cn = c * lax.rsqrt(jnp.mean(c * c, axis=-1, keepdims=True) + RMS_EPS) * gain
            cn_s[kb, 0:nrow, :] = cn.astype(BF16)
            kk_s[kb, 0:nrow, :] = kk_ref[lo:lo + nrow, :]
            for rb in range(nrow // LANES):
                for cb in range(D_C // LANES):
                    tile = cn[rb * LANES:(rb + 1) * LANES, cb * LANES:(cb + 1) * LANES]
                    cnt_s[kb, cb * LANES:(cb + 1) * LANES, rb * LANES:(rb + 1) * LANES] = tile.T.astype(BF16)
            if nrow < KB:
                cn_s[kb, nrow:KB, :] = jnp.zeros((KB - nrow, D_C), BF16)
                kk_s[kb, nrow:KB, :] = jnp.zeros((KB - nrow, LANES), BF16)
                cnt_s[kb, :, nrow:KB] = jnp.zeros((D_C, KB - nrow), BF16)

    for h in range(H_B):
        qh = q_ref[:, h * DH_B:(h + 1) * DH_B]
        ql = _dot(qh, wuk_ref[h]) * (DH_B ** -0.5 * LOG2_E)
        qlat_s[h * QB:(h + 1) * QB, :] = ql.astype(BF16)

    lane = lax.broadcasted_iota(I32, (QB, LANES), 1)
    for p in range(H_IDX // 2):
        qp = qi_ref[:, p * LANES:(p + 1) * LANES]
        zero = jnp.zeros_like(qp)
        qm_s[p, 0:QB, :] = jnp.where(lane < D_IDX, qp, zero)
        qm_s[p, QB:2 * QB, :] = jnp.where(lane >= D_IDX, qp, zero)

    wit = wi_ref[...].astype(F32).T

    def score_body(kb, carry):
        kk = kk_s[kb]
        acc = jnp.zeros((KB, QB), F32)
        for p in range(H_IDX // 2):
            d2 = _dot(kk, qm_s[p], _NT)
            acc = acc + jnp.maximum(d2[:, 0:QB], 0.0) * wit[2 * p:2 * p + 1, :]
            acc = acc + jnp.maximum(d2[:, QB:2 * QB], 0.0) * wit[2 * p + 1:2 * p + 2, :]
        bits = lax.bitcast_convert_type(acc, I32)
        key = bits ^ ((bits >> 31) & 0x7FFFFFFF)
        s_idx = kb * KB + lax.broadcasted_iota(I32, (KB, QB), 0)
        t_idx = i * QB + lax.broadcasted_iota(I32, (KB, QB), 1)
        adm = (s_idx >= FRONT_PAD) & (jnp.maximum(s_idx >> 6, 1) <= jnp.maximum(t_idx >> 6, 1))
        key = jnp.where(adm, key, INT_MIN)
        st_s[kb] = key
        hi_s[kb] = (key >> 16).astype(I16)
        return carry

    lax.fori_loop(0, n_kb, score_body, 0)

    def search16(ref, need):
        def count_ge(cand):
            cand16 = jnp.broadcast_to(cand, (PACK16, QB)).astype(I16)[None]
            one, zero = jnp.ones((), I16), jnp.zeros((), I16)

            def body(j, acc):
                parts = []
                for u in range(2):
                    blk = ref[2 * j + u].reshape(KB // PACK16, PACK16, QB)
                    ind = jnp.where(blk >= cand16, one, zero)
                    parts += [ind[g] for g in range(KB // PACK16)]
                while len(parts) > 1:
                    parts = [parts[g] + parts[g + 1] for g in range(0, len(parts), 2)]
                return acc + parts[0]

            acc = lax.fori_loop(0, (n_kb + 1) // 2, body, jnp.zeros((PACK16, QB), I16))
            return jnp.sum(acc.astype(I32), axis=0, keepdims=True)

        def bit_body(it, carry):
            t, above = carry
            cand = t + lax.shift_left(jnp.int32(1), jnp.int32(15) - it)
            cnt = count_ge(cand)
            ok = cnt >= need
            return jnp.where(ok, cand, t), jnp.where(ok, above, cnt)

        init = (jnp.full((1, QB), I16_MIN, I32), jnp.zeros((1, QB), I32))
        return lax.fori_loop(0, 16, bit_body, init)

    hi_s[n_kb] = jnp.full((KB, QB), I16_MIN, I16)
    lo_s[n_kb] = jnp.full((KB, QB), I16_MIN, I16)
    tau_hi, n_above = search16(hi_s, jnp.full((1, QB), topk, I32))

    def low_half_body(kb, carry):
        key = st_s[kb]
        low = (key & 0xFFFF) + I16_MIN
        lo_s[kb] = jnp.where((key >> 16) == tau_hi, low, I16_MIN).astype(I16)
        return carry

    lax.fori_loop(0, n_kb, low_half_body, 0)
    tau_lo, _ = search16(lo_s, topk - n_above)
    tau = lax.shift_left(tau_hi, jnp.int32(16)) | (tau_lo - I16_MIN)
    tau = jnp.maximum(tau, INT_MIN + 1)

    m_s[...] = jnp.full(m_s.shape, NEG_INF, F32)
    l_s[...] = jnp.zeros(l_s.shape, F32)
    acc_s[...] = jnp.zeros(acc_s.shape, F32)

    def logits_stage(kb, lg_ref):
        lg_ref[...] = _dot(cn_s[kb], qlat_s[...], _NT)

    def softmax_stage(kb, lg_ref):
        neg = jnp.where(st_s[kb] >= tau, 0.0, NEG_INF)
        v0 = pl.multiple_of(jnp.maximum(KB * kb - QB * i + BIAS_ORIGIN, 0), LANES)
        ps, alphas = [], []
        for h in range(H_B):
            lh = lg_ref[:, h * QB:(h + 1) * QB] + bias_ref[h, pl.ds(v0, KB), :] + neg
            m_old = m_s[h:h + 1, :]
            m_new = jnp.maximum(m_old, jnp.max(lh, axis=0, keepdims=True))
            p = jnp.exp2(lh - m_new)
            alpha = jnp.exp2(m_old - m_new)
            l_s[h:h + 1, :] = alpha * l_s[h:h + 1, :] + jnp.sum(p, axis=0, keepdims=True)
            m_s[h:h + 1, :] = m_new
            ps.append(p.astype(BF16))
            alphas.append(alpha)
        pv = _dot(cnt_s[kb], jnp.concatenate(ps, axis=1))
        acc_s[...] = acc_s[...] * jnp.concatenate(alphas, axis=1) + pv

    logits_stage(0, lg_a)

    def att_pair(j, carry):
        kb = 2 * j
        logits_stage(kb + 1, lg_b)
        softmax_stage(kb, lg_a)
        logits_stage(jnp.minimum(kb + 2, n_kb - 1), lg_a)
        softmax_stage(kb + 1, lg_b)
        return carry

    lax.fori_loop(0, n_kb // 2, att_pair, 0)

    @pl.when(n_kb % 2 == 1)
    def _():
        softmax_stage(n_kb - 1, lg_a)

    for h in range(H_B):
        cols = slice(h * QB, (h + 1) * QB)
        ot = (acc_s[:, cols] * (1.0 / l_s[h:h + 1, :])).astype(BF16)
        bt = _dot(wuvt_ref[h], ot)
        gate = gb_ref[:, cols].astype(F32)
        o_ref[:, cols] = (bt.T * _silu(gate)).astype(o_ref.dtype)


def dsa_attention(proj, kv_gain, wuk_t, wuv_t, bias_tab, *, topk):
    bsz, tp, _ = proj.shape
    n_qb = tp // QB
    n_kb_all = -(-tp // KB)
    return pl.pallas_call(
        functools.partial(_attn_kernel, tp=tp, topk=topk),
        grid=(bsz, n_qb),
        in_specs=[
            pl.BlockSpec((None, tp, D_C), lambda b, i: (b, 0, E_COL_C // D_C)),
            pl.BlockSpec((None, tp, LANES), lambda b, i: (b, 0, E_COL_KK // LANES)),
            pl.BlockSpec((None, QB, LANES), lambda b, i: (b, i, E_COL_WI // LANES)),
            pl.BlockSpec((None, QB, H_B * DH_B), lambda b, i: (b, i, E_COL_Q // (H_B * DH_B))),
            pl.BlockSpec((None, QB, H_IDX * D_IDX), lambda b, i: (b, i, E_COL_QI // (H_IDX * D_IDX))),
            pl.BlockSpec((None, QB, H_B * DH_B), lambda b, i: (b, i, E_COL_GB // (H_B * DH_B))),
            pl.BlockSpec((1, D_C), lambda b, i: (0, 0)),
            pl.BlockSpec((H_B, DH_B, D_C), lambda b, i: (0, 0, 0)),
            pl.BlockSpec((H_B, DH_B, D_C), lambda b, i: (0, 0, 0)),
            pl.BlockSpec((H_B, BIAS_ROWS, LANES), lambda b, i: (0, 0, 0)),
        ],
        out_specs=pl.BlockSpec((None, QB, H_B * DH_B), lambda b, i: (b, i, 0)),
        out_shape=jax.ShapeDtypeStruct((bsz, tp, H_B * DH_B), BF16),
        scratch_shapes=[
            pltpu.VMEM((n_kb_all, KB, D_C), BF16),
            pltpu.VMEM((n_kb_all, D_C, KB), BF16),
            pltpu.VMEM((n_kb_all, KB, LANES), BF16),
            pltpu.VMEM((n_kb_all, KB, QB), I32),
            pltpu.VMEM((n_kb_all + 1, KB, QB), I16),
            pltpu.VMEM((n_kb_all + 1, KB, QB), I16),
            pltpu.VMEM((H_IDX // 2, 2 * QB, LANES), BF16),
            pltpu.VMEM((H_B * QB, D_C), BF16),
            pltpu.VMEM((KB, H_B * QB), F32),
            pltpu.VMEM((KB, H_B * QB), F32),
            pltpu.VMEM((D_C, H_B * QB), F32),
            pltpu.VMEM((H_B, QB), F32),
            pltpu.VMEM((H_B, QB), F32),
        ],
        compiler_params=_cparams(("parallel", "arbitrary"), 40),
        name="dsa_attention",
    )(proj, proj, proj, proj, proj, proj, kv_gain.reshape(1, D_C).astype(F32), wuk_t, wuv_t, bias_tab)


def _hgrn_constants():
    n = HC
    t = np.arange(n)[:, None]
    j = np.arange(n)[None, :]
    mats = [(j <= t), (j > t)]
    for lvl in range(N_LEVELS):
        upper = ((t >> lvl) & 1) == 1
        m_up = (t >> lvl) << lvl
        m_lo = ((t >> lvl) + 1) << lvl
        mats.append(np.where(upper, (j >= m_up) & (j <= t), (j > t) & (j < m_lo)))
    m = np.concatenate(mats, axis=0).astype(np.float32)
    mcat = np.concatenate([m, m], axis=1)
    x = t ^ j
    lev = np.where(j < t, np.floor(np.log2(np.maximum(x, 1))).astype(np.int32), np.where(j == t, -1, -2))
    return mcat, lev.astype(np.int32)


HGRN_HEADS_PER_STEP = 4


def _role_select(qf, k, lvl, up_small):
    half = 1 << lvl
    if half >= SUBLANES:
        pieces = []
        for blk in range(HC // (2 * half)):
            lo = 2 * half * blk
            pieces += [k[lo:lo + half], qf[lo + half:lo + 2 * half]]
        return jnp.concatenate(pieces, axis=0)
    g, w = HC // SUBLANES, qf.shape[-1]
    sel = jnp.where(up_small[lvl], qf.reshape(g, SUBLANES, w), k.reshape(g, SUBLANES, w))
    return sel.reshape(HC, w)


_SLOT_QE, _SLOT_KT, _SLOT_QB, _SLOT_KB, _SLOT_Y0 = 0, 1, 2, 3, 4
_N_SLOTS = _SLOT_Y0 + N_LEVELS


def _hgrn_kernel(q_ref, fz_ref, i_ref, g_ref, lbp_ref, ng_ref, mcat_ref, lev_ref, o_ref,
                 s_ref, stage_a, stage_b, dec_a, dec_b, *, n_steps, nh):
    wide = nh * DK_C
    lev = lev_ref[...]
    row8 = lax.broadcasted_iota(I32, (1, SUBLANES, wide), 1)
    up_small = [((row8 >> lvl) & 1) == 1 for lvl in range(3)]
    s_ref[...] = jnp.zeros(s_ref.shape, F32)

    def gate_stage(c):
        rows = pl.ds(pl.multiple_of(c * HC, HC), HC)
        lb = lbp_ref[0:1, :]
        one_m_lb = lbp_ref[1:2, :]
        qf = _silu(q_ref[rows, :].astype(F32))
        z = fz_ref[rows, :].astype(F32)
        e = jnp.exp(-jnp.abs(z))
        r = 1.0 / (1.0 + e)
        er = e * r
        pos = z >= 0
        k = one_m_lb * jnp.where(pos, er, r)
        f = lb + one_m_lb * jnp.where(pos, r, er)
        lf2 = jnp.log(f) * LOG2_E
        hi = lf2.astype(BF16)
        lo = (lf2 - hi.astype(F32)).astype(BF16)
        return qf, k, _dot(mcat_ref[...], jnp.concatenate([hi, lo], axis=0))

    def operand_stage(gates, stage, dec_ref):
        qf, k, gsum = gates
        ex = jnp.exp2(gsum)
        e_b = ex[0:HC]
        stage[_SLOT_QE] = (qf * e_b).astype(BF16)
        stage[_SLOT_KT] = (k * ex[HC:2 * HC]).astype(BF16)
        stage[_SLOT_QB] = qf.astype(BF16)
        stage[_SLOT_KB] = k.astype(BF16)
        for lvl in range(N_LEVELS):
            y = _role_select(qf, k, lvl, up_small) * ex[(2 + lvl) * HC:(3 + lvl) * HC]
            stage[_SLOT_Y0 + lvl] = y.astype(BF16)
        dec_ref[...] = e_b[HC - 1:HC, :]

    def level_stage(c, stage):
        lns = [slice(hh * DK_C, (hh + 1) * DK_C) for hh in range(nh)]
        sts = [s_ref[hh] for hh in range(nh)]
        outs = [_dot(stage[_SLOT_QE, :, ln], st.astype(BF16), _NT) for ln, st in zip(lns, sts)]
        accs = [jnp.where(lev == -1, _dot(stage[_SLOT_QB, :, ln], stage[_SLOT_KB, :, ln], _NT), 0.0)
                for ln in lns]
        for lvl in range(N_LEVELS):
            for hh, ln in enumerate(lns):
                y = stage[_SLOT_Y0 + lvl, :, ln]
                accs[hh] = jnp.where(lev == lvl, _dot(y, y, _NT), accs[hh])
        return sts, outs, accs

    def output_stage(c, partial, stage, dec_ref):
        sts, outs, accs = partial
        rows = pl.ds(pl.multiple_of(c * HC, HC), HC)
        normed = []
        for hh in range(nh):
            ln = slice(hh * DK_C, (hh + 1) * DK_C)
            vh = i_ref[rows, ln]
            o = outs[hh] + _dot(accs[hh].astype(BF16), vh)
            vt = vh.astype(F32).T.astype(BF16)
            s_ref[hh] = sts[hh] * dec_ref[:, ln] + _dot(vt, stage[_SLOT_KT, :, ln])
            normed.append(o * lax.rsqrt(jnp.mean(o * o, axis=-1, keepdims=True) + RMS_EPS))
        on = jnp.concatenate(normed, axis=1) * ng_ref[...]
        gx = g_ref[rows, :].astype(F32)
        o_ref[rows, :] = (on * _silu(gx)).astype(o_ref.dtype)

    last = n_steps - 1
    operand_stage(gate_stage(0), stage_a, dec_a)

    def pair(j, carry):
        c = 2 * j
        partial = level_stage(c, stage_a)
        gates = gate_stage(c + 1)
        output_stage(c, partial, stage_a, dec_a)
        operand_stage(gates, stage_b, dec_b)
        partial = level_stage(c + 1, stage_b)
        gates = gate_stage(jnp.minimum(c + 2, last))
        output_stage(c + 1, partial, stage_b, dec_b)
        operand_stage(gates, stage_a, dec_a)
        return carry

    lax.fori_loop(0, n_steps // 2, pair, 0)
    if n_steps % 2:
        output_stage(last, level_stage(last, stage_a), stage_a, dec_a)


def hgrn2(proj, lb_params, norm_g):
    bsz, tp, _ = proj.shape
    mcat, lev = _hgrn_constants()
    nh = HGRN_HEADS_PER_STEP
    wide = nh * DK_C
    n_groups = H_C // nh
    col = lambda off: pl.BlockSpec((None, tp, wide), lambda b, h, off=off: (b, 0, off + h))
    return pl.pallas_call(
        functools.partial(_hgrn_kernel, n_steps=tp // HC, nh=nh),
        grid=(bsz, n_groups),
        in_specs=[
            col(0), col(n_groups), col(2 * n_groups), col(3 * n_groups),
            pl.BlockSpec((2, wide), lambda b, h: (0, h)),
            pl.BlockSpec((1, wide), lambda b, h: (0, h)),
            pl.BlockSpec(mcat.shape, lambda b, h: (0, 0)),
            pl.BlockSpec(lev.shape, lambda b, h: (0, 0)),
        ],
        out_specs=pl.BlockSpec((None, tp, wide), lambda b, h: (b, 0, h)),
        out_shape=jax.ShapeDtypeStruct((bsz, tp, H_C * DK_C), BF16),
        scratch_shapes=[
            pltpu.VMEM((nh, DK_C, DK_C), F32),
            pltpu.VMEM((_N_SLOTS, HC, wide), BF16),
            pltpu.VMEM((_N_SLOTS, HC, wide), BF16),
            pltpu.VMEM((1, wide), F32),
            pltpu.VMEM((1, wide), F32),
        ],
        compiler_params=_cparams(("parallel", "parallel"), 40),
        name="hgrn2",
    )(proj, proj, proj, proj, lb_params, norm_g.reshape(1, -1).astype(F32),
      jnp.asarray(mcat, BF16), jnp.asarray(lev, I32))


def _t5_bucket(rel):
    nb = N_BUCKETS // 2
    ret = jnp.where(rel > 0, nb, 0)
    n = jnp.abs(rel)
    max_exact = nb // 2
    nf = jnp.maximum(n, 1).astype(F32)
    large = max_exact + (jnp.log(nf / max_exact) / math.log(MAX_DISTANCE / max_exact)
                         * (nb - max_exact)).astype(I32)
    large = jnp.minimum(large, nb - 1)
    return ret + jnp.where(n < max_exact, n, large)


def _bias_table(rel_bias_table):
    v = jnp.arange(BIAS_ROWS, dtype=I32)[:, None]
    q = jnp.arange(LANES, dtype=I32)[None, :]
    onehot = jax.nn.one_hot(_t5_bucket(v - BIAS_ORIGIN - q), N_BUCKETS, dtype=F32)
    tab = jnp.einsum("vqb,bh->hvq", onehot, rel_bias_table.astype(F32), precision=lax.Precision.HIGHEST)
    return tab * LOG2_E


def _even_in_weight(w):
    glu_v, glu_g, gate_a, q, c, gate_b, qi, ki, wi = jnp.split(
        w, np.cumsum([D_A, D_A, D_A, H_B * DH_B, D_C, H_B * DH_B, H_IDX * D_IDX, D_IDX])[:8].tolist(), axis=1)
    pad = jnp.zeros((w.shape[0], E_NPAD - E_COL_WI - H_IDX), w.dtype)
    cols = [glu_v, glu_g, gate_a, q, gate_b, qi * (D_IDX ** -0.5), c, ki, ki, wi * (H_IDX ** -0.5), pad]
    return jnp.concatenate(cols, axis=1).astype(BF16)


def kernel(x, meta_tokens, norm_gain, final_norm_gain, rel_bias_table, w_in_even, conv_w, conv_b,
           conv_ln_gain, conv_ln_bias, kv_norm_gain, w_uk, w_uv, w_out_even, w_in_odd, lb_logits,
           rec_norm_gain, w_out_odd):
    bsz, seq, d = x.shape
    assert d == D_MODEL and seq % LANES == 0
    depth = norm_gain.shape[0]
    tp = FRONT_PAD + N_META + seq
    topk = min(TOPK_MAX, seq // 4)
    tt_mm = tp // 2 if (tp // 2) % 16 == 0 else tp
    tt_out = tp // 4 if (tp // 4) % 16 == 0 else tp
    tt_conv = tp // 8 if (tp // 8) % CONV_RB == 0 else tp // 2 if (tp // 2) % CONV_RB == 0 else tp
    h = jnp.concatenate([
        jnp.zeros((bsz, FRONT_PAD, d), F32),
        jnp.broadcast_to(meta_tokens.astype(F32), (bsz, N_META, d)),
        x.astype(F32)], axis=1)

    lb_soft = jax.nn.softmax(lb_logits.astype(F32), axis=0)
    lower_bounds = jnp.cumsum(lb_soft, axis=0) - lb_soft[0]
    bias_tab = _bias_table(rel_bias_table)

    for layer in range(depth):
        last = layer == depth - 1
        fg = final_norm_gain if last else None
        if layer % 2 == 0:
            e = layer // 2
            proj = norm_matmul(h, norm_gain[layer], _even_in_weight(w_in_even[e]), tt=tt_mm, tn=E_NPAD // 4)
            a_out = conv_module(proj, conv_w[e], conv_b[e], conv_ln_gain[e], conv_ln_bias[e], tt=tt_conv)
            wuk_t = jnp.transpose(w_uk[e], (0, 2, 1)).astype(BF16)
            wuv_t = jnp.transpose(w_uv[e], (0, 2, 1)).astype(BF16)
            b_out = dsa_attention(proj, kv_norm_gain[e], wuk_t, wuv_t, bias_tab, topk=topk)
            h = out_proj([a_out, b_out], w_out_even[e].astype(BF16), h, fg, tt=tt_out)
        else:
            o = layer // 2
            lb = lower_bounds[layer]
            lb_params = jnp.stack([lb, 1.0 - lb], axis=0)
            proj = norm_matmul(h, norm_gain[layer], w_in_odd[o].astype(BF16), tt=tt_mm, tn=2048)
            rec = hgrn2(proj, lb_params, rec_norm_gain[o])
            h = out_proj([rec], w_out_odd[o].astype(BF16), h, fg, tt=tt_out)
    return h[:, FRONT_PAD + N_META:]
```

```python
import functools
import math

import numpy as np
import jax
import jax.numpy as jnp
from jax import lax
from jax.experimental import pallas as pl
from jax.experimental.pallas import tpu as pltpu

F32 = jnp.float32
BF16 = jnp.bfloat16
I32 = jnp.int32
I16 = jnp.int16

D_MODEL = 2048
CHUNK = 64
N_META = 16
D_A = 1024
CONV_WIDTH = 31
H_B = 8
DH_B = 128
D_C = 256
H_IDX = 16
D_IDX = 64
TOPK_MAX = 256
N_BUCKETS = 32
MAX_DISTANCE = 128
H_C = 16
DK_C = 128
RMS_EPS = 1e-6
LN_EPS = 1e-5
NEG_INF = -1e30
INT_MIN = -(2 ** 31)
LOG2_E = 1.4426950408889634

LANES = 128
SUBLANES = 8
PACK16 = 16
I16_MIN = -(2 ** 15)
MXU_N = 256

FRONT_PAD = LANES - N_META
QB = LANES
KB = MXU_N
HC = 128
N_LEVELS = 7

E_NPAD = 6656
E_COL_Q, E_COL_GB, E_COL_QI = 3072, 4096, 5120
E_COL_C, E_COL_KK, E_COL_WI = 6144, 6400, 6528

MIB = 1024 * 1024


def _cparams(sem, vmem_mib, flags=None):
    return pltpu.CompilerParams(dimension_semantics=sem, vmem_limit_bytes=vmem_mib * MIB, flags=flags)


def _sigmoid(x):
    return 1.0 / (1.0 + jnp.exp(-x))


def _silu(x):
    h = 0.5 * x
    return h + h * jnp.tanh(h)


_NN = (((1,), (0,)), ((), ()))
_NT = (((1,), (1,)), ((), ()))


def _dot(a, b, dims=_NN):
    return lax.dot_general(a, b, dims, preferred_element_type=F32)


def _norm_matmul_kernel(x_ref, g_ref, w_ref, o_ref, hn_ref):
    @pl.when(pl.program_id(2) == 0)
    def _():
        x = x_ref[...]
        ms = jnp.mean(x * x, axis=-1, keepdims=True)
        hn_ref[...] = (x * lax.rsqrt(ms + RMS_EPS) * g_ref[...]).astype(BF16)

    o_ref[...] = _dot(hn_ref[...], w_ref[...]).astype(o_ref.dtype)


def norm_matmul(h, gain, w, *, tt, tn):
    bsz, tp, d = h.shape
    n = w.shape[1]
    return pl.pallas_call(
        _norm_matmul_kernel,
        grid=(bsz, tp // tt, n // tn),
        in_specs=[
            pl.BlockSpec((None, tt, d), lambda b, i, j: (b, i, 0)),
            pl.BlockSpec((1, d), lambda b, i, j: (0, 0)),
            pl.BlockSpec((d, tn), lambda b, i, j: (0, j)),
        ],
        out_specs=pl.BlockSpec((None, tt, tn), lambda b, i, j: (b, i, j)),
        out_shape=jax.ShapeDtypeStruct((bsz, tp, n), BF16),
        scratch_shapes=[pltpu.VMEM((tt, d), BF16)],
        compiler_params=_cparams(("parallel", "parallel", "arbitrary"), 56),
        name="norm_matmul",
    )(h, gain.reshape(1, d).astype(F32), w)


def _out_proj_kernel(*refs, n_in, tt, final):
    xs = refs[:n_in]
    w_ref, h_ref = refs[n_in], refs[n_in + 1]
    g_ref = refs[n_in + 2] if final else None
    o_ref = refs[-1]
    y = None
    off = 0
    for x_ref in xs:
        kdim = x_ref.shape[-1]
        part = _dot(x_ref[...], w_ref[off:off + kdim, :])
        y = part if y is None else y + part
        off += kdim
    row = pl.program_id(1) * tt + lax.broadcasted_iota(I32, (tt, 1), 0)
    hn = jnp.where(row >= FRONT_PAD, h_ref[...] + y, 0.0)
    if final:
        ms = jnp.mean(hn * hn, axis=-1, keepdims=True)
        hn = hn * lax.rsqrt(ms + RMS_EPS) * g_ref[...]
    o_ref[...] = hn


def out_proj(xs, w, h, final_gain=None, *, tt):
    bsz, tp, d = h.shape
    final = final_gain is not None
    in_specs = [pl.BlockSpec((None, tt, x.shape[-1]), lambda b, i: (b, i, 0)) for x in xs]
    in_specs += [
        pl.BlockSpec(w.shape, lambda b, i: (0, 0)),
        pl.BlockSpec((None, tt, d), lambda b, i: (b, i, 0)),
    ]
    args = list(xs) + [w, h]
    if final:
        in_specs.append(pl.BlockSpec((1, d), lambda b, i: (0, 0)))
        args.append(final_gain.reshape(1, d).astype(F32))
    return pl.pallas_call(
        functools.partial(_out_proj_kernel, n_in=len(xs), tt=tt, final=final),
        grid=(bsz, tp // tt),
        in_specs=in_specs,
        out_specs=pl.BlockSpec((None, tt, d), lambda b, i: (b, i, 0)),
        out_shape=jax.ShapeDtypeStruct((bsz, tp, d), F32),
        compiler_params=_cparams(("parallel", "parallel"), 56),
        name="out_proj_final" if final else "out_proj",
    )(*args)


CONV_HALO = 32
CONV_RB = 16
CONV_CHAINS = 4


def _conv_kernel(v_ref, g_ref, ga_ref, w_ref, cb_ref, lg_ref, lb_ref, o_ref, us_ref, cv_ref, *, tt):
    j = pl.program_id(1)
    u_new = v_ref[...].astype(F32) * _sigmoid(g_ref[...].astype(F32))
    span = tt + CONV_HALO - SUBLANES
    n_rb = tt // CONV_RB
    for cb in range(D_A // LANES):
        lanes = slice(cb * LANES, (cb + 1) * LANES)

        @pl.when(j == 0)
        def _(cb=cb):
            us_ref[0, cb, 0:CONV_HALO, :] = jnp.zeros((CONV_HALO, LANES), F32)

        @pl.when(j > 0)
        def _(cb=cb):
            us_ref[0, cb, 0:CONV_HALO, :] = us_ref[0, cb, tt:tt + CONV_HALO, :]

        us_ref[0, cb, CONV_HALO:CONV_HALO + tt, :] = u_new[:, lanes]
        for r in range(1, SUBLANES):
            us_ref[r, cb, 0:span, :] = us_ref[0, cb, r:r + span, :]

        wb = [jnp.broadcast_to(w_ref[k:k + 1, lanes], (CONV_RB, LANES)) for k in range(CONV_WIDTH)]
        bias = jnp.broadcast_to(cb_ref[0:1, lanes], (CONV_RB, LANES))

        def rb_body(rb, carry, cb=cb, lanes=lanes, wb=wb, bias=bias):
            base = pl.multiple_of(rb * CONV_RB, CONV_RB)
            parts = [bias] + [None] * (CONV_CHAINS - 1)
            for k in range(CONV_WIDTH):
                a, r = divmod(k + CONV_HALO - (CONV_WIDTH - 1), SUBLANES)
                term = wb[k] * us_ref[r, cb, pl.ds(base + SUBLANES * a, CONV_RB), :]
                c = k % CONV_CHAINS
                parts[c] = term if parts[c] is None else parts[c] + term
            cv_ref[pl.ds(base, CONV_RB), lanes] = (parts[0] + parts[1]) + (parts[2] + parts[3])
            return carry

        lax.fori_loop(0, n_rb, rb_body, 0)

    x = cv_ref[...]
    mu = jnp.mean(x, axis=-1, keepdims=True)
    xc = x - mu
    var = jnp.mean(xc * xc, axis=-1, keepdims=True)
    y = xc * lax.rsqrt(var + LN_EPS) * lg_ref[...] + lb_ref[...]
    ga = ga_ref[...].astype(F32)
    o_ref[...] = (_silu(y) * _silu(ga)).astype(o_ref.dtype)


def conv_module(proj, conv_w, conv_b, ln_g, ln_b, *, tt):
    bsz, tp, _ = proj.shape
    row = lambda a: a.reshape(1, D_A).astype(F32)
    return pl.pallas_call(
        functools.partial(_conv_kernel, tt=tt),
        grid=(bsz, tp // tt),
        in_specs=[
            pl.BlockSpec((None, tt, D_A), lambda b, j: (b, j, 0)),
            pl.BlockSpec((None, tt, D_A), lambda b, j: (b, j, 1)),
            pl.BlockSpec((None, tt, D_A), lambda b, j: (b, j, 2)),
            pl.BlockSpec((CONV_WIDTH, D_A), lambda b, j: (0, 0)),
            pl.BlockSpec((1, D_A), lambda b, j: (0, 0)),
            pl.BlockSpec((1, D_A), lambda b, j: (0, 0)),
            pl.BlockSpec((1, D_A), lambda b, j: (0, 0)),
        ],
        out_specs=pl.BlockSpec((None, tt, D_A), lambda b, j: (b, j, 0)),
        out_shape=jax.ShapeDtypeStruct((bsz, tp, D_A), BF16),
        scratch_shapes=[
            pltpu.VMEM((SUBLANES, D_A // LANES, tt + CONV_HALO, LANES), F32),
            pltpu.VMEM((tt, D_A), F32),
        ],
        compiler_params=_cparams(("parallel", "arbitrary"), 40),
        name="conv_module",
    )(proj, proj, proj, conv_w.astype(F32), row(conv_b), row(ln_g), row(ln_b))


BIAS_ROWS = 640
BIAS_ORIGIN = 384


def _attn_kernel(c_ref, kk_ref, wi_ref, q_ref, qi_ref, gb_ref, kvg_ref, wuk_ref, wuvt_ref, bias_ref,
                 o_ref, cn_s, cnt_s, kk_s, st_s, hi_s, lo_s, qm_s, qlat_s, lg_a, lg_b, acc_s, m_s, l_s,
                 tau_s, *, tp, topk):
    i = pl.program_id(1)
    n_kb = (i + 2) // 2
    n_kb_all = cn_s.shape[0]

    @pl.when(i == 0)
    def _prepare_keys():
        gain = kvg_ref[...]
        for kb in range(n_kb_all):
            lo = kb * KB
            nrow = min(KB, tp - lo)
            c = c_ref[lo:lo + nrow, :].astype(F32)
            cn = c * lax.rsqrt(jnp.mean(c * c, axis=-1, keepdims=True) + RMS_EPS) * gain
            cn_s[kb, 0:nrow, :] = cn.astype(BF16)
            kk_s[kb, 0:nrow, :] = kk_ref[lo:lo + nrow, :]
            for rb in range(nrow // LANES):
                for cb in range(D_C // LANES):
                    tile = cn[rb * LANES:(rb + 1) * LANES, cb * LANES:(cb + 1) * LANES]
                    cnt_s[kb, cb * LANES:(cb + 1) * LANES, rb * LANES:(rb + 1) * LANES] = tile.T.astype(BF16)
            if nrow < KB:
                cn_s[kb, nrow:KB, :] = jnp.zeros((KB - nrow, D_C), BF16)
                kk_s[kb, nrow:KB, :] = jnp.zeros((KB - nrow, LANES), BF16)
                cnt_s[kb, :, nrow:KB] = jnp.zeros((D_C, KB - nrow), BF16)

    for h in range(H_B):
        qh = q_ref[:, h * DH_B:(h + 1) * DH_B]
        ql = _dot(qh, wuk_ref[h]) * (DH_B ** -0.5 * LOG2_E)
        qlat_s[h * QB:(h + 1) * QB, :] = ql.astype(BF16)

    lane = lax.broadcasted_iota(I32, (QB, LANES), 1)
    for p in range(H_IDX // 2):
        qp = qi_ref[:, p * LANES:(p + 1) * LANES]
        zero = jnp.zeros_like(qp)
        qm_s[p, 0:QB, :] = jnp.where(lane < D_IDX, qp, zero)
        qm_s[p, QB:2 * QB, :] = jnp.where(lane >= D_IDX, qp, zero)

    wit = wi_ref[...].astype(F32).T

    def score_body(kb, carry):
        kk = kk_s[kb]
        acc = jnp.zeros((KB, QB), F32)
        for p in range(H_IDX // 2):
            d2 = _dot(kk, qm_s[p], _NT)
            acc = acc + jnp.maximum(d2[:, 0:QB], 0.0) * wit[2 * p:2 * p + 1, :]
            acc = acc + jnp.maximum(d2[:, QB:2 * QB], 0.0) * wit[2 * p + 1:2 * p + 2, :]
        bits = lax.bitcast_convert_type(acc, I32)
        key = bits ^ ((bits >> 31) & 0x7FFFFFFF)
        s_idx = kb * KB + lax.broadcasted_iota(I32, (KB, QB), 0)
        t_idx = i * QB + lax.broadcasted_iota(I32, (KB, QB), 1)
        adm = (s_idx >= FRONT_PAD) & (jnp.maximum(s_idx >> 6, 1) <= jnp.maximum(t_idx >> 6, 1))
        key = jnp.where(adm, key, INT_MIN)
        st_s[kb] = key
        hi_s[kb] = (key >> 16).astype(I16)
        return carry

    lax.fori_loop(0, n_kb, score_body, 0)

    def search16(ref, need, n_blocks):
        one, zero = jnp.ones((), I16), jnp.zeros((), I16)

        def bit_body(it, carry):
            t, above = carry
            cand = t + lax.shift_left(jnp.int32(1), jnp.int32(15) - it)
            cand16 = jnp.broadcast_to(cand, (PACK16, QB)).astype(I16)[None]
            parts = []
            for kb in range(n_blocks):
                blk = ref[kb].reshape(KB // PACK16, PACK16, QB)
                ind = jnp.where(blk >= cand16, one, zero)
                parts += [ind[g] for g in range(KB // PACK16)]
            while len(parts) > 1:
                nxt = [parts[g] + parts[g + 1] for g in range(0, len(parts) - 1, 2)]
                parts = nxt + parts[len(parts) - len(parts) % 2:]
            cnt = jnp.sum(parts[0].astype(I32), axis=0, keepdims=True)
            ok = cnt >= need
            return jnp.where(ok, cand, t), jnp.where(ok, above, cnt)

        init = (jnp.full((1, QB), I16_MIN, I32), jnp.zeros((1, QB), I32))
        return lax.fori_loop(0, 16, bit_body, init)

    st_s[n_kb] = jnp.full((KB, QB), INT_MIN, I32)
    hi_s[n_kb] = jnp.full((KB, QB), I16_MIN, I16)
    n_pairs = (n_kb + 1) // 2
    few_keys = N_META + QB * i <= topk

    @pl.when(few_keys)
    def _():
        tau_s[...] = jnp.full((1, QB), INT_MIN, I32)

    for pairs in range(1, (n_kb_all + 1) // 2 + 1):
        @pl.when(jnp.logical_and(n_pairs == pairs, jnp.logical_not(few_keys)))
        def _(n_blocks=2 * pairs):
            tau_hi, n_above = search16(hi_s, jnp.full((1, QB), topk, I32), n_blocks)
            for kb in range(n_blocks):
                key = st_s[kb]
                low = (key & 0xFFFF) + I16_MIN
                lo_s[kb] = jnp.where((key >> 16) == tau_hi, low, I16_MIN).astype(I16)
            tau_lo, _ = search16(lo_s, topk - n_above, n_blocks)
            tau_s[...] = lax.shift_left(tau_hi, jnp.int32(16)) | (tau_lo - I16_MIN)

    tau = jnp.maximum(tau_s[...], INT_MIN + 1)

    m_s[...] = jnp.full(m_s.shape, NEG_INF, F32)
    l_s[...] = jnp.zeros(l_s.shape, F32)
    acc_s[...] = jnp.zeros(acc_s.shape, F32)

    def logits_stage(kb, lg_ref):
        lg_ref[...] = _dot(cn_s[kb], qlat_s[...], _NT)

    def softmax_stage(kb, lg_ref):
        neg = jnp.where(st_s[kb] >= tau, 0.0, NEG_INF)
        v0 = pl.multiple_of(jnp.maximum(KB * kb - QB * i + BIAS_ORIGIN, 0), LANES)
        ps, alphas = [], []
        for h in range(H_B):
            lh = lg_ref[:, h * QB:(h + 1) * QB] + bias_ref[h, pl.ds(v0, KB), :] + neg
            m_old = m_s[h:h + 1, :]
            m_new = jnp.maximum(m_old, jnp.max(lh, axis=0, keepdims=True))
            p = jnp.exp2(lh - m_new)
            alpha = jnp.exp2(m_old - m_new)
            l_s[h:h + 1, :] = alpha * l_s[h:h + 1, :] + jnp.sum(p, axis=0, keepdims=True)
            m_s[h:h + 1, :] = m_new
            ps.append(p.astype(BF16))
            alphas.append(alpha)
        pv = _dot(cnt_s[kb], jnp.concatenate(ps, axis=1))
        acc_s[...] = acc_s[...] * jnp.concatenate(alphas, axis=1) + pv

    logits_stage(0, lg_a)

    def att_pair(j, carry):
        kb = 2 * j
        logits_stage(kb + 1, lg_b)
        softmax_stage(kb, lg_a)
        logits_stage(jnp.minimum(kb + 2, n_kb - 1), lg_a)
        softmax_stage(kb + 1, lg_b)
        return carry

    lax.fori_loop(0, n_kb // 2, att_pair, 0)

    @pl.when(n_kb % 2 == 1)
    def _():
        softmax_stage(n_kb - 1, lg_a)

    for h in range(H_B):
        cols = slice(h * QB, (h + 1) * QB)
        ot = (acc_s[:, cols] * (1.0 / l_s[h:h + 1, :])).astype(BF16)
        bt = _dot(wuvt_ref[h], ot)
        gate = gb_ref[:, cols].astype(F32)
        o_ref[:, cols] = (bt.T * _silu(gate)).astype(o_ref.dtype)


def dsa_attention(proj, kv_gain, wuk_t, wuv_t, bias_tab, *, topk):
    bsz, tp, _ = proj.shape
    n_qb = tp // QB
    n_kb_all = -(-tp // KB)
    return pl.pallas_call(
        functools.partial(_attn_kernel, tp=tp, topk=topk),
        grid=(bsz, n_qb),
        in_specs=[
            pl.BlockSpec((None, tp, D_C), lambda b, i: (b, 0, E_COL_C // D_C)),
            pl.BlockSpec((None, tp, LANES), lambda b, i: (b, 0, E_COL_KK // LANES)),
            pl.BlockSpec((None, QB, LANES), lambda b, i: (b, i, E_COL_WI // LANES)),
            pl.BlockSpec((None, QB, H_B * DH_B), lambda b, i: (b, i, E_COL_Q // (H_B * DH_B))),
            pl.BlockSpec((None, QB, H_IDX * D_IDX), lambda b, i: (b, i, E_COL_QI // (H_IDX * D_IDX))),
            pl.BlockSpec((None, QB, H_B * DH_B), lambda b, i: (b, i, E_COL_GB // (H_B * DH_B))),
            pl.BlockSpec((1, D_C), lambda b, i: (0, 0)),
            pl.BlockSpec((H_B, DH_B, D_C), lambda b, i: (0, 0, 0)),
            pl.BlockSpec((H_B, DH_B, D_C), lambda b, i: (0, 0, 0)),
            pl.BlockSpec((H_B, BIAS_ROWS, LANES), lambda b, i: (0, 0, 0)),
        ],
        out_specs=pl.BlockSpec((None, QB, H_B * DH_B), lambda b, i: (b, i, 0)),
        out_shape=jax.ShapeDtypeStruct((bsz, tp, H_B * DH_B), BF16),
        scratch_shapes=[
            pltpu.VMEM((n_kb_all, KB, D_C), BF16),
            pltpu.VMEM((n_kb_all, D_C, KB), BF16),
            pltpu.VMEM((n_kb_all, KB, LANES), BF16),
            pltpu.VMEM((n_kb_all + 1, KB, QB), I32),
            pltpu.VMEM((n_kb_all + 1, KB, QB), I16),
            pltpu.VMEM((n_kb_all + 1, KB, QB), I16),
            pltpu.VMEM((H_IDX // 2, 2 * QB, LANES), BF16),
            pltpu.VMEM((H_B * QB, D_C), BF16),
            pltpu.VMEM((KB, H_B * QB), F32),
            pltpu.VMEM((KB, H_B * QB), F32),
            pltpu.VMEM((D_C, H_B * QB), F32),
            pltpu.VMEM((H_B, QB), F32),
            pltpu.VMEM((H_B, QB), F32),
            pltpu.VMEM((1, QB), I32),
        ],
        compiler_params=_cparams(("parallel", "arbitrary"), 40),
        name="dsa_attention",
    )(proj, proj, proj, proj, proj, proj, kv_gain.reshape(1, D_C).astype(F32), wuk_t, wuv_t, bias_tab)


def _hgrn_constants():
    n = HC
    t = np.arange(n)[:, None]
    j = np.arange(n)[None, :]
    mats = [(j <= t), (j > t)]
    for lvl in range(N_LEVELS):
        upper = ((t >> lvl) & 1) == 1
        m_up = (t >> lvl) << lvl
        m_lo = ((t >> lvl) + 1) << lvl
        mats.append(np.where(upper, (j >= m_up) & (j <= t), (j > t) & (j < m_lo)))
    m = np.concatenate(mats, axis=0).astype(np.float32)
    mcat = np.concatenate([m, m], axis=1)
    x = t ^ j
    lev = np.where(j < t, np.floor(np.log2(np.maximum(x, 1))).astype(np.int32), np.where(j == t, -1, -2))
    return mcat, lev.astype(np.int32)


HGRN_HEADS_PER_STEP = 4


def _role_select(qf, k, lvl, up_small):
    half = 1 << lvl
    if half >= SUBLANES:
        pieces = []
        for blk in range(HC // (2 * half)):
            lo = 2 * half * blk
            pieces += [k[lo:lo + half], qf[lo + half:lo + 2 * half]]
        return jnp.concatenate(pieces, axis=0)
    g, w = HC // SUBLANES, qf.shape[-1]
    sel = jnp.where(up_small[lvl], qf.reshape(g, SUBLANES, w), k.reshape(g, SUBLANES, w))
    return sel.reshape(HC, w)


_SLOT_QE, _SLOT_KT, _SLOT_QB, _SLOT_KB, _SLOT_Y0 = 0, 1, 2, 3, 4
_N_SLOTS = _SLOT_Y0 + N_LEVELS


def _hgrn_kernel(q_ref, fz_ref, i_ref, g_ref, lbp_ref, ng_ref, mcat_ref, lev_ref, o_ref,
                 s_ref, stage_a, stage_b, dec_a, dec_b, *, n_steps, nh):
    wide = nh * DK_C
    lev = lev_ref[...]
    row8 = lax.broadcasted_iota(I32, (1, SUBLANES, wide), 1)
    up_small = [((row8 >> lvl) & 1) == 1 for lvl in range(3)]
    s_ref[...] = jnp.zeros(s_ref.shape, F32)

    def gate_stage(c):
        rows = pl.ds(pl.multiple_of(c * HC, HC), HC)
        lb = lbp_ref[0:1, :]
        one_m_lb = lbp_ref[1:2, :]
        qf = _silu(q_ref[rows, :].astype(F32))
        z = fz_ref[rows, :].astype(F32)
        e = jnp.exp(-jnp.abs(z))
        r = 1.0 / (1.0 + e)
        er = e * r
        pos = z >= 0
        k = one_m_lb * jnp.where(pos, er, r)
        f = lb + one_m_lb * jnp.where(pos, r, er)
        lf2 = jnp.log(f) * LOG2_E
        hi = lf2.astype(BF16)
        lo = (lf2 - hi.astype(F32)).astype(BF16)
        return qf, k, _dot(mcat_ref[...], jnp.concatenate([hi, lo], axis=0))

    def operand_stage(gates, stage, dec_ref):
        qf, k, gsum = gates
        ex = jnp.exp2(gsum)
        e_b = ex[0:HC]
        stage[_SLOT_QE] = (qf * e_b).astype(BF16)
        stage[_SLOT_KT] = (k * ex[HC:2 * HC]).astype(BF16)
        stage[_SLOT_QB] = qf.astype(BF16)
        stage[_SLOT_KB] = k.astype(BF16)
        for lvl in range(N_LEVELS):
            y = _role_select(qf, k, lvl, up_small) * ex[(2 + lvl) * HC:(3 + lvl) * HC]
            stage[_SLOT_Y0 + lvl] = y.astype(BF16)
        dec_ref[...] = e_b[HC - 1:HC, :]

    def level_stage(c, stage):
        lns = [slice(hh * DK_C, (hh + 1) * DK_C) for hh in range(nh)]
        sts = [s_ref[hh] for hh in range(nh)]
        outs = [_dot(stage[_SLOT_QE, :, ln], st.astype(BF16), _NT) for ln, st in zip(lns, sts)]
        accs = [jnp.where(lev == -1, _dot(stage[_SLOT_QB, :, ln], stage[_SLOT_KB, :, ln], _NT), 0.0)
                for ln in lns]
        for lvl in range(N_LEVELS):
            for hh, ln in enumerate(lns):
                y = stage[_SLOT_Y0 + lvl, :, ln]
                accs[hh] = jnp.where(lev == lvl, _dot(y, y, _NT), accs[hh])
        return sts, outs, accs

    def output_stage(c, partial, stage, dec_ref):
        sts, outs, accs = partial
        rows = pl.ds(pl.multiple_of(c * HC, HC), HC)
        lns = [slice(hh * DK_C, (hh + 1) * DK_C) for hh in range(nh)]
        vhs = [i_ref[rows, ln] for ln in lns]
        for hh, ln in enumerate(lns):
            vt = vhs[hh].astype(F32).T.astype(BF16)
            s_ref[hh] = sts[hh] * dec_ref[:, ln] + _dot(vt, stage[_SLOT_KT, :, ln])
        normed = []
        for hh in range(nh):
            o = outs[hh] + _dot(accs[hh].astype(BF16), vhs[hh])
            normed.append(o * lax.rsqrt(jnp.mean(o * o, axis=-1, keepdims=True) + RMS_EPS))
        on = jnp.concatenate(normed, axis=1) * ng_ref[...]
        gx = g_ref[rows, :].astype(F32)
        o_ref[rows, :] = (on * _silu(gx)).astype(o_ref.dtype)

    last = n_steps - 1
    operand_stage(gate_stage(0), stage_a, dec_a)

    def pair(j, carry):
        c = 2 * j
        partial = level_stage(c, stage_a)
        gates = gate_stage(c + 1)
        output_stage(c, partial, stage_a, dec_a)
        operand_stage(gates, stage_b, dec_b)
        partial = level_stage(c + 1, stage_b)
        gates = gate_stage(jnp.minimum(c + 2, last))
        output_stage(c + 1, partial, stage_b, dec_b)
        operand_stage(gates, stage_a, dec_a)
        return carry

    lax.fori_loop(0, n_steps // 2, pair, 0)
    if n_steps % 2:
        output_stage(last, level_stage(last, stage_a), stage_a, dec_a)


def hgrn2(proj, lb_params, norm_g):
    bsz, tp, _ = proj.shape
    mcat, lev = _hgrn_constants()
    nh = HGRN_HEADS_PER_STEP
    wide = nh * DK_C
    n_groups = H_C // nh
    col = lambda off: pl.BlockSpec((None, tp, wide), lambda b, h, off=off: (b, 0, off + h))
    return pl.pallas_call(
        functools.partial(_hgrn_kernel, n_steps=tp // HC, nh=nh),
        grid=(bsz, n_groups),
        in_specs=[
            col(0), col(n_groups), col(2 * n_groups), col(3 * n_groups),
            pl.BlockSpec((2, wide), lambda b, h: (0, h)),
            pl.BlockSpec((1, wide), lambda b, h: (0, h)),
            pl.BlockSpec(mcat.shape, lambda b, h: (0, 0)),
            pl.BlockSpec(lev.shape, lambda b, h: (0, 0)),
        ],
        out_specs=pl.BlockSpec((None, tp, wide), lambda b, h: (b, 0, h)),
        out_shape=jax.ShapeDtypeStruct((bsz, tp, H_C * DK_C), BF16),
        scratch_shapes=[
            pltpu.VMEM((nh, DK_C, DK_C), F32),
            pltpu.VMEM((_N_SLOTS, HC, wide), BF16),
            pltpu.VMEM((_N_SLOTS, HC, wide), BF16),
            pltpu.VMEM((1, wide), F32),
            pltpu.VMEM((1, wide), F32),
        ],
        compiler_params=_cparams(("parallel", "parallel"), 40),
        name="hgrn2",
    )(proj, proj, proj, proj, lb_params, norm_g.reshape(1, -1).astype(F32),
      jnp.asarray(mcat, BF16), jnp.asarray(lev, I32))


def _t5_bucket(rel):
    nb = N_BUCKETS // 2
    ret = jnp.where(rel > 0, nb, 0)
    n = jnp.abs(rel)
    max_exact = nb // 2
    nf = jnp.maximum(n, 1).astype(F32)
    large = max_exact + (jnp.log(nf / max_exact) / math.log(MAX_DISTANCE / max_exact)
                         * (nb - max_exact)).astype(I32)
    large = jnp.minimum(large, nb - 1)
    return ret + jnp.where(n < max_exact, n, large)


def _bias_table(rel_bias_table):
    v = jnp.arange(BIAS_ROWS, dtype=I32)[:, None]
    q = jnp.arange(LANES, dtype=I32)[None, :]
    onehot = jax.nn.one_hot(_t5_bucket(v - BIAS_ORIGIN - q), N_BUCKETS, dtype=F32)
    tab = jnp.einsum("vqb,bh->hvq", onehot, rel_bias_table.astype(F32), precision=lax.Precision.HIGHEST)
    return tab * LOG2_E


def _even_in_weight(w):
    glu_v, glu_g, gate_a, q, c, gate_b, qi, ki, wi = jnp.split(
        w, np.cumsum([D_A, D_A, D_A, H_B * DH_B, D_C, H_B * DH_B, H_IDX * D_IDX, D_IDX])[:8].tolist(), axis=1)
    pad = jnp.zeros((w.shape[0], E_NPAD - E_COL_WI - H_IDX), w.dtype)
    cols = [glu_v, glu_g, gate_a, q, gate_b, qi * (D_IDX ** -0.5), c, ki, ki, wi * (H_IDX ** -0.5), pad]
    return jnp.concatenate(cols, axis=1).astype(BF16)


def kernel(x, meta_tokens, norm_gain, final_norm_gain, rel_bias_table, w_in_even, conv_w, conv_b,
           conv_ln_gain, conv_ln_bias, kv_norm_gain, w_uk, w_uv, w_out_even, w_in_odd, lb_logits,
           rec_norm_gain, w_out_odd):
    bsz, seq, d = x.shape
    assert d == D_MODEL and seq % LANES == 0
    depth = norm_gain.shape[0]
    tp = FRONT_PAD + N_META + seq
    topk = min(TOPK_MAX, seq // 4)
    tt_mm = tp // 2 if (tp // 2) % 16 == 0 else tp
    tt_out = tp // 4 if (tp // 4) % 16 == 0 else tp
    tt_conv = tp // 8 if (tp // 8) % CONV_RB == 0 else tp // 2 if (tp // 2) % CONV_RB == 0 else tp
    h = jnp.concatenate([
        jnp.zeros((bsz, FRONT_PAD, d), F32),
        jnp.broadcast_to(meta_tokens.astype(F32), (bsz, N_META, d)),
        x.astype(F32)], axis=1)

    lb_soft = jax.nn.softmax(lb_logits.astype(F32), axis=0)
    lower_bounds = jnp.cumsum(lb_soft, axis=0) - lb_soft[0]
    bias_tab = _bias_table(rel_bias_table)

    for layer in range(depth):
        last = layer == depth - 1
        fg = final_norm_gain if last else None
        if layer % 2 == 0:
            e = layer // 2
            proj = norm_matmul(h, norm_gain[layer], _even_in_weight(w_in_even[e]), tt=tt_mm, tn=E_NPAD // 4)
            a_out = conv_module(proj, conv_w[e], conv_b[e], conv_ln_gain[e], conv_ln_bias[e], tt=tt_conv)
            wuk_t = jnp.transpose(w_uk[e], (0, 2, 1)).astype(BF16)
            wuv_t = jnp.transpose(w_uv[e], (0, 2, 1)).astype(BF16)
            b_out = dsa_attention(proj, kv_norm_gain[e], wuk_t, wuv_t, bias_tab, topk=topk)
            h = out_proj([a_out, b_out], w_out_even[e].astype(BF16), h, fg, tt=tt_out)
        else:
            o = layer // 2
            lb = lower_bounds[layer]
            lb_params = jnp.stack([lb, 1.0 - lb], axis=0)
            proj = norm_matmul(h, norm_gain[layer], w_in_odd[o].astype(BF16), tt=tt_mm, tn=2048)
            rec = hgrn2(proj, lb_params, rec_norm_gain[o])
            h = out_proj([rec], w_out_odd[o].astype(BF16), h, fg, tt=tt_out)
    return h[:, FRONT_PAD + N_META:]
```

```python
import functools
import math

import numpy as np
import jax
import jax.numpy as jnp
from jax import lax
from jax.experimental import pallas as pl
from jax.experimental.pallas import tpu as pltpu

F32 = jnp.float32
BF16 = jnp.bfloat16
I32 = jnp.int32
I16 = jnp.int16

D_MODEL = 2048
CHUNK = 64
N_META = 16
D_A = 1024
CONV_WIDTH = 31
H_B = 8
DH_B = 128
D_C = 256
H_IDX = 16
D_IDX = 64
TOPK_MAX = 256
N_BUCKETS = 32
MAX_DISTANCE = 128
H_C = 16
DK_C = 128
RMS_EPS = 1e-6
LN_EPS = 1e-5
NEG_INF = -1e30
INT_MIN = -(2 ** 31)
LOG2_E = 1.4426950408889634

LANES = 128
SUBLANES = 8
PACK16 = 16
I16_MIN = -(2 ** 15)
MXU_N = 256

FRONT_PAD = LANES - N_META
QB = LANES
KB = MXU_N
HC = 128
N_LEVELS = 7

E_NPAD = 6656
E_COL_Q, E_COL_GB, E_COL_QI = 3072, 4096, 5120
E_COL_C, E_COL_KK, E_COL_WI = 6144, 6400, 6528

MIB = 1024 * 1024


def _cparams(sem, vmem_mib, flags=None):
    return pltpu.CompilerParams(dimension_semantics=sem, vmem_limit_bytes=vmem_mib * MIB, flags=flags)


def _sigmoid(x):
    return 1.0 / (1.0 + jnp.exp(-x))


def _silu(x):
    h = 0.5 * x
    return h + h * jnp.tanh(h)


_NN = (((1,), (0,)), ((), ()))
_NT = (((1,), (1,)), ((), ()))


def _dot(a, b, dims=_NN):
    return lax.dot_general(a, b, dims, preferred_element_type=F32)


def _norm_matmul_kernel(x_ref, g_ref, w_ref, o_ref, hn_ref):
    @pl.when(pl.program_id(2) == 0)
    def _():
        x = x_ref[...]
        ms = jnp.mean(x * x, axis=-1, keepdims=True)
        hn_ref[...] = (x * lax.rsqrt(ms + RMS_EPS) * g_ref[...]).astype(BF16)

    o_ref[...] = _dot(hn_ref[...], w_ref[...]).astype(o_ref.dtype)


def norm_matmul(h, gain, w, *, tt, tn):
    bsz, tp, d = h.shape
    n = w.shape[1]
    return pl.pallas_call(
        _norm_matmul_kernel,
        grid=(bsz, tp // tt, n // tn),
        in_specs=[
            pl.BlockSpec((None, tt, d), lambda b, i, j: (b, i, 0)),
            pl.BlockSpec((1, d), lambda b, i, j: (0, 0)),
            pl.BlockSpec((d, tn), lambda b, i, j: (0, j)),
        ],
        out_specs=pl.BlockSpec((None, tt, tn), lambda b, i, j: (b, i, j)),
        out_shape=jax.ShapeDtypeStruct((bsz, tp, n), BF16),
        scratch_shapes=[pltpu.VMEM((tt, d), BF16)],
        compiler_params=_cparams(("parallel", "parallel", "arbitrary"), 56),
        name="norm_matmul",
    )(h, gain.reshape(1, d).astype(F32), w)


def _out_proj_kernel(*refs, n_in, tt, final):
    xs = refs[:n_in]
    w_ref, h_ref = refs[n_in], refs[n_in + 1]
    g_ref = refs[n_in + 2] if final else None
    o_ref = refs[-1]
    y = None
    off = 0
    for x_ref in xs:
        kdim = x_ref.shape[-1]
        part = _dot(x_ref[...], w_ref[off:off + kdim, :])
        y = part if y is None else y + part
        off += kdim
    row = pl.program_id(1) * tt + lax.broadcasted_iota(I32, (tt, 1), 0)
    hn = jnp.where(row >= FRONT_PAD, h_ref[...] + y, 0.0)
    if final:
        ms = jnp.mean(hn * hn, axis=-1, keepdims=True)
        hn = hn * lax.rsqrt(ms + RMS_EPS) * g_ref[...]
    o_ref[...] = hn


def out_proj(xs, w, h, final_gain=None, *, tt):
    bsz, tp, d = h.shape
    final = final_gain is not None
    in_specs = [pl.BlockSpec((None, tt, x.shape[-1]), lambda b, i: (b, i, 0)) for x in xs]
    in_specs += [
        pl.BlockSpec(w.shape, lambda b, i: (0, 0)),
        pl.BlockSpec((None, tt, d), lambda b, i: (b, i, 0)),
    ]
    args = list(xs) + [w, h]
    if final:
        in_specs.append(pl.BlockSpec((1, d), lambda b, i: (0, 0)))
        args.append(final_gain.reshape(1, d).astype(F32))
    return pl.pallas_call(
        functools.partial(_out_proj_kernel, n_in=len(xs), tt=tt, final=final),
        grid=(bsz, tp // tt),
        in_specs=in_specs,
        out_specs=pl.BlockSpec((None, tt, d), lambda b, i: (b, i, 0)),
        out_shape=jax.ShapeDtypeStruct((bsz, tp, d), F32),
        compiler_params=_cparams(("parallel", "parallel"), 56),
        name="out_proj_final" if final else "out_proj",
    )(*args)


CONV_HALO = 32
CONV_RB = 16
CONV_CHAINS = 4


def _conv_kernel(v_ref, g_ref, ga_ref, w_ref, cb_ref, lg_ref, lb_ref, o_ref, us_ref, cv_ref, *, tt):
    j = pl.program_id(1)
    u_new = v_ref[...].astype(F32) * _sigmoid(g_ref[...].astype(F32))
    span = tt + CONV_HALO - SUBLANES
    n_rb = tt // CONV_RB
    for cb in range(D_A // LANES):
        lanes = slice(cb * LANES, (cb + 1) * LANES)

        @pl.when(j == 0)
        def _(cb=cb):
            us_ref[0, cb, 0:CONV_HALO, :] = jnp.zeros((CONV_HALO, LANES), F32)

        @pl.when(j > 0)
        def _(cb=cb):
            us_ref[0, cb, 0:CONV_HALO, :] = us_ref[0, cb, tt:tt + CONV_HALO, :]

        us_ref[0, cb, CONV_HALO:CONV_HALO + tt, :] = u_new[:, lanes]
        for r in range(1, SUBLANES):
            us_ref[r, cb, 0:span, :] = us_ref[0, cb, r:r + span, :]

        wb = [jnp.broadcast_to(w_ref[k:k + 1, lanes], (CONV_RB, LANES)) for k in range(CONV_WIDTH)]
        bias = jnp.broadcast_to(cb_ref[0:1, lanes], (CONV_RB, LANES))

        def rb_body(rb, carry, cb=cb, lanes=lanes, wb=wb, bias=bias):
            base = pl.multiple_of(rb * CONV_RB, CONV_RB)
            parts = [bias] + [None] * (CONV_CHAINS - 1)
            for k in range(CONV_WIDTH):
                a, r = divmod(k + CONV_HALO - (CONV_WIDTH - 1), SUBLANES)
                term = wb[k] * us_ref[r, cb, pl.ds(base + SUBLANES * a, CONV_RB), :]
                c = k % CONV_CHAINS
                parts[c] = term if parts[c] is None else parts[c] + term
            cv_ref[pl.ds(base, CONV_RB), lanes] = (parts[0] + parts[1]) + (parts[2] + parts[3])
            return carry

        lax.fori_loop(0, n_rb, rb_body, 0)

    x = cv_ref[...]
    mu = jnp.mean(x, axis=-1, keepdims=True)
    xc = x - mu
    var = jnp.mean(xc * xc, axis=-1, keepdims=True)
    y = xc * lax.rsqrt(var + LN_EPS) * lg_ref[...] + lb_ref[...]
    ga = ga_ref[...].astype(F32)
    o_ref[...] = (_silu(y) * _silu(ga)).astype(o_ref.dtype)


def conv_module(proj, conv_w, conv_b, ln_g, ln_b, *, tt):
    bsz, tp, _ = proj.shape
    row = lambda a: a.reshape(1, D_A).astype(F32)
    return pl.pallas_call(
        functools.partial(_conv_kernel, tt=tt),
        grid=(bsz, tp // tt),
        in_specs=[
            pl.BlockSpec((None, tt, D_A), lambda b, j: (b, j, 0)),
            pl.BlockSpec((None, tt, D_A), lambda b, j: (b, j, 1)),
            pl.BlockSpec((None, tt, D_A), lambda b, j: (b, j, 2)),
            pl.BlockSpec((CONV_WIDTH, D_A), lambda b, j: (0, 0)),
            pl.BlockSpec((1, D_A), lambda b, j: (0, 0)),
            pl.BlockSpec((1, D_A), lambda b, j: (0, 0)),
            pl.BlockSpec((1, D_A), lambda b, j: (0, 0)),
        ],
        out_specs=pl.BlockSpec((None, tt, D_A), lambda b, j: (b, j, 0)),
        out_shape=jax.ShapeDtypeStruct((bsz, tp, D_A), BF16),
        scratch_shapes=[
            pltpu.VMEM((SUBLANES, D_A // LANES, tt + CONV_HALO, LANES), F32),
            pltpu.VMEM((tt, D_A), F32),
        ],
        compiler_params=_cparams(("parallel", "arbitrary"), 40),
        name="conv_module",
    )(proj, proj, proj, conv_w.astype(F32), row(conv_b), row(ln_g), row(ln_b))


BIAS_ROWS = 640
BIAS_ORIGIN = 384


def _attn_kernel(c_ref, kk_ref, wi_ref, q_ref, qi_ref, gb_ref, kvg_ref, wuk_ref, wuvt_ref, bias_ref,
                 o_ref, cn_s, cnt_s, kk_s, st_s, hi_s, lo_s, qm_s, qlat_s, lg_a, lg_b, acc_s, m_s, l_s,
                 tau_s, *, tp, topk):
    i = pl.program_id(1)
    n_kb = (i + 2) // 2
    n_kb_all = cn_s.shape[0]

    @pl.when(i == 0)
    def _prepare_keys():
        gain = kvg_ref[...]
        for kb in range(n_kb_all):
            lo = kb * KB
            nrow = min(KB, tp - lo)
            c = c_ref[lo:lo + nrow, :].astype(F32)
            cn = c * lax.rsqrt(jnp.mean(c * c, axis=-1, keepdims=True) + RMS_EPS) * gain
            cn_s[kb, 0:nrow, :] = cn.astype(BF16)
            kk_s[kb, 0:nrow, :] = kk_ref[lo:lo + nrow, :]
            for rb in range(nrow // LANES):
                for cb in range(D_C // LANES):
                    tile = cn[rb * LANES:(rb + 1) * LANES, cb * LANES:(cb + 1) * LANES]
                    cnt_s[kb, cb * LANES:(cb + 1) * LANES, rb * LANES:(rb + 1) * LANES] = tile.T.astype(BF16)
            if nrow < KB:
                cn_s[kb, nrow:KB, :] = jnp.zeros((KB - nrow, D_C), BF16)
                kk_s[kb, nrow:KB, :] = jnp.zeros((KB - nrow, LANES), BF16)
                cnt_s[kb, :, nrow:KB] = jnp.zeros((D_C, KB - nrow), BF16)

    for h in range(H_B):
        qh = q_ref[:, h * DH_B:(h + 1) * DH_B]
        ql = _dot(qh, wuk_ref[h]) * (DH_B ** -0.5 * LOG2_E)
        qlat_s[h * QB:(h + 1) * QB, :] = ql.astype(BF16)

    lane = lax.broadcasted_iota(I32, (QB, LANES), 1)
    for p in range(H_IDX // 2):
        qp = qi_ref[:, p * LANES:(p + 1) * LANES]
        zero = jnp.zeros_like(qp)
        qm_s[p, 0:QB, :] = jnp.where(lane < D_IDX, qp, zero)
        qm_s[p, QB:2 * QB, :] = jnp.where(lane >= D_IDX, qp, zero)

    wit = wi_ref[...].astype(F32).T

    def score_body(kb, carry):
        kk = kk_s[kb]
        acc = jnp.zeros((KB, QB), F32)
        for p in range(H_IDX // 2):
            d2 = _dot(kk, qm_s[p], _NT)
            acc = acc + jnp.maximum(d2[:, 0:QB], 0.0) * wit[2 * p:2 * p + 1, :]
            acc = acc + jnp.maximum(d2[:, QB:2 * QB], 0.0) * wit[2 * p + 1:2 * p + 2, :]
        bits = lax.bitcast_convert_type(acc, I32)
        key = bits ^ ((bits >> 31) & 0x7FFFFFFF)
        s_idx = kb * KB + lax.broadcasted_iota(I32, (KB, QB), 0)
        t_idx = i * QB + lax.broadcasted_iota(I32, (KB, QB), 1)
        adm = (s_idx >= FRONT_PAD) & (jnp.maximum(s_idx >> 6, 1) <= jnp.maximum(t_idx >> 6, 1))
        key = jnp.where(adm, key, INT_MIN)
        st_s[kb] = key
        hi_s[kb] = key >> 16
        return carry

    lax.fori_loop(0, n_kb, score_body, 0)

    def search16(ref, need, n_blocks):
        def bit_body(it, carry):
            t, above = carry
            cand = t + lax.shift_left(jnp.int32(1), jnp.int32(15) - it)
            cand8 = jnp.broadcast_to(cand, (SUBLANES, QB))[None]
            parts = []
            for kb in range(n_blocks):
                below = (ref[kb].reshape(KB // SUBLANES, SUBLANES, QB) - cand8) >> 31
                parts += [below[g] for g in range(KB // SUBLANES)]
            while len(parts) > 1:
                nxt = [parts[g] + parts[g + 1] for g in range(0, len(parts) - 1, 2)]
                parts = nxt + parts[len(parts) - len(parts) % 2:]
            cnt = n_blocks * KB + jnp.sum(parts[0], axis=0, keepdims=True)
            ok = cnt >= need
            return jnp.where(ok, cand, t), jnp.where(ok, above, cnt)

        init = (jnp.full((1, QB), I16_MIN, I32), jnp.zeros((1, QB), I32))
        return lax.fori_loop(0, 16, bit_body, init)

    st_s[n_kb] = jnp.full((KB, QB), INT_MIN, I32)
    hi_s[n_kb] = jnp.full((KB, QB), I16_MIN, I32)
    n_pairs = (n_kb + 1) // 2
    few_keys = N_META + QB * i <= topk

    @pl.when(few_keys)
    def _():
        tau_s[...] = jnp.full((1, QB), INT_MIN, I32)

    for pairs in range(1, (n_kb_all + 1) // 2 + 1):
        @pl.when(jnp.logical_and(n_pairs == pairs, jnp.logical_not(few_keys)))
        def _(n_blocks=2 * pairs):
            tau_hi, n_above = search16(hi_s, jnp.full((1, QB), topk, I32), n_blocks)
            for kb in range(n_blocks):
                key = st_s[kb]
                low = (key & 0xFFFF) + I16_MIN
                lo_s[kb] = jnp.where((key >> 16) == tau_hi, low, I16_MIN)
            tau_lo, _ = search16(lo_s, topk - n_above, n_blocks)
            tau_s[...] = lax.shift_left(tau_hi, jnp.int32(16)) | (tau_lo - I16_MIN)

    tau = jnp.maximum(tau_s[...], INT_MIN + 1)

    m_s[...] = jnp.full(m_s.shape, NEG_INF, F32)
    l_s[...] = jnp.zeros(l_s.shape, F32)
    acc_s[...] = jnp.zeros(acc_s.shape, F32)

    def logits_stage(kb, lg_ref):
        lg_ref[...] = _dot(cn_s[kb], qlat_s[...], _NT)

    def softmax_stage(kb, lg_ref):
        neg = jnp.where(st_s[kb] >= tau, 0.0, NEG_INF)
        v0 = pl.multiple_of(jnp.maximum(KB * kb - QB * i + BIAS_ORIGIN, 0), LANES)
        ps, alphas = [], []
        for h in range(H_B):
            lh = lg_ref[:, h * QB:(h + 1) * QB] + bias_ref[h, pl.ds(v0, KB), :] + neg
            m_old = m_s[h:h + 1, :]
            m_new = jnp.maximum(m_old, jnp.max(lh, axis=0, keepdims=True))
            p = jnp.exp2(lh - m_new)
            alpha = jnp.exp2(m_old - m_new)
            l_s[h:h + 1, :] = alpha * l_s[h:h + 1, :] + jnp.sum(p, axis=0, keepdims=True)
            m_s[h:h + 1, :] = m_new
            ps.append(p.astype(BF16))
            alphas.append(alpha)
        pv = _dot(cnt_s[kb], jnp.concatenate(ps, axis=1))
        acc_s[...] = acc_s[...] * jnp.concatenate(alphas, axis=1) + pv

    logits_stage(0, lg_a)

    def att_pair(j, carry):
        kb = 2 * j
        logits_stage(kb + 1, lg_b)
        softmax_stage(kb, lg_a)
        logits_stage(jnp.minimum(kb + 2, n_kb - 1), lg_a)
        softmax_stage(kb + 1, lg_b)
        return carry

    lax.fori_loop(0, n_kb // 2, att_pair, 0)

    @pl.when(n_kb % 2 == 1)
    def _():
        softmax_stage(n_kb - 1, lg_a)

    for h in range(H_B):
        cols = slice(h * QB, (h + 1) * QB)
        ot = (acc_s[:, cols] * (1.0 / l_s[h:h + 1, :])).astype(BF16)
        bt = _dot(wuvt_ref[h], ot)
        gate = gb_ref[:, cols].astype(F32)
        o_ref[:, cols] = (bt.T * _silu(gate)).astype(o_ref.dtype)


def dsa_attention(proj, kv_gain, wuk_t, wuv_t, bias_tab, *, topk):
    bsz, tp, _ = proj.shape
    n_qb = tp // QB
    n_kb_all = -(-tp // KB)
    return pl.pallas_call(
        functools.partial(_attn_kernel, tp=tp, topk=topk),
        grid=(bsz, n_qb),
        in_specs=[
            pl.BlockSpec((None, tp, D_C), lambda b, i: (b, 0, E_COL_C // D_C)),
            pl.BlockSpec((None, tp, LANES), lambda b, i: (b, 0, E_COL_KK // LANES)),
            pl.BlockSpec((None, QB, LANES), lambda b, i: (b, i, E_COL_WI // LANES)),
            pl.BlockSpec((None, QB, H_B * DH_B), lambda b, i: (b, i, E_COL_Q // (H_B * DH_B))),
            pl.BlockSpec((None, QB, H_IDX * D_IDX), lambda b, i: (b, i, E_COL_QI // (H_IDX * D_IDX))),
            pl.BlockSpec((None, QB, H_B * DH_B), lambda b, i: (b, i, E_COL_GB // (H_B * DH_B))),
            pl.BlockSpec((1, D_C), lambda b, i: (0, 0)),
            pl.BlockSpec((H_B, DH_B, D_C), lambda b, i: (0, 0, 0)),
            pl.BlockSpec((H_B, DH_B, D_C), lambda b, i: (0, 0, 0)),
            pl.BlockSpec((H_B, BIAS_ROWS, LANES), lambda b, i: (0, 0, 0)),
        ],
        out_specs=pl.BlockSpec((None, QB, H_B * DH_B), lambda b, i: (b, i, 0)),
        out_shape=jax.ShapeDtypeStruct((bsz, tp, H_B * DH_B), BF16),
        scratch_shapes=[
            pltpu.VMEM((n_kb_all, KB, D_C), BF16),
            pltpu.VMEM((n_kb_all, D_C, KB), BF16),
            pltpu.VMEM((n_kb_all, KB, LANES), BF16),
            pltpu.VMEM((n_kb_all + 1, KB, QB), I32),
            pltpu.VMEM((n_kb_all + 1, KB, QB), I32),
            pltpu.VMEM((n_kb_all + 1, KB, QB), I32),
            pltpu.VMEM((H_IDX // 2, 2 * QB, LANES), BF16),
            pltpu.VMEM((H_B * QB, D_C), BF16),
            pltpu.VMEM((KB, H_B * QB), F32),
            pltpu.VMEM((KB, H_B * QB), F32),
            pltpu.VMEM((D_C, H_B * QB), F32),
            pltpu.VMEM((H_B, QB), F32),
            pltpu.VMEM((H_B, QB), F32),
            pltpu.VMEM((1, QB), I32),
        ],
        compiler_params=_cparams(("parallel", "arbitrary"), 40),
        name="dsa_attention",
    )(proj, proj, proj, proj, proj, proj, kv_gain.reshape(1, D_C).astype(F32), wuk_t, wuv_t, bias_tab)


def _hgrn_constants():
    n = HC
    t = np.arange(n)[:, None]
    j = np.arange(n)[None, :]
    mats = [(j <= t), (j > t)]
    for lvl in range(N_LEVELS):
        upper = ((t >> lvl) & 1) == 1
        m_up = (t >> lvl) << lvl
        m_lo = ((t >> lvl) + 1) << lvl
        mats.append(np.where(upper, (j >= m_up) & (j <= t), (j > t) & (j < m_lo)))
    m = np.concatenate(mats, axis=0).astype(np.float32)
    mcat = np.concatenate([m, m], axis=1)
    x = t ^ j
    lev = np.where(j < t, np.floor(np.log2(np.maximum(x, 1))).astype(np.int32), np.where(j == t, -1, -2))
    masks = np.stack([(lev == lvl) for lvl in range(-1, N_LEVELS)]).astype(np.float32)
    return mcat, masks


HGRN_HEADS_PER_STEP = 4


def _role_select(qf, k, lvl, up_small):
    half = 1 << lvl
    if half >= SUBLANES:
        pieces = []
        for blk in range(HC // (2 * half)):
            lo = 2 * half * blk
            pieces += [k[lo:lo + half], qf[lo + half:lo + 2 * half]]
        return jnp.concatenate(pieces, axis=0)
    g, w = HC // SUBLANES, qf.shape[-1]
    sel = jnp.where(up_small[lvl], qf.reshape(g, SUBLANES, w), k.reshape(g, SUBLANES, w))
    return sel.reshape(HC, w)


_SLOT_QE, _SLOT_KT, _SLOT_QB, _SLOT_KB, _SLOT_Y0 = 0, 1, 2, 3, 4
_N_SLOTS = _SLOT_Y0 + N_LEVELS


def _hgrn_kernel(q_ref, fz_ref, i_ref, g_ref, lbp_ref, ng_ref, mcat_ref, mask_ref, o_ref,
                 s_ref, stage_a, stage_b, dec_a, dec_b, *, n_steps, nh):
    wide = nh * DK_C
    row8 = lax.broadcasted_iota(I32, (1, SUBLANES, wide), 1)
    up_small = [((row8 >> lvl) & 1) == 1 for lvl in range(3)]
    s_ref[...] = jnp.zeros(s_ref.shape, F32)

    def gate_stage(c):
        rows = pl.ds(pl.multiple_of(c * HC, HC), HC)
        lb = lbp_ref[0:1, :]
        one_m_lb = lbp_ref[1:2, :]
        qf = _silu(q_ref[rows, :].astype(F32))
        z = fz_ref[rows, :].astype(F32)
        w = one_m_lb * (0.5 + 0.5 * jnp.tanh(0.5 * z))
        k = one_m_lb - w
        lf2 = jnp.log(lb + w) * LOG2_E
        hi = lf2.astype(BF16)
        lo = (lf2 - hi.astype(F32)).astype(BF16)
        return qf, k, _dot(mcat_ref[...], jnp.concatenate([hi, lo], axis=0))

    def operand_stage(gates, stage, dec_ref):
        qf, k, gsum = gates
        ex = jnp.exp2(gsum)
        e_b = ex[0:HC]
        stage[_SLOT_QE] = (qf * e_b).astype(BF16)
        stage[_SLOT_KT] = (k * ex[HC:2 * HC]).astype(BF16)
        stage[_SLOT_QB] = qf.astype(BF16)
        stage[_SLOT_KB] = k.astype(BF16)
        for lvl in range(N_LEVELS):
            y = _role_select(qf, k, lvl, up_small) * ex[(2 + lvl) * HC:(3 + lvl) * HC]
            stage[_SLOT_Y0 + lvl] = y.astype(BF16)
        dec_ref[...] = e_b[HC - 1:HC, :]

    def level_stage(c, stage):
        lns = [slice(hh * DK_C, (hh + 1) * DK_C) for hh in range(nh)]
        sts = [s_ref[hh] for hh in range(nh)]
        outs = [_dot(stage[_SLOT_QE, :, ln], st.astype(BF16), _NT) for ln, st in zip(lns, sts)]
        accs = [_dot(stage[_SLOT_QB, :, ln], stage[_SLOT_KB, :, ln], _NT) * mask_ref[0] for ln in lns]
        for lvl in range(N_LEVELS):
            for hh, ln in enumerate(lns):
                y = stage[_SLOT_Y0 + lvl, :, ln]
                accs[hh] = accs[hh] + _dot(y, y, _NT) * mask_ref[lvl + 1]
        return sts, outs, accs

    def output_stage(c, partial, stage, dec_ref):
        sts, outs, accs = partial
        rows = pl.ds(pl.multiple_of(c * HC, HC), HC)
        lns = [slice(hh * DK_C, (hh + 1) * DK_C) for hh in range(nh)]
        vhs = [i_ref[rows, ln] for ln in lns]
        for hh, ln in enumerate(lns):
            vt = vhs[hh].astype(F32).T.astype(BF16)
            s_ref[hh] = sts[hh] * dec_ref[:, ln] + _dot(vt, stage[_SLOT_KT, :, ln])
        normed = []
        for hh in range(nh):
            o = outs[hh] + _dot(accs[hh].astype(BF16), vhs[hh])
            normed.append(o * lax.rsqrt(jnp.mean(o * o, axis=-1, keepdims=True) + RMS_EPS))
        on = jnp.concatenate(normed, axis=1) * ng_ref[...]
        gx = g_ref[rows, :].astype(F32)
        o_ref[rows, :] = (on * _silu(gx)).astype(o_ref.dtype)

    last = n_steps - 1
    operand_stage(gate_stage(0), stage_a, dec_a)

    def pair(j, carry):
        c = 2 * j
        partial = level_stage(c, stage_a)
        gates = gate_stage(c + 1)
        output_stage(c, partial, stage_a, dec_a)
        operand_stage(gates, stage_b, dec_b)
        partial = level_stage(c + 1, stage_b)
        gates = gate_stage(jnp.minimum(c + 2, last))
        output_stage(c + 1, partial, stage_b, dec_b)
        operand_stage(gates, stage_a, dec_a)
        return carry

    lax.fori_loop(0, n_steps // 2, pair, 0)
    if n_steps % 2:
        output_stage(last, level_stage(last, stage_a), stage_a, dec_a)


def hgrn2(proj, lb_params, norm_g):
    bsz, tp, _ = proj.shape
    mcat, masks = _hgrn_constants()
    nh = HGRN_HEADS_PER_STEP
    wide = nh * DK_C
    n_groups = H_C // nh
    col = lambda off: pl.BlockSpec((None, tp, wide), lambda b, h, off=off: (b, 0, off + h))
    return pl.pallas_call(
        functools.partial(_hgrn_kernel, n_steps=tp // HC, nh=nh),
        grid=(bsz, n_groups),
        in_specs=[
            col(0), col(n_groups), col(2 * n_groups), col(3 * n_groups),
            pl.BlockSpec((2, wide), lambda b, h: (0, h)),
            pl.BlockSpec((1, wide), lambda b, h: (0, h)),
            pl.BlockSpec(mcat.shape, lambda b, h: (0, 0)),
            pl.BlockSpec(masks.shape, lambda b, h: (0, 0, 0)),
        ],
        out_specs=pl.BlockSpec((None, tp, wide), lambda b, h: (b, 0, h)),
        out_shape=jax.ShapeDtypeStruct((bsz, tp, H_C * DK_C), BF16),
        scratch_shapes=[
            pltpu.VMEM((nh, DK_C, DK_C), F32),
            pltpu.VMEM((_N_SLOTS, HC, wide), BF16),
            pltpu.VMEM((_N_SLOTS, HC, wide), BF16),
            pltpu.VMEM((1, wide), F32),
            pltpu.VMEM((1, wide), F32),
        ],
        compiler_params=_cparams(("parallel", "parallel"), 40),
        name="hgrn2",
    )(proj, proj, proj, proj, lb_params, norm_g.reshape(1, -1).astype(F32),
      jnp.asarray(mcat, BF16), jnp.asarray(masks, F32))


def _t5_bucket(rel):
    nb = N_BUCKETS // 2
    ret = jnp.where(rel > 0, nb, 0)
    n = jnp.abs(rel)
    max_exact = nb // 2
    nf = jnp.maximum(n, 1).astype(F32)
    large = max_exact + (jnp.log(nf / max_exact) / math.log(MAX_DISTANCE / max_exact)
                         * (nb - max_exact)).astype(I32)
    large = jnp.minimum(large, nb - 1)
    return ret + jnp.where(n < max_exact, n, large)


def _bias_table(rel_bias_table):
    v = jnp.arange(BIAS_ROWS, dtype=I32)[:, None]
    q = jnp.arange(LANES, dtype=I32)[None, :]
    onehot = jax.nn.one_hot(_t5_bucket(v - BIAS_ORIGIN - q), N_BUCKETS, dtype=F32)
    tab = jnp.einsum("vqb,bh->hvq", onehot, rel_bias_table.astype(F32), precision=lax.Precision.HIGHEST)
    return tab * LOG2_E


def _even_in_weight(w):
    glu_v, glu_g, gate_a, q, c, gate_b, qi, ki, wi = jnp.split(
        w, np.cumsum([D_A, D_A, D_A, H_B * DH_B, D_C, H_B * DH_B, H_IDX * D_IDX, D_IDX])[:8].tolist(), axis=1)
    pad = jnp.zeros((w.shape[0], E_NPAD - E_COL_WI - H_IDX), w.dtype)
    cols = [glu_v, glu_g, gate_a, q, gate_b, qi * (D_IDX ** -0.5), c, ki, ki, wi * (H_IDX ** -0.5), pad]
    return jnp.concatenate(cols, axis=1).astype(BF16)


def kernel(x, meta_tokens, norm_gain, final_norm_gain, rel_bias_table, w_in_even, conv_w, conv_b,
           conv_ln_gain, conv_ln_bias, kv_norm_gain, w_uk, w_uv, w_out_even, w_in_odd, lb_logits,
           rec_norm_gain, w_out_odd):
    bsz, seq, d = x.shape
    assert d == D_MODEL and seq % LANES == 0
    depth = norm_gain.shape[0]
    tp = FRONT_PAD + N_META + seq
    topk = min(TOPK_MAX, seq // 4)
    tt_mm = tp // 2 if (tp // 2) % 16 == 0 else tp
    tt_out = tp // 4 if (tp // 4) % 16 == 0 else tp
    tt_conv = tp // 8 if (tp // 8) % CONV_RB == 0 else tp // 2 if (tp // 2) % CONV_RB == 0 else tp
    h = jnp.concatenate([
        jnp.zeros((bsz, FRONT_PAD, d), F32),
        jnp.broadcast_to(meta_tokens.astype(F32), (bsz, N_META, d)),
        x.astype(F32)], axis=1)

    lb_soft = jax.nn.softmax(lb_logits.astype(F32), axis=0)
    lower_bounds = jnp.cumsum(lb_soft, axis=0) - lb_soft[0]
    bias_tab = _bias_table(rel_bias_table)

    for layer in range(depth):
        last = layer == depth - 1
        fg = final_norm_gain if last else None
        if layer % 2 == 0:
            e = layer // 2
            proj = norm_matmul(h, norm_gain[layer], _even_in_weight(w_in_even[e]), tt=tt_mm, tn=E_NPAD // 4)
            a_out = conv_module(proj, conv_w[e], conv_b[e], conv_ln_gain[e], conv_ln_bias[e], tt=tt_conv)
            wuk_t = jnp.transpose(w_uk[e], (0, 2, 1)).astype(BF16)
            wuv_t = jnp.transpose(w_uv[e], (0, 2, 1)).astype(BF16)
            b_out = dsa_attention(proj, kv_norm_gain[e], wuk_t, wuv_t, bias_tab, topk=topk)
            h = out_proj([a_out, b_out], w_out_even[e].astype(BF16), h, fg, tt=tt_out)
        else:
            o = layer // 2
            lb = lower_bounds[layer]
            lb_params = jnp.stack([lb, 1.0 - lb], axis=0)
            proj = norm_matmul(h, norm_gain[layer], w_in_odd[o].astype(BF16), tt=tt_mm, tn=2048)
            rec = hgrn2(proj, lb_params, rec_norm_gain[o])
            h = out_proj([rec], w_out_odd[o].astype(BF16), h, fg, tt=tt_out)
    return h[:, FRONT_PAD + N_META:]
```

```python
import functools
import math

import numpy as np
import jax
import jax.numpy as jnp
from jax import lax
from jax.experimental import pallas as pl
from jax.experimental.pallas import tpu as pltpu

F32 = jnp.float32
BF16 = jnp.bfloat16
I32 = jnp.int32
I16 = jnp.int16

D_MODEL = 2048
CHUNK = 64
N_META = 16
D_A = 1024
CONV_WIDTH = 31
H_B = 8
DH_B = 128
D_C = 256
H_IDX = 16
D_IDX = 64
TOPK_MAX = 256
N_BUCKETS = 32
MAX_DISTANCE = 128
H_C = 16
DK_C = 128
RMS_EPS = 1e-6
LN_EPS = 1e-5
NEG_INF = -1e30
INT_MIN = -(2 ** 31)
LOG2_E = 1.4426950408889634

LANES = 128
SUBLANES = 8
PACK16 = 16
I16_MIN = -(2 ** 15)
MXU_N = 256

FRONT_PAD = LANES - N_META
QB = LANES
KB = MXU_N
HC = 128
N_LEVELS = 7

E_NPAD = 6656
E_COL_Q, E_COL_GB, E_COL_QI = 3072, 4096, 5120
E_COL_C, E_COL_KK, E_COL_WI = 6144, 6400, 6528

MIB = 1024 * 1024


def _cparams(sem, vmem_mib, flags=None):
    return pltpu.CompilerParams(dimension_semantics=sem, vmem_limit_bytes=vmem_mib * MIB, flags=flags)


def _sigmoid(x):
    return 1.0 / (1.0 + jnp.exp(-x))


def _silu(x):
    h = 0.5 * x
    return h + h * jnp.tanh(h)


_NN = (((1,), (0,)), ((), ()))
_NT = (((1,), (1,)), ((), ()))


def _dot(a, b, dims=_NN):
    return lax.dot_general(a, b, dims, preferred_element_type=F32)


def _rms_normed(x, gain):
    ms = jnp.mean(x * x, axis=-1, keepdims=True)
    return x * lax.rsqrt(ms + RMS_EPS) * gain


def _norm_matmul_kernel(x_ref, g_ref, w_ref, o_ref, hn_ref):
    @pl.when(pl.program_id(2) == 0)
    def _():
        hn_ref[...] = _rms_normed(x_ref[...], g_ref[...]).astype(BF16)

    o_ref[...] = _dot(hn_ref[...], w_ref[...]).astype(o_ref.dtype)


def _matmul_kernel(x_ref, w_ref, o_ref):
    o_ref[...] = _dot(x_ref[...], w_ref[...]).astype(o_ref.dtype)


def matmul(hn, w, *, tt, tn):
    bsz, tp, d = hn.shape
    n = w.shape[1]
    return pl.pallas_call(
        _matmul_kernel,
        grid=(bsz, tp // tt, n // tn),
        in_specs=[
            pl.BlockSpec((None, tt, d), lambda b, i, j: (b, i, 0)),
            pl.BlockSpec((d, tn), lambda b, i, j: (0, j)),
        ],
        out_specs=pl.BlockSpec((None, tt, tn), lambda b, i, j: (b, i, j)),
        out_shape=jax.ShapeDtypeStruct((bsz, tp, n), BF16),
        compiler_params=_cparams(("parallel", "parallel", "parallel"), 56),
        name="in_proj",
    )(hn, w)


def norm_matmul(h, gain, w, *, tt, tn):
    bsz, tp, d = h.shape
    n = w.shape[1]
    return pl.pallas_call(
        _norm_matmul_kernel,
        grid=(bsz, tp // tt, n // tn),
        in_specs=[
            pl.BlockSpec((None, tt, d), lambda b, i, j: (b, i, 0)),
            pl.BlockSpec((1, d), lambda b, i, j: (0, 0)),
            pl.BlockSpec((d, tn), lambda b, i, j: (0, j)),
        ],
        out_specs=pl.BlockSpec((None, tt, tn), lambda b, i, j: (b, i, j)),
        out_shape=jax.ShapeDtypeStruct((bsz, tp, n), BF16),
        scratch_shapes=[pltpu.VMEM((tt, d), BF16)],
        compiler_params=_cparams(("parallel", "parallel", "arbitrary"), 56),
        name="norm_matmul",
    )(h, gain.reshape(1, d).astype(F32), w)


def _out_proj_kernel(*refs, n_in, tt, final):
    xs = refs[:n_in]
    w_ref, h_ref, g_ref = refs[n_in], refs[n_in + 1], refs[n_in + 2]
    tile = lambda ref: ref[0] if len(ref.shape) == 3 else ref[...]
    y = None
    off = 0
    for x_ref in xs:
        kdim = x_ref.shape[-1]
        part = _dot(tile(x_ref), w_ref[off:off + kdim, :])
        y = part if y is None else y + part
        off += kdim
    h_new = tile(h_ref) + y
    if final:
        refs[-1][...] = _rms_normed(h_new, g_ref[...])
    else:
        row = pl.program_id(1) * tt + lax.broadcasted_iota(I32, (tt, 1), 0)
        h_new = jnp.where(row >= FRONT_PAD, h_new, 0.0)
        refs[-2][...] = h_new
        refs[-1][...] = _rms_normed(h_new, g_ref[...]).astype(BF16)


def out_proj(xs, w, h, gain, *, tt, final):
    bsz, tp, d = h.shape
    if final:
        first = FRONT_PAD + N_META
        rows_out = tp - first
        row_spec = lambda k: pl.BlockSpec((pl.Element(1), pl.Element(tt), pl.Element(k)),
                                          lambda b, i: (b, pl.multiple_of(first + i * tt, LANES), 0))
    else:
        rows_out = tp
        row_spec = lambda k: pl.BlockSpec((None, tt, k), lambda b, i: (b, i, 0))
    in_specs = [row_spec(x.shape[-1]) for x in xs]
    in_specs += [pl.BlockSpec(w.shape, lambda b, i: (0, 0)), row_spec(d), pl.BlockSpec((1, d), lambda b, i: (0, 0))]
    out_spec = pl.BlockSpec((None, tt, d), lambda b, i: (b, i, 0))
    out_f32 = jax.ShapeDtypeStruct((bsz, rows_out, d), F32)
    return pl.pallas_call(
        functools.partial(_out_proj_kernel, n_in=len(xs), tt=tt, final=final),
        grid=(bsz, rows_out // tt),
        in_specs=in_specs,
        out_specs=out_spec if final else [out_spec, out_spec],
        out_shape=out_f32 if final else [out_f32, jax.ShapeDtypeStruct((bsz, rows_out, d), BF16)],
        compiler_params=_cparams(("parallel", "parallel"), 56),
        name="out_proj_final" if final else "out_proj",
    )(*xs, w, h, gain.reshape(1, d).astype(F32))


CONV_HALO = 32
CONV_RB = 16
CONV_CHAINS = 4


def _conv_kernel(v_ref, g_ref, ga_ref, w_ref, cb_ref, lg_ref, lb_ref, o_ref, us_ref, cv_ref, *, tt):
    j = pl.program_id(1)
    u_new = v_ref[...].astype(F32) * _sigmoid(g_ref[...].astype(F32))
    span = tt + CONV_HALO - SUBLANES
    n_rb = tt // CONV_RB
    for cb in range(D_A // LANES):
        lanes = slice(cb * LANES, (cb + 1) * LANES)

        @pl.when(j == 0)
        def _(cb=cb):
            us_ref[0, cb, 0:CONV_HALO, :] = jnp.zeros((CONV_HALO, LANES), F32)

        @pl.when(j > 0)
        def _(cb=cb):
            us_ref[0, cb, 0:CONV_HALO, :] = us_ref[0, cb, tt:tt + CONV_HALO, :]

        us_ref[0, cb, CONV_HALO:CONV_HALO + tt, :] = u_new[:, lanes]
        for r in range(1, SUBLANES):
            us_ref[r, cb, 0:span, :] = us_ref[0, cb, r:r + span, :]

        wb = [jnp.broadcast_to(w_ref[k:k + 1, lanes], (CONV_RB, LANES)) for k in range(CONV_WIDTH)]
        bias = jnp.broadcast_to(cb_ref[0:1, lanes], (CONV_RB, LANES))

        def rb_body(rb, carry, cb=cb, lanes=lanes, wb=wb, bias=bias):
            base = pl.multiple_of(rb * CONV_RB, CONV_RB)
            parts = [bias] + [None] * (CONV_CHAINS - 1)
            for k in range(CONV_WIDTH):
                a, r = divmod(k + CONV_HALO - (CONV_WIDTH - 1), SUBLANES)
                term = wb[k] * us_ref[r, cb, pl.ds(base + SUBLANES * a, CONV_RB), :]
                c = k % CONV_CHAINS
                parts[c] = term if parts[c] is None else parts[c] + term
            cv_ref[pl.ds(base, CONV_RB), lanes] = (parts[0] + parts[1]) + (parts[2] + parts[3])
            return carry

        lax.fori_loop(0, n_rb, rb_body, 0)

    x = cv_ref[...]
    mu = jnp.mean(x, axis=-1, keepdims=True)
    xc = x - mu
    var = jnp.mean(xc * xc, axis=-1, keepdims=True)
    y = xc * lax.rsqrt(var + LN_EPS) * lg_ref[...] + lb_ref[...]
    ga = ga_ref[...].astype(F32)
    o_ref[...] = (_silu(y) * _silu(ga)).astype(o_ref.dtype)


def conv_module(proj, conv_w, conv_b, ln_g, ln_b, *, tt):
    bsz, tp, _ = proj.shape
    row = lambda a: a.reshape(1, D_A).astype(F32)
    return pl.pallas_call(
        functools.partial(_conv_kernel, tt=tt),
        grid=(bsz, tp // tt),
        in_specs=[
            pl.BlockSpec((None, tt, D_A), lambda b, j: (b, j, 0)),
            pl.BlockSpec((None, tt, D_A), lambda b, j: (b, j, 1)),
            pl.BlockSpec((None, tt, D_A), lambda b, j: (b, j, 2)),
            pl.BlockSpec((CONV_WIDTH, D_A), lambda b, j: (0, 0)),
            pl.BlockSpec((1, D_A), lambda b, j: (0, 0)),
            pl.BlockSpec((1, D_A), lambda b, j: (0, 0)),
            pl.BlockSpec((1, D_A), lambda b, j: (0, 0)),
        ],
        out_specs=pl.BlockSpec((None, tt, D_A), lambda b, j: (b, j, 0)),
        out_shape=jax.ShapeDtypeStruct((bsz, tp, D_A), BF16),
        scratch_shapes=[
            pltpu.VMEM((SUBLANES, D_A // LANES, tt + CONV_HALO, LANES), F32),
            pltpu.VMEM((tt, D_A), F32),
        ],
        compiler_params=_cparams(("parallel", "arbitrary"), 40),
        name="conv_module",
    )(proj, proj, proj, conv_w.astype(F32), row(conv_b), row(ln_g), row(ln_b))


BIAS_ROWS = 640
BIAS_ORIGIN = 384


def _attn_kernel(c_ref, kk_ref, wi_ref, q_ref, qi_ref, gb_ref, kvg_ref, wuk_ref, wuvt_ref, bias_ref,
                 o_ref, cn_s, cnt_s, kk_s, st_s, hi_s, lo_s, qlat_s, lg_a, lg_b, acc_s, m_s, l_s,
                 tau_s, *, tp, topk):
    i = pl.program_id(1)
    n_kb = (i + 2) // 2
    n_kb_all = cn_s.shape[0]

    @pl.when(i == 0)
    def _prepare_keys():
        gain = kvg_ref[...]
        for kb in range(n_kb_all):
            lo = kb * KB
            nrow = min(KB, tp - lo)
            c = c_ref[lo:lo + nrow, :].astype(F32)
            cn = c * lax.rsqrt(jnp.mean(c * c, axis=-1, keepdims=True) + RMS_EPS) * gain
            cn_s[kb, 0:nrow, :] = cn.astype(BF16)
            kk = kk_ref[lo:lo + nrow, :]
            lane = lax.broadcasted_iota(I32, kk.shape, 1)
            kk_s[0, kb, 0:nrow, :] = jnp.where(lane < D_IDX, kk, jnp.zeros_like(kk))
            kk_s[1, kb, 0:nrow, :] = jnp.where(lane >= D_IDX, kk, jnp.zeros_like(kk))
            for rb in range(nrow // LANES):
                for cb in range(D_C // LANES):
                    tile = cn[rb * LANES:(rb + 1) * LANES, cb * LANES:(cb + 1) * LANES]
                    cnt_s[kb, cb * LANES:(cb + 1) * LANES, rb * LANES:(rb + 1) * LANES] = tile.T.astype(BF16)
            if nrow < KB:
                cn_s[kb, nrow:KB, :] = jnp.zeros((KB - nrow, D_C), BF16)
                kk_s[:, kb, nrow:KB, :] = jnp.zeros((2, KB - nrow, LANES), BF16)
                cnt_s[kb, :, nrow:KB] = jnp.zeros((D_C, KB - nrow), BF16)

    for h in range(H_B):
        qh = q_ref[:, h * DH_B:(h + 1) * DH_B]
        ql = _dot(qh, wuk_ref[h]) * (DH_B ** -0.5 * LOG2_E)
        qlat_s[h * QB:(h + 1) * QB, :] = ql.astype(BF16)

    wit = wi_ref[...].astype(F32).T

    def score_block(kb):
        k_even = kk_s[0, kb]
        k_odd = kk_s[1, kb]
        acc = jnp.zeros((KB, QB), F32)
        for p in range(H_IDX // 2):
            qp = qi_ref[:, p * LANES:(p + 1) * LANES]
            acc = acc + jnp.maximum(_dot(k_even, qp, _NT), 0.0) * wit[2 * p:2 * p + 1, :]
            acc = acc + jnp.maximum(_dot(k_odd, qp, _NT), 0.0) * wit[2 * p + 1:2 * p + 2, :]
        bits = lax.bitcast_convert_type(acc, I32)
        key = bits ^ ((bits >> 31) & 0x7FFFFFFF)
        s_idx = kb * KB + lax.broadcasted_iota(I32, (KB, QB), 0)
        t_idx = i * QB + lax.broadcasted_iota(I32, (KB, QB), 1)
        adm = (s_idx >= FRONT_PAD) & (jnp.maximum(s_idx >> 6, 1) <= jnp.maximum(t_idx >> 6, 1))
        key = jnp.where(adm, key, INT_MIN)
        st_s[kb] = key
        hi_s[kb] = key >> 16

    def score_pair(j, carry):
        score_block(2 * j)
        score_block(jnp.minimum(2 * j + 1, n_kb_all - 1))
        return carry

    lax.fori_loop(0, (n_kb + 1) // 2, score_pair, 0)

    def search16(ref, need, n_blocks):
        def bit_body(it, carry):
            t, above = carry
            cand = t + lax.shift_left(jnp.int32(1), jnp.int32(15) - it)
            cand8 = jnp.broadcast_to(cand, (SUBLANES, QB))[None]
            parts = []
            for kb in range(n_blocks):
                below = (ref[kb].reshape(KB // SUBLANES, SUBLANES, QB) - cand8) >> 31
                parts += [below[g] for g in range(KB // SUBLANES)]
            while len(parts) > 1:
                nxt = [parts[g] + parts[g + 1] for g in range(0, len(parts) - 1, 2)]
                parts = nxt + parts[len(parts) - len(parts) % 2:]
            cnt = n_blocks * KB + jnp.sum(parts[0], axis=0, keepdims=True)
            ok = cnt >= need
            return jnp.where(ok, cand, t), jnp.where(ok, above, cnt)

        init = (jnp.full((1, QB), I16_MIN, I32), jnp.zeros((1, QB), I32))
        return lax.fori_loop(0, 16, bit_body, init)

    st_s[n_kb] = jnp.full((KB, QB), INT_MIN, I32)
    hi_s[n_kb] = jnp.full((KB, QB), I16_MIN, I32)
    n_pairs = (n_kb + 1) // 2
    few_keys = N_META + QB * i <= topk

    @pl.when(few_keys)
    def _():
        tau_s[...] = jnp.full((1, QB), INT_MIN, I32)

    for pairs in range(1, (n_kb_all + 1) // 2 + 1):
        @pl.when(jnp.logical_and(n_pairs == pairs, jnp.logical_not(few_keys)))
        def _(n_blocks=2 * pairs):
            tau_hi, n_above = search16(hi_s, jnp.full((1, QB), topk, I32), n_blocks)
            for kb in range(n_blocks):
                key = st_s[kb]
                low = (key & 0xFFFF) + I16_MIN
                lo_s[kb] = jnp.where((key >> 16) == tau_hi, low, I16_MIN)
            tau_lo, _ = search16(lo_s, topk - n_above, n_blocks)
            tau_s[...] = lax.shift_left(tau_hi, jnp.int32(16)) | (tau_lo - I16_MIN)

    tau = jnp.maximum(tau_s[...], INT_MIN + 1)

    m_s[...] = jnp.full(m_s.shape, NEG_INF, F32)
    l_s[...] = jnp.zeros(l_s.shape, F32)
    acc_s[...] = jnp.zeros(acc_s.shape, F32)

    def logits_stage(kb, lg_ref):
        lg_ref[...] = _dot(cn_s[kb], qlat_s[...], _NT)

    def softmax_stage(kb, lg_ref):
        neg = jnp.where(st_s[kb] >= tau, 0.0, NEG_INF)
        v0 = pl.multiple_of(jnp.maximum(KB * kb - QB * i + BIAS_ORIGIN, 0), LANES)
        ps, alphas = [], []
        for h in range(H_B):
            lh = lg_ref[:, h * QB:(h + 1) * QB] + bias_ref[h, pl.ds(v0, KB), :] + neg
            m_old = m_s[h:h + 1, :]
            m_new = jnp.maximum(m_old, jnp.max(lh, axis=0, keepdims=True))
            p = jnp.exp2(lh - m_new)
            alpha = jnp.exp2(m_old - m_new)
            l_s[h:h + 1, :] = alpha * l_s[h:h + 1, :] + jnp.sum(p, axis=0, keepdims=True)
            m_s[h:h + 1, :] = m_new
            ps.append(p.astype(BF16))
            alphas.append(alpha)
        pv = _dot(cnt_s[kb], jnp.concatenate(ps, axis=1))
        acc_s[...] = acc_s[...] * jnp.concatenate(alphas, axis=1) + pv

    logits_stage(0, lg_a)

    def att_pair(j, carry):
        kb = 2 * j
        logits_stage(kb + 1, lg_b)
        softmax_stage(kb, lg_a)
        logits_stage(jnp.minimum(kb + 2, n_kb - 1), lg_a)
        softmax_stage(kb + 1, lg_b)
        return carry

    lax.fori_loop(0, n_kb // 2, att_pair, 0)

    @pl.when(n_kb % 2 == 1)
    def _():
        softmax_stage(n_kb - 1, lg_a)

    for h in range(H_B):
        cols = slice(h * QB, (h + 1) * QB)
        ot = (acc_s[:, cols] * (1.0 / l_s[h:h + 1, :])).astype(BF16)
        bt = _dot(wuvt_ref[h], ot)
        gate = gb_ref[:, cols].astype(F32)
        o_ref[:, cols] = (bt.T * _silu(gate)).astype(o_ref.dtype)


def dsa_attention(proj, kv_gain, wuk_t, wuv_t, bias_tab, *, topk):
    bsz, tp, _ = proj.shape
    n_qb = tp // QB
    n_kb_all = -(-tp // KB)
    return pl.pallas_call(
        functools.partial(_attn_kernel, tp=tp, topk=topk),
        grid=(bsz, n_qb),
        in_specs=[
            pl.BlockSpec((None, tp, D_C), lambda b, i: (b, 0, E_COL_C // D_C)),
            pl.BlockSpec((None, tp, LANES), lambda b, i: (b, 0, E_COL_KK // LANES)),
            pl.BlockSpec((None, QB, LANES), lambda b, i: (b, i, E_COL_WI // LANES)),
            pl.BlockSpec((None, QB, H_B * DH_B), lambda b, i: (b, i, E_COL_Q // (H_B * DH_B))),
            pl.BlockSpec((None, QB, H_IDX * D_IDX), lambda b, i: (b, i, E_COL_QI // (H_IDX * D_IDX))),
            pl.BlockSpec((None, QB, H_B * DH_B), lambda b, i: (b, i, E_COL_GB // (H_B * DH_B))),
            pl.BlockSpec((1, D_C), lambda b, i: (0, 0)),
            pl.BlockSpec((H_B, DH_B, D_C), lambda b, i: (0, 0, 0)),
            pl.BlockSpec((H_B, DH_B, D_C), lambda b, i: (0, 0, 0)),
            pl.BlockSpec((H_B, BIAS_ROWS, LANES), lambda b, i: (0, 0, 0)),
        ],
        out_specs=pl.BlockSpec((None, QB, H_B * DH_B), lambda b, i: (b, i, 0)),
        out_shape=jax.ShapeDtypeStruct((bsz, tp, H_B * DH_B), BF16),
        scratch_shapes=[
            pltpu.VMEM((n_kb_all, KB, D_C), BF16),
            pltpu.VMEM((n_kb_all, D_C, KB), BF16),
            pltpu.VMEM((2, n_kb_all, KB, LANES), BF16),
            pltpu.VMEM((n_kb_all + 1, KB, QB), I32),
            pltpu.VMEM((n_kb_all + 1, KB, QB), I32),
            pltpu.VMEM((n_kb_all + 1, KB, QB), I32),
            pltpu.VMEM((H_B * QB, D_C), BF16),
            pltpu.VMEM((KB, H_B * QB), F32),
            pltpu.VMEM((KB, H_B * QB), F32),
            pltpu.VMEM((D_C, H_B * QB), F32),
            pltpu.VMEM((H_B, QB), F32),
            pltpu.VMEM((H_B, QB), F32),
            pltpu.VMEM((1, QB), I32),
        ],
        compiler_params=_cparams(("parallel", "arbitrary"), 40),
        name="dsa_attention",
    )(proj, proj, proj, proj, proj, proj, kv_gain.reshape(1, D_C).astype(F32), wuk_t, wuv_t, bias_tab)


def _hgrn_constants():
    n = HC
    t = np.arange(n)[:, None]
    j = np.arange(n)[None, :]
    mats = [(j <= t), (j > t)]
    for lvl in range(N_LEVELS):
        upper = ((t >> lvl) & 1) == 1
        m_up = (t >> lvl) << lvl
        m_lo = ((t >> lvl) + 1) << lvl
        mats.append(np.where(upper, (j >= m_up) & (j <= t), (j > t) & (j < m_lo)))
    m = np.concatenate(mats, axis=0).astype(np.float32)
    mcat = np.concatenate([m, m], axis=1)
    x = t ^ j
    lev = np.where(j < t, np.floor(np.log2(np.maximum(x, 1))).astype(np.int32), np.where(j == t, -1, -2))
    masks = np.stack([(lev == lvl) for lvl in range(-1, N_LEVELS)]).astype(np.float32)
    return mcat, masks


HGRN_HEADS_PER_STEP = 4


def _role_select(qf, k, lvl, up_small):
    half = 1 << lvl
    if half >= SUBLANES:
        pieces = []
        for blk in range(HC // (2 * half)):
            lo = 2 * half * blk
            pieces += [k[lo:lo + half], qf[lo + half:lo + 2 * half]]
        return jnp.concatenate(pieces, axis=0)
    g, w = HC // SUBLANES, qf.shape[-1]
    sel = jnp.where(up_small[lvl], qf.reshape(g, SUBLANES, w), k.reshape(g, SUBLANES, w))
    return sel.reshape(HC, w)


_SLOT_QE, _SLOT_KT, _SLOT_QB, _SLOT_KB, _SLOT_Y0 = 0, 1, 2, 3, 4
_N_SLOTS = _SLOT_Y0 + N_LEVELS


def _hgrn_kernel(q_ref, fz_ref, i_ref, g_ref, lbp_ref, ng_ref, mcat_ref, mask_ref, o_ref,
                 s_ref, stage_a, stage_b, dec_a, dec_b, *, n_steps, nh):
    wide = nh * DK_C
    row8 = lax.broadcasted_iota(I32, (1, SUBLANES, wide), 1)
    up_small = [((row8 >> lvl) & 1) == 1 for lvl in range(3)]
    s_ref[...] = jnp.zeros(s_ref.shape, F32)

    def gate_stage(c):
        rows = pl.ds(pl.multiple_of(c * HC, HC), HC)
        lb = lbp_ref[0:1, :]
        one_m_lb = lbp_ref[1:2, :]
        qf = _silu(q_ref[rows, :].astype(F32))
        z = fz_ref[rows, :].astype(F32)
        w = one_m_lb * (0.5 + 0.5 * jnp.tanh(0.5 * z))
        k = one_m_lb - w
        lf2 = jnp.log(lb + w) * LOG2_E
        hi = lf2.astype(BF16)
        lo = (lf2 - hi.astype(F32)).astype(BF16)
        return qf, k, _dot(mcat_ref[...], jnp.concatenate([hi, lo], axis=0))

    def operand_stage(gates, stage, dec_ref):
        qf, k, gsum = gates
        ex = jnp.exp2(gsum)
        e_b = ex[0:HC]
        stage[_SLOT_QE] = (qf * e_b).astype(BF16)
        stage[_SLOT_KT] = (k * ex[HC:2 * HC]).astype(BF16)
        stage[_SLOT_QB] = qf.astype(BF16)
        stage[_SLOT_KB] = k.astype(BF16)
        for lvl in range(N_LEVELS):
            y = _role_select(qf, k, lvl, up_small) * ex[(2 + lvl) * HC:(3 + lvl) * HC]
            stage[_SLOT_Y0 + lvl] = y.astype(BF16)
        dec_ref[...] = e_b[HC - 1:HC, :]

    def level_stage(c, stage):
        lns = [slice(hh * DK_C, (hh + 1) * DK_C) for hh in range(nh)]
        sts = [s_ref[hh] for hh in range(nh)]
        outs = [_dot(stage[_SLOT_QE, :, ln], st.astype(BF16), _NT) for ln, st in zip(lns, sts)]
        accs = [_dot(stage[_SLOT_QB, :, ln], stage[_SLOT_KB, :, ln], _NT) * mask_ref[0] for ln in lns]
        for lvl in range(N_LEVELS):
            for hh, ln in enumerate(lns):
                y = stage[_SLOT_Y0 + lvl, :, ln]
                accs[hh] = accs[hh] + _dot(y, y, _NT) * mask_ref[lvl + 1]
        return sts, outs, accs

    def output_stage(c, partial, stage, dec_ref):
        sts, outs, accs = partial
        rows = pl.ds(pl.multiple_of(c * HC, HC), HC)
        lns = [slice(hh * DK_C, (hh + 1) * DK_C) for hh in range(nh)]
        vhs = [i_ref[rows, ln] for ln in lns]
        for hh, ln in enumerate(lns):
            vt = vhs[hh].astype(F32).T.astype(BF16)
            s_ref[hh] = sts[hh] * dec_ref[:, ln] + _dot(vt, stage[_SLOT_KT, :, ln])
        normed = []
        for hh in range(nh):
            o = outs[hh] + _dot(accs[hh].astype(BF16), vhs[hh])
            normed.append(o * lax.rsqrt(jnp.mean(o * o, axis=-1, keepdims=True) + RMS_EPS))
        on = jnp.concatenate(normed, axis=1) * ng_ref[...]
        gx = g_ref[rows, :].astype(F32)
        o_ref[rows, :] = (on * _silu(gx)).astype(o_ref.dtype)

    last = n_steps - 1
    operand_stage(gate_stage(0), stage_a, dec_a)

    def pair(j, carry):
        c = 2 * j
        partial = level_stage(c, stage_a)
        gates = gate_stage(c + 1)
        output_stage(c, partial, stage_a, dec_a)
        operand_stage(gates, stage_b, dec_b)
        partial = level_stage(c + 1, stage_b)
        gates = gate_stage(jnp.minimum(c + 2, last))
        output_stage(c + 1, partial, stage_b, dec_b)
        operand_stage(gates, stage_a, dec_a)
        return carry

    lax.fori_loop(0, n_steps // 2, pair, 0)
    if n_steps % 2:
        output_stage(last, level_stage(last, stage_a), stage_a, dec_a)


def hgrn2(proj, lb_params, norm_g):
    bsz, tp, _ = proj.shape
    mcat, masks = _hgrn_constants()
    nh = HGRN_HEADS_PER_STEP
    wide = nh * DK_C
    n_groups = H_C // nh
    col = lambda off: pl.BlockSpec((None, tp, wide), lambda b, h, off=off: (b, 0, off + h))
    return pl.pallas_call(
        functools.partial(_hgrn_kernel, n_steps=tp // HC, nh=nh),
        grid=(bsz, n_groups),
        in_specs=[
            col(0), col(n_groups), col(2 * n_groups), col(3 * n_groups),
            pl.BlockSpec((2, wide), lambda b, h: (0, h)),
            pl.BlockSpec((1, wide), lambda b, h: (0, h)),
            pl.BlockSpec(mcat.shape, lambda b, h: (0, 0)),
            pl.BlockSpec(masks.shape, lambda b, h: (0, 0, 0)),
        ],
        out_specs=pl.BlockSpec((None, tp, wide), lambda b, h: (b, 0, h)),
        out_shape=jax.ShapeDtypeStruct((bsz, tp, H_C * DK_C), BF16),
        scratch_shapes=[
            pltpu.VMEM((nh, DK_C, DK_C), F32),
            pltpu.VMEM((_N_SLOTS, HC, wide), BF16),
            pltpu.VMEM((_N_SLOTS, HC, wide), BF16),
            pltpu.VMEM((1, wide), F32),
            pltpu.VMEM((1, wide), F32),
        ],
        compiler_params=_cparams(("parallel", "parallel"), 40),
        name="hgrn2",
    )(proj, proj, proj, proj, lb_params, norm_g.reshape(1, -1).astype(F32),
      jnp.asarray(mcat, BF16), jnp.asarray(masks, F32))


def _t5_bucket(rel):
    nb = N_BUCKETS // 2
    ret = jnp.where(rel > 0, nb, 0)
    n = jnp.abs(rel)
    max_exact = nb // 2
    nf = jnp.maximum(n, 1).astype(F32)
    large = max_exact + (jnp.log(nf / max_exact) / math.log(MAX_DISTANCE / max_exact)
                         * (nb - max_exact)).astype(I32)
    large = jnp.minimum(large, nb - 1)
    return ret + jnp.where(n < max_exact, n, large)


def _bias_table(rel_bias_table):
    v = jnp.arange(BIAS_ROWS, dtype=I32)[:, None]
    q = jnp.arange(LANES, dtype=I32)[None, :]
    onehot = jax.nn.one_hot(_t5_bucket(v - BIAS_ORIGIN - q), N_BUCKETS, dtype=F32)
    tab = jnp.einsum("vqb,bh->hvq", onehot, rel_bias_table.astype(F32), precision=lax.Precision.HIGHEST)
    return tab * LOG2_E


def _even_in_weight(w):
    glu_v, glu_g, gate_a, q, c, gate_b, qi, ki, wi = jnp.split(
        w, np.cumsum([D_A, D_A, D_A, H_B * DH_B, D_C, H_B * DH_B, H_IDX * D_IDX, D_IDX])[:8].tolist(), axis=1)
    pad = jnp.zeros((w.shape[0], E_NPAD - E_COL_WI - H_IDX), w.dtype)
    cols = [glu_v, glu_g, gate_a, q, gate_b, qi * (D_IDX ** -0.5), c, ki, ki, wi * (H_IDX ** -0.5), pad]
    return jnp.concatenate(cols, axis=1).astype(BF16)


def kernel(x, meta_tokens, norm_gain, final_norm_gain, rel_bias_table, w_in_even, conv_w, conv_b,
           conv_ln_gain, conv_ln_bias, kv_norm_gain, w_uk, w_uv, w_out_even, w_in_odd, lb_logits,
           rec_norm_gain, w_out_odd):
    bsz, seq, d = x.shape
    assert d == D_MODEL and seq % LANES == 0
    depth = norm_gain.shape[0]
    tp = FRONT_PAD + N_META + seq
    topk = min(TOPK_MAX, seq // 4)
    tt_mm = tp // 2 if (tp // 2) % 16 == 0 else tp
    tt_out = tp // 4 if (tp // 4) % 16 == 0 else tp
    tt_last = seq // 4
    tt_conv = tp // 8 if (tp // 8) % CONV_RB == 0 else tp // 2 if (tp // 2) % CONV_RB == 0 else tp
    h = jnp.concatenate([
        jnp.zeros((bsz, FRONT_PAD, d), F32),
        jnp.broadcast_to(meta_tokens.astype(F32), (bsz, N_META, d)),
        x.astype(F32)], axis=1)

    lb_soft = jax.nn.softmax(lb_logits.astype(F32), axis=0)
    lower_bounds = jnp.cumsum(lb_soft, axis=0) - lb_soft[0]
    bias_tab = _bias_table(rel_bias_table)

    hn = None
    for layer in range(depth):
        last = layer == depth - 1
        if layer % 2 == 0:
            e = layer // 2
            w_in, tn = _even_in_weight(w_in_even[e]), E_NPAD // 2
        else:
            o = layer // 2
            w_in, tn = w_in_odd[o].astype(BF16), 2048
        if hn is None:
            proj = norm_matmul(h, norm_gain[layer], w_in, tt=tt_mm, tn=tn // 2)
        else:
            proj = matmul(hn, w_in, tt=tt_mm, tn=tn)
        if layer % 2 == 0:
            a_out = conv_module(proj, conv_w[e], conv_b[e], conv_ln_gain[e], conv_ln_bias[e], tt=tt_conv)
            wuk_t = jnp.transpose(w_uk[e], (0, 2, 1)).astype(BF16)
            wuv_t = jnp.transpose(w_uv[e], (0, 2, 1)).astype(BF16)
            b_out = dsa_attention(proj, kv_norm_gain[e], wuk_t, wuv_t, bias_tab, topk=topk)
            mixed, w_out = [a_out, b_out], w_out_even[e].astype(BF16)
        else:
            lb = lower_bounds[layer]
            mixed = [hgrn2(proj, jnp.stack([lb, 1.0 - lb], axis=0), rec_norm_gain[o])]
            w_out = w_out_odd[o].astype(BF16)
        if last:
            return out_proj(mixed, w_out, h, final_norm_gain, tt=tt_last, final=True)
        h, hn = out_proj(mixed, w_out, h, norm_gain[layer + 1], tt=tt_out, final=False)
```

```python
import functools
import math

import numpy as np
import jax
import jax.numpy as jnp
from jax import lax
from jax.experimental import pallas as pl
from jax.experimental.pallas import tpu as pltpu

F32 = jnp.float32
BF16 = jnp.bfloat16
I32 = jnp.int32
I16 = jnp.int16

D_MODEL = 2048
CHUNK = 64
N_META = 16
D_A = 1024
CONV_WIDTH = 31
H_B = 8
DH_B = 128
D_C = 256
H_IDX = 16
D_IDX = 64
TOPK_MAX = 256
N_BUCKETS = 32
MAX_DISTANCE = 128
H_C = 16
DK_C = 128
RMS_EPS = 1e-6
LN_EPS = 1e-5
NEG_INF = -1e30
INT_MIN = -(2 ** 31)
LOG2_E = 1.4426950408889634

LANES = 128
SUBLANES = 8
PACK16 = 16
I16_MIN = -(2 ** 15)
MXU_N = 256

FRONT_PAD = LANES - N_META
QB = LANES
KB = MXU_N
HC = 128
N_LEVELS = 7

E_NPAD = 6656
E_COL_Q, E_COL_GB, E_COL_QI = 3072, 4096, 5120
E_COL_C, E_COL_KK, E_COL_WI = 6144, 6400, 6528

MIB = 1024 * 1024


def _cparams(sem, vmem_mib, flags=None):
    return pltpu.CompilerParams(dimension_semantics=sem, vmem_limit_bytes=vmem_mib * MIB, flags=flags)


def _sigmoid(x):
    return 1.0 / (1.0 + jnp.exp(-x))


def _silu(x):
    h = 0.5 * x
    return h + h * jnp.tanh(h)


_NN = (((1,), (0,)), ((), ()))
_NT = (((1,), (1,)), ((), ()))


def _dot(a, b, dims=_NN):
    return lax.dot_general(a, b, dims, preferred_element_type=F32)


def _rms_normed(x, gain):
    ms = jnp.mean(x * x, axis=-1, keepdims=True)
    return x * lax.rsqrt(ms + RMS_EPS) * gain


def _embed_kernel(x_ref, meta_ref, g_ref, h_ref, hn_ref):
    @pl.when(pl.program_id(1) == 0)
    def _():
        d = h_ref.shape[-1]
        meta = meta_ref[...]
        h_ref[0:FRONT_PAD, :] = jnp.zeros((FRONT_PAD, d), F32)
        h_ref[FRONT_PAD:, :] = meta
        hn_ref[0:FRONT_PAD, :] = jnp.zeros((FRONT_PAD, d), BF16)
        hn_ref[FRONT_PAD:, :] = _rms_normed(meta, g_ref[...]).astype(BF16)

    @pl.when(pl.program_id(1) > 0)
    def _():
        x = x_ref[...]
        h_ref[...] = x
        hn_ref[...] = _rms_normed(x, g_ref[...]).astype(BF16)


def embed(x, meta_tokens, gain):
    bsz, seq, d = x.shape
    blk = FRONT_PAD + N_META
    tp = blk + seq
    spec = pl.BlockSpec((None, blk, d), lambda b, i: (b, i, 0))
    return pl.pallas_call(
        _embed_kernel,
        grid=(bsz, tp // blk),
        in_specs=[
            pl.BlockSpec((None, blk, d), lambda b, i: (b, jnp.maximum(i - 1, 0), 0)),
            pl.BlockSpec((N_META, d), lambda b, i: (0, 0)),
            pl.BlockSpec((1, d), lambda b, i: (0, 0)),
        ],
        out_specs=[spec, spec],
        out_shape=[jax.ShapeDtypeStruct((bsz, tp, d), F32), jax.ShapeDtypeStruct((bsz, tp, d), BF16)],
        compiler_params=_cparams(("parallel", "parallel"), 16),
        name="embed",
    )(x.astype(F32), meta_tokens.astype(F32), gain.reshape(1, d).astype(F32))


def _matmul_kernel(x_ref, w_ref, o_ref):
    o_ref[...] = _dot(x_ref[...], w_ref[...]).astype(o_ref.dtype)


def matmul(hn, w, *, tt, tn):
    bsz, tp, d = hn.shape
    n = w.shape[1]
    return pl.pallas_call(
        _matmul_kernel,
        grid=(bsz, tp // tt, n // tn),
        in_specs=[
            pl.BlockSpec((None, tt, d), lambda b, i, j: (b, i, 0)),
            pl.BlockSpec((d, tn), lambda b, i, j: (0, j)),
        ],
        out_specs=pl.BlockSpec((None, tt, tn), lambda b, i, j: (b, i, j)),
        out_shape=jax.ShapeDtypeStruct((bsz, tp, n), BF16),
        compiler_params=_cparams(("parallel", "parallel", "parallel"), 56),
        name="in_proj",
    )(hn, w)


def _out_proj_kernel(*refs, n_in, tt, final):
    xs = refs[:n_in]
    w_ref, h_ref, g_ref = refs[n_in], refs[n_in + 1], refs[n_in + 2]
    tile = lambda ref: ref[0] if len(ref.shape) == 3 else ref[...]
    y = None
    off = 0
    for x_ref in xs:
        kdim = x_ref.shape[-1]
        part = _dot(tile(x_ref), w_ref[off:off + kdim, :])
        y = part if y is None else y + part
        off += kdim
    h_new = tile(h_ref) + y
    if final:
        refs[-1][...] = _rms_normed(h_new, g_ref[...])
    else:
        row = pl.program_id(1) * tt + lax.broadcasted_iota(I32, (tt, 1), 0)
        h_new = jnp.where(row >= FRONT_PAD, h_new, 0.0)
        refs[-2][...] = h_new
        refs[-1][...] = _rms_normed(h_new, g_ref[...]).astype(BF16)


def out_proj(xs, w, h, gain, *, tt, final):
    bsz, tp, d = h.shape
    if final:
        first = FRONT_PAD + N_META
        rows_out = tp - first
        row_spec = lambda k: pl.BlockSpec((pl.Element(1), pl.Element(tt), pl.Element(k)),
                                          lambda b, i: (b, pl.multiple_of(first + i * tt, LANES), 0))
    else:
        rows_out = tp
        row_spec = lambda k: pl.BlockSpec((None, tt, k), lambda b, i: (b, i, 0))
    in_specs = [row_spec(x.shape[-1]) for x in xs]
    in_specs += [pl.BlockSpec(w.shape, lambda b, i: (0, 0)), row_spec(d), pl.BlockSpec((1, d), lambda b, i: (0, 0))]
    out_spec = pl.BlockSpec((None, tt, d), lambda b, i: (b, i, 0))
    out_f32 = jax.ShapeDtypeStruct((bsz, rows_out, d), F32)
    return pl.pallas_call(
        functools.partial(_out_proj_kernel, n_in=len(xs), tt=tt, final=final),
        grid=(bsz, rows_out // tt),
        in_specs=in_specs,
        out_specs=out_spec if final else [out_spec, out_spec],
        out_shape=out_f32 if final else [out_f32, jax.ShapeDtypeStruct((bsz, rows_out, d), BF16)],
        compiler_params=_cparams(("parallel", "parallel"), 56),
        name="out_proj_final" if final else "out_proj",
    )(*xs, w, h, gain.reshape(1, d).astype(F32))


CONV_HALO = 32
CONV_RB = 16
CONV_CHAINS = 4


def _conv_kernel(v_ref, g_ref, ga_ref, w_ref, cb_ref, lg_ref, lb_ref, o_ref, us_ref, cv_ref, *, tt):
    j = pl.program_id(1)
    u_new = v_ref[...].astype(F32) * _sigmoid(g_ref[...].astype(F32))
    span = tt + CONV_HALO - SUBLANES
    n_rb = tt // CONV_RB
    for cb in range(D_A // LANES):
        lanes = slice(cb * LANES, (cb + 1) * LANES)

        @pl.when(j == 0)
        def _(cb=cb):
            us_ref[0, cb, 0:CONV_HALO, :] = jnp.zeros((CONV_HALO, LANES), F32)

        @pl.when(j > 0)
        def _(cb=cb):
            us_ref[0, cb, 0:CONV_HALO, :] = us_ref[0, cb, tt:tt + CONV_HALO, :]

        us_ref[0, cb, CONV_HALO:CONV_HALO + tt, :] = u_new[:, lanes]
        for r in range(1, SUBLANES):
            us_ref[r, cb, 0:span, :] = us_ref[0, cb, r:r + span, :]

        wb = [jnp.broadcast_to(w_ref[k:k + 1, lanes], (CONV_RB, LANES)) for k in range(CONV_WIDTH)]
        bias = jnp.broadcast_to(cb_ref[0:1, lanes], (CONV_RB, LANES))

        def rb_body(rb, carry, cb=cb, lanes=lanes, wb=wb, bias=bias):
            base = pl.multiple_of(rb * CONV_RB, CONV_RB)
            parts = [bias] + [None] * (CONV_CHAINS - 1)
            for k in range(CONV_WIDTH):
                a, r = divmod(k + CONV_HALO - (CONV_WIDTH - 1), SUBLANES)
                term = wb[k] * us_ref[r, cb, pl.ds(base + SUBLANES * a, CONV_RB), :]
                c = k % CONV_CHAINS
                parts[c] = term if parts[c] is None else parts[c] + term
            cv_ref[pl.ds(base, CONV_RB), lanes] = (parts[0] + parts[1]) + (parts[2] + parts[3])
            return carry

        lax.fori_loop(0, n_rb, rb_body, 0)

    x = cv_ref[...]
    mu = jnp.mean(x, axis=-1, keepdims=True)
    xc = x - mu
    var = jnp.mean(xc * xc, axis=-1, keepdims=True)
    y = xc * lax.rsqrt(var + LN_EPS) * lg_ref[...] + lb_ref[...]
    ga = ga_ref[...].astype(F32)
    o_ref[...] = (_silu(y) * _silu(ga)).astype(o_ref.dtype)


def conv_module(proj, conv_w, conv_b, ln_g, ln_b, *, tt):
    bsz, tp, _ = proj.shape
    row = lambda a: a.reshape(1, D_A).astype(F32)
    return pl.pallas_call(
        functools.partial(_conv_kernel, tt=tt),
        grid=(bsz, tp // tt),
        in_specs=[
            pl.BlockSpec((None, tt, D_A), lambda b, j: (b, j, 0)),
            pl.BlockSpec((None, tt, D_A), lambda b, j: (b, j, 1)),
            pl.BlockSpec((None, tt, D_A), lambda b, j: (b, j, 2)),
            pl.BlockSpec((CONV_WIDTH, D_A), lambda b, j: (0, 0)),
            pl.BlockSpec((1, D_A), lambda b, j: (0, 0)),
            pl.BlockSpec((1, D_A), lambda b, j: (0, 0)),
            pl.BlockSpec((1, D_A), lambda b, j: (0, 0)),
        ],
        out_specs=pl.BlockSpec((None, tt, D_A), lambda b, j: (b, j, 0)),
        out_shape=jax.ShapeDtypeStruct((bsz, tp, D_A), BF16),
        scratch_shapes=[
            pltpu.VMEM((SUBLANES, D_A // LANES, tt + CONV_HALO, LANES), F32),
            pltpu.VMEM((tt, D_A), F32),
        ],
        compiler_params=_cparams(("parallel", "arbitrary"), 40),
        name="conv_module",
    )(proj, proj, proj, conv_w.astype(F32), row(conv_b), row(ln_g), row(ln_b))


BIAS_ROWS = 640
BIAS_ORIGIN = 384


def _attn_kernel(c_ref, kk_ref, wi_ref, q_ref, qi_ref, gb_ref, kvg_ref, wuk_ref, wuvt_ref, bias_ref,
                 o_ref, cn_s, cnt_s, kk_s, st_s, hi_s, lo_s, qlat_s, lg_a, lg_b, acc_s, m_s, l_s,
                 tau_s, *, tp, topk):
    i = pl.program_id(1)
    n_kb = (i + 2) // 2
    n_kb_all = cn_s.shape[0]

    @pl.when(i == 0)
    def _prepare_keys():
        gain = kvg_ref[...]
        for kb in range(n_kb_all):
            lo = kb * KB
            nrow = min(KB, tp - lo)
            c = c_ref[lo:lo + nrow, :].astype(F32)
            cn = c * lax.rsqrt(jnp.mean(c * c, axis=-1, keepdims=True) + RMS_EPS) * gain
            cn_s[kb, 0:nrow, :] = cn.astype(BF16)
            kk = kk_ref[lo:lo + nrow, :]
            lane = lax.broadcasted_iota(I32, kk.shape, 1)
            kk_s[0, kb, 0:nrow, :] = jnp.where(lane < D_IDX, kk, jnp.zeros_like(kk))
            kk_s[1, kb, 0:nrow, :] = jnp.where(lane >= D_IDX, kk, jnp.zeros_like(kk))
            for rb in range(nrow // LANES):
                for cb in range(D_C // LANES):
                    tile = cn[rb * LANES:(rb + 1) * LANES, cb * LANES:(cb + 1) * LANES]
                    cnt_s[kb, cb * LANES:(cb + 1) * LANES, rb * LANES:(rb + 1) * LANES] = tile.T.astype(BF16)
            if nrow < KB:
                cn_s[kb, nrow:KB, :] = jnp.zeros((KB - nrow, D_C), BF16)
                kk_s[:, kb, nrow:KB, :] = jnp.zeros((2, KB - nrow, LANES), BF16)
                cnt_s[kb, :, nrow:KB] = jnp.zeros((D_C, KB - nrow), BF16)

    for h in range(H_B):
        qh = q_ref[:, h * DH_B:(h + 1) * DH_B]
        ql = _dot(qh, wuk_ref[h]) * (DH_B ** -0.5 * LOG2_E)
        qlat_s[h * QB:(h + 1) * QB, :] = ql.astype(BF16)

    wit = wi_ref[...].astype(F32).T

    def score_block(kb):
        k_even = kk_s[0, kb]
        k_odd = kk_s[1, kb]
        acc = jnp.zeros((KB, QB), F32)
        for p in range(H_IDX // 2):
            qp = qi_ref[:, p * LANES:(p + 1) * LANES]
            acc = acc + jnp.maximum(_dot(k_even, qp, _NT), 0.0) * wit[2 * p:2 * p + 1, :]
            acc = acc + jnp.maximum(_dot(k_odd, qp, _NT), 0.0) * wit[2 * p + 1:2 * p + 2, :]
        bits = lax.bitcast_convert_type(acc, I32)
        key = bits ^ ((bits >> 31) & 0x7FFFFFFF)
        s_idx = kb * KB + lax.broadcasted_iota(I32, (KB, QB), 0)
        t_idx = i * QB + lax.broadcasted_iota(I32, (KB, QB), 1)
        adm = (s_idx >= FRONT_PAD) & (jnp.maximum(s_idx >> 6, 1) <= jnp.maximum(t_idx >> 6, 1))
        key = jnp.where(adm, key, INT_MIN)
        st_s[kb] = key
        hi_s[kb] = key >> 16

    def score_pair(j, carry):
        score_block(2 * j)
        score_block(jnp.minimum(2 * j + 1, n_kb_all - 1))
        return carry

    lax.fori_loop(0, (n_kb + 1) // 2, score_pair, 0)

    def search16(ref, need, n_blocks):
        def bit_body(it, carry):
            t, above = carry
            cand = t + lax.shift_left(jnp.int32(1), jnp.int32(15) - it)
            cand8 = jnp.broadcast_to(cand, (SUBLANES, QB))[None]
            parts = []
            for kb in range(n_blocks):
                below = (ref[kb].reshape(KB // SUBLANES, SUBLANES, QB) - cand8) >> 31
                parts += [below[g] for g in range(KB // SUBLANES)]
            while len(parts) > 1:
                nxt = [parts[g] + parts[g + 1] for g in range(0, len(parts) - 1, 2)]
                parts = nxt + parts[len(parts) - len(parts) % 2:]
            cnt = n_blocks * KB + jnp.sum(parts[0], axis=0, keepdims=True)
            ok = cnt >= need
            return jnp.where(ok, cand, t), jnp.where(ok, above, cnt)

        init = (jnp.full((1, QB), I16_MIN, I32), jnp.zeros((1, QB), I32))
        return lax.fori_loop(0, 16, bit_body, init)

    st_s[n_kb] = jnp.full((KB, QB), INT_MIN, I32)
    hi_s[n_kb] = jnp.full((KB, QB), I16_MIN, I32)
    n_pairs = (n_kb + 1) // 2
    few_keys = N_META + QB * i <= topk

    @pl.when(few_keys)
    def _():
        tau_s[...] = jnp.full((1, QB), INT_MIN, I32)

    for pairs in range(1, (n_kb_all + 1) // 2 + 1):
        @pl.when(jnp.logical_and(n_pairs == pairs, jnp.logical_not(few_keys)))
        def _(n_blocks=2 * pairs):
            tau_hi, n_above = search16(hi_s, jnp.full((1, QB), topk, I32), n_blocks)
            for kb in range(n_blocks):
                key = st_s[kb]
                low = (key & 0xFFFF) + I16_MIN
                lo_s[kb] = jnp.where((key >> 16) == tau_hi, low, I16_MIN)
            tau_lo, _ = search16(lo_s, topk - n_above, n_blocks)
            tau_s[...] = lax.shift_left(tau_hi, jnp.int32(16)) | (tau_lo - I16_MIN)

    tau = jnp.maximum(tau_s[...], INT_MIN + 1)

    m_s[...] = jnp.full(m_s.shape, NEG_INF, F32)
    l_s[...] = jnp.zeros(l_s.shape, F32)
    acc_s[...] = jnp.zeros(acc_s.shape, F32)

    def logits_stage(kb, lg_ref):
        lg_ref[...] = _dot(cn_s[kb], qlat_s[...], _NT)

    def softmax_stage(kb, lg_ref):
        neg = jnp.where(st_s[kb] >= tau, 0.0, NEG_INF)
        v0 = pl.multiple_of(jnp.maximum(KB * kb - QB * i + BIAS_ORIGIN, 0), LANES)
        ps, alphas = [], []
        for h in range(H_B):
            lh = lg_ref[:, h * QB:(h + 1) * QB] + bias_ref[h, pl.ds(v0, KB), :] + neg
            m_old = m_s[h:h + 1, :]
            m_new = jnp.maximum(m_old, jnp.max(lh, axis=0, keepdims=True))
            p = jnp.exp2(lh - m_new)
            alpha = jnp.exp2(m_old - m_new)
            l_s[h:h + 1, :] = alpha * l_s[h:h + 1, :] + jnp.sum(p, axis=0, keepdims=True)
            m_s[h:h + 1, :] = m_new
            ps.append(p.astype(BF16))
            alphas.append(alpha)
        pv = _dot(cnt_s[kb], jnp.concatenate(ps, axis=1))
        acc_s[...] = acc_s[...] * jnp.concatenate(alphas, axis=1) + pv

    logits_stage(0, lg_a)

    def att_pair(j, carry):
        kb = 2 * j
        logits_stage(kb + 1, lg_b)
        softmax_stage(kb, lg_a)
        logits_stage(jnp.minimum(kb + 2, n_kb - 1), lg_a)
        softmax_stage(kb + 1, lg_b)
        return carry

    lax.fori_loop(0, n_kb // 2, att_pair, 0)

    @pl.when(n_kb % 2 == 1)
    def _():
        softmax_stage(n_kb - 1, lg_a)

    for h in range(H_B):
        cols = slice(h * QB, (h + 1) * QB)
        ot = (acc_s[:, cols] * (1.0 / l_s[h:h + 1, :])).astype(BF16)
        bt = _dot(wuvt_ref[h], ot)
        gate = gb_ref[:, cols].astype(F32)
        o_ref[:, cols] = (bt.T * _silu(gate)).astype(o_ref.dtype)


def dsa_attention(proj, kv_gain, wuk_t, wuv_t, bias_tab, *, topk):
    bsz, tp, _ = proj.shape
    n_qb = tp // QB
    n_kb_all = -(-tp // KB)
    return pl.pallas_call(
        functools.partial(_attn_kernel, tp=tp, topk=topk),
        grid=(bsz, n_qb),
        in_specs=[
            pl.BlockSpec((None, tp, D_C), lambda b, i: (b, 0, E_COL_C // D_C)),
            pl.BlockSpec((None, tp, LANES), lambda b, i: (b, 0, E_COL_KK // LANES)),
            pl.BlockSpec((None, QB, LANES), lambda b, i: (b, i, E_COL_WI // LANES)),
            pl.BlockSpec((None, QB, H_B * DH_B), lambda b, i: (b, i, E_COL_Q // (H_B * DH_B))),
            pl.BlockSpec((None, QB, H_IDX * D_IDX), lambda b, i: (b, i, E_COL_QI // (H_IDX * D_IDX))),
            pl.BlockSpec((None, QB, H_B * DH_B), lambda b, i: (b, i, E_COL_GB // (H_B * DH_B))),
            pl.BlockSpec((1, D_C), lambda b, i: (0, 0)),
            pl.BlockSpec((H_B, DH_B, D_C), lambda b, i: (0, 0, 0)),
            pl.BlockSpec((H_B, DH_B, D_C), lambda b, i: (0, 0, 0)),
            pl.BlockSpec((H_B, BIAS_ROWS, LANES), lambda b, i: (0, 0, 0)),
        ],
        out_specs=pl.BlockSpec((None, QB, H_B * DH_B), lambda b, i: (b, i, 0)),
        out_shape=jax.ShapeDtypeStruct((bsz, tp, H_B * DH_B), BF16),
        scratch_shapes=[
            pltpu.VMEM((n_kb_all, KB, D_C), BF16),
            pltpu.VMEM((n_kb_all, D_C, KB), BF16),
            pltpu.VMEM((2, n_kb_all, KB, LANES), BF16),
            pltpu.VMEM((n_kb_all + 1, KB, QB), I32),
            pltpu.VMEM((n_kb_all + 1, KB, QB), I32),
            pltpu.VMEM((n_kb_all + 1, KB, QB), I32),
            pltpu.VMEM((H_B * QB, D_C), BF16),
            pltpu.VMEM((KB, H_B * QB), F32),
            pltpu.VMEM((KB, H_B * QB), F32),
            pltpu.VMEM((D_C, H_B * QB), F32),
            pltpu.VMEM((H_B, QB), F32),
            pltpu.VMEM((H_B, QB), F32),
            pltpu.VMEM((1, QB), I32),
        ],
        compiler_params=_cparams(("parallel", "arbitrary"), 40),
        name="dsa_attention",
    )(proj, proj, proj, proj, proj, proj, kv_gain.reshape(1, D_C).astype(F32), wuk_t, wuv_t, bias_tab)


def _hgrn_constants():
    n = HC
    t = np.arange(n)[:, None]
    j = np.arange(n)[None, :]
    mats = [(j <= t), (j > t)]
    for lvl in range(N_LEVELS):
        upper = ((t >> lvl) & 1) == 1
        m_up = (t >> lvl) << lvl
        m_lo = ((t >> lvl) + 1) << lvl
        mats.append(np.where(upper, (j >= m_up) & (j <= t), (j > t) & (j < m_lo)))
    m = np.concatenate(mats, axis=0).astype(np.float32)
    mcat = np.concatenate([m, m], axis=1)
    x = t ^ j
    lev = np.where(j < t, np.floor(np.log2(np.maximum(x, 1))).astype(np.int32), np.where(j == t, -1, -2))
    masks = np.stack([(lev == lvl) for lvl in range(-1, N_LEVELS)]).astype(np.float32)
    return mcat, masks


HGRN_HEADS_PER_STEP = 4


def _role_select(qf, k, lvl, up_small):
    half = 1 << lvl
    if half >= SUBLANES:
        pieces = []
        for blk in range(HC // (2 * half)):
            lo = 2 * half * blk
            pieces += [k[lo:lo + half], qf[lo + half:lo + 2 * half]]
        return jnp.concatenate(pieces, axis=0)
    g, w = HC // SUBLANES, qf.shape[-1]
    sel = jnp.where(up_small[lvl], qf.reshape(g, SUBLANES, w), k.reshape(g, SUBLANES, w))
    return sel.reshape(HC, w)


_SLOT_QE, _SLOT_KT, _SLOT_QB, _SLOT_KB, _SLOT_Y0 = 0, 1, 2, 3, 4
_N_SLOTS = _SLOT_Y0 + N_LEVELS


def _hgrn_kernel(q_ref, fz_ref, i_ref, g_ref, lbp_ref, ng_ref, mcat_ref, mask_ref, o_ref,
                 s_ref, stage_a, stage_b, dec_a, dec_b, *, n_steps, nh):
    wide = nh * DK_C
    row8 = lax.broadcasted_iota(I32, (1, SUBLANES, wide), 1)
    up_small = [((row8 >> lvl) & 1) == 1 for lvl in range(3)]
    s_ref[...] = jnp.zeros(s_ref.shape, F32)

    def gate_stage(c):
        rows = pl.ds(pl.multiple_of(c * HC, HC), HC)
        lb = lbp_ref[0:1, :]
        one_m_lb = lbp_ref[1:2, :]
        qf = _silu(q_ref[rows, :].astype(F32))
        z = fz_ref[rows, :].astype(F32)
        w = one_m_lb * (0.5 + 0.5 * jnp.tanh(0.5 * z))
        k = one_m_lb - w
        lf2 = jnp.log(lb + w) * LOG2_E
        hi = lf2.astype(BF16)
        lo = (lf2 - hi.astype(F32)).astype(BF16)
        return qf, k, _dot(mcat_ref[...], jnp.concatenate([hi, lo], axis=0))

    def operand_stage(gates, stage, dec_ref):
        qf, k, gsum = gates
        ex = jnp.exp2(gsum)
        e_b = ex[0:HC]
        stage[_SLOT_QE] = (qf * e_b).astype(BF16)
        stage[_SLOT_KT] = (k * ex[HC:2 * HC]).astype(BF16)
        stage[_SLOT_QB] = qf.astype(BF16)
        stage[_SLOT_KB] = k.astype(BF16)
        for lvl in range(N_LEVELS):
            y = _role_select(qf, k, lvl, up_small) * ex[(2 + lvl) * HC:(3 + lvl) * HC]
            stage[_SLOT_Y0 + lvl] = y.astype(BF16)
        dec_ref[...] = e_b[HC - 1:HC, :]

    def level_stage(c, stage):
        lns = [slice(hh * DK_C, (hh + 1) * DK_C) for hh in range(nh)]
        sts = [s_ref[hh] for hh in range(nh)]
        outs = [_dot(stage[_SLOT_QE, :, ln], st.astype(BF16), _NT) for ln, st in zip(lns, sts)]
        accs = [_dot(stage[_SLOT_QB, :, ln], stage[_SLOT_KB, :, ln], _NT) * mask_ref[0] for ln in lns]
        for lvl in range(N_LEVELS):
            for hh, ln in enumerate(lns):
                y = stage[_SLOT_Y0 + lvl, :, ln]
                accs[hh] = accs[hh] + _dot(y, y, _NT) * mask_ref[lvl + 1]
        return sts, outs, accs

    def output_stage(c, partial, stage, dec_ref):
        sts, outs, accs = partial
        rows = pl.ds(pl.multiple_of(c * HC, HC), HC)
        lns = [slice(hh * DK_C, (hh + 1) * DK_C) for hh in range(nh)]
        vhs = [i_ref[rows, ln] for ln in lns]
        for hh, ln in enumerate(lns):
            vt = vhs[hh].astype(F32).T.astype(BF16)
            s_ref[hh] = sts[hh] * dec_ref[:, ln] + _dot(vt, stage[_SLOT_KT, :, ln])
        normed = []
        for hh in range(nh):
            o = outs[hh] + _dot(accs[hh].astype(BF16), vhs[hh])
            normed.append(o * lax.rsqrt(jnp.mean(o * o, axis=-1, keepdims=True) + RMS_EPS))
        on = jnp.concatenate(normed, axis=1) * ng_ref[...]
        gx = g_ref[rows, :].astype(F32)
        o_ref[rows, :] = (on * _silu(gx)).astype(o_ref.dtype)

    last = n_steps - 1
    operand_stage(gate_stage(0), stage_a, dec_a)

    def pair(j, carry):
        c = 2 * j
        partial = level_stage(c, stage_a)
        gates = gate_stage(c + 1)
        output_stage(c, partial, stage_a, dec_a)
        operand_stage(gates, stage_b, dec_b)
        partial = level_stage(c + 1, stage_b)
        gates = gate_stage(jnp.minimum(c + 2, last))
        output_stage(c + 1, partial, stage_b, dec_b)
        operand_stage(gates, stage_a, dec_a)
        return carry

    lax.fori_loop(0, n_steps // 2, pair, 0)
    if n_steps % 2:
        output_stage(last, level_stage(last, stage_a), stage_a, dec_a)


def hgrn2(proj, lb_params, norm_g):
    bsz, tp, _ = proj.shape
    mcat, masks = _hgrn_constants()
    nh = HGRN_HEADS_PER_STEP
    wide = nh * DK_C
    n_groups = H_C // nh
    col = lambda off: pl.BlockSpec((None, tp, wide), lambda b, h, off=off: (b, 0, off + h))
    return pl.pallas_call(
        functools.partial(_hgrn_kernel, n_steps=tp // HC, nh=nh),
        grid=(bsz, n_groups),
        in_specs=[
            col(0), col(n_groups), col(2 * n_groups), col(3 * n_groups),
            pl.BlockSpec((2, wide), lambda b, h: (0, h)),
            pl.BlockSpec((1, wide), lambda b, h: (0, h)),
            pl.BlockSpec(mcat.shape, lambda b, h: (0, 0)),
            pl.BlockSpec(masks.shape, lambda b, h: (0, 0, 0)),
        ],
        out_specs=pl.BlockSpec((None, tp, wide), lambda b, h: (b, 0, h)),
        out_shape=jax.ShapeDtypeStruct((bsz, tp, H_C * DK_C), BF16),
        scratch_shapes=[
            pltpu.VMEM((nh, DK_C, DK_C), F32),
            pltpu.VMEM((_N_SLOTS, HC, wide), BF16),
            pltpu.VMEM((_N_SLOTS, HC, wide), BF16),
            pltpu.VMEM((1, wide), F32),
            pltpu.VMEM((1, wide), F32),
        ],
        compiler_params=_cparams(("parallel", "parallel"), 40),
        name="hgrn2",
    )(proj, proj, proj, proj, lb_params, norm_g.reshape(1, -1).astype(F32),
      jnp.asarray(mcat, BF16), jnp.asarray(masks, F32))


def _t5_bucket(rel):
    nb = N_BUCKETS // 2
    ret = jnp.where(rel > 0, nb, 0)
    n = jnp.abs(rel)
    max_exact = nb // 2
    nf = jnp.maximum(n, 1).astype(F32)
    large = max_exact + (jnp.log(nf / max_exact) / math.log(MAX_DISTANCE / max_exact)
                         * (nb - max_exact)).astype(I32)
    large = jnp.minimum(large, nb - 1)
    return ret + jnp.where(n < max_exact, n, large)


def _bias_table(rel_bias_table):
    v = jnp.arange(BIAS_ROWS, dtype=I32)[:, None]
    q = jnp.arange(LANES, dtype=I32)[None, :]
    onehot = jax.nn.one_hot(_t5_bucket(v - BIAS_ORIGIN - q), N_BUCKETS, dtype=F32)
    tab = jnp.einsum("vqb,bh->hvq", onehot, rel_bias_table.astype(F32), precision=lax.Precision.HIGHEST)
    return tab * LOG2_E


def _even_in_weight(w):
    glu_v, glu_g, gate_a, q, c, gate_b, qi, ki, wi = jnp.split(
        w, np.cumsum([D_A, D_A, D_A, H_B * DH_B, D_C, H_B * DH_B, H_IDX * D_IDX, D_IDX])[:8].tolist(), axis=1)
    pad = jnp.zeros((w.shape[0], E_NPAD - E_COL_WI - H_IDX), w.dtype)
    cols = [glu_v, glu_g, gate_a, q, gate_b, qi * (D_IDX ** -0.5), c, ki, ki, wi * (H_IDX ** -0.5), pad]
    return jnp.concatenate([col.astype(BF16) for col in cols], axis=1)


def kernel(x, meta_tokens, norm_gain, final_norm_gain, rel_bias_table, w_in_even, conv_w, conv_b,
           conv_ln_gain, conv_ln_bias, kv_norm_gain, w_uk, w_uv, w_out_even, w_in_odd, lb_logits,
           rec_norm_gain, w_out_odd):
    bsz, seq, d = x.shape
    assert d == D_MODEL and seq % LANES == 0
    depth = norm_gain.shape[0]
    tp = FRONT_PAD + N_META + seq
    topk = min(TOPK_MAX, seq // 4)
    tt_mm = tp // 2 if (tp // 2) % 16 == 0 else tp
    tt_out = tp // 4 if (tp // 4) % 16 == 0 else tp
    tt_last = seq // 4
    tt_conv = tp // 8 if (tp // 8) % CONV_RB == 0 else tp // 2 if (tp // 2) % CONV_RB == 0 else tp
    h, hn = embed(x, meta_tokens, norm_gain[0])

    lb_soft = jax.nn.softmax(lb_logits.astype(F32), axis=0)
    lower_bounds = jnp.cumsum(lb_soft, axis=0) - lb_soft[0]
    bias_tab = _bias_table(rel_bias_table)

    for layer in range(depth):
        last = layer == depth - 1
        if layer % 2 == 0:
            e = layer // 2
            w_in, tn = _even_in_weight(w_in_even[e]), E_NPAD // 2
        else:
            o = layer // 2
            w_in, tn = w_in_odd[o].astype(BF16), 2048
        proj = matmul(hn, w_in, tt=tt_mm, tn=tn)
        if layer % 2 == 0:
            a_out = conv_module(proj, conv_w[e], conv_b[e], conv_ln_gain[e], conv_ln_bias[e], tt=tt_conv)
            wuk_t = jnp.transpose(w_uk[e], (0, 2, 1)).astype(BF16)
            wuv_t = jnp.transpose(w_uv[e], (0, 2, 1)).astype(BF16)
            b_out = dsa_attention(proj, kv_norm_gain[e], wuk_t, wuv_t, bias_tab, topk=topk)
            mixed, w_out = [a_out, b_out], w_out_even[e].astype(BF16)
        else:
            lb = lower_bounds[layer]
            mixed = [hgrn2(proj, jnp.stack([lb, 1.0 - lb], axis=0), rec_norm_gain[o])]
            w_out = w_out_odd[o].astype(BF16)
        if last:
            return out_proj(mixed, w_out, h, final_norm_gain, tt=tt_last, final=True)
        h, hn = out_proj(mixed, w_out, h, norm_gain[layer + 1], tt=tt_out, final=False)
```

```python
import functools
import math

import numpy as np
import jax
import jax.numpy as jnp
from jax import lax
from jax.experimental import pallas as pl
from jax.experimental.pallas import tpu as pltpu

F32 = jnp.float32
BF16 = jnp.bfloat16
I32 = jnp.int32
I16 = jnp.int16

D_MODEL = 2048
CHUNK = 64
N_META = 16
D_A = 1024
CONV_WIDTH = 31
H_B = 8
DH_B = 128
D_C = 256
H_IDX = 16
D_IDX = 64
TOPK_MAX = 256
N_BUCKETS = 32
MAX_DISTANCE = 128
H_C = 16
DK_C = 128
RMS_EPS = 1e-6
LN_EPS = 1e-5
NEG_INF = -1e30
INT_MIN = -(2 ** 31)
LOG2_E = 1.4426950408889634

LANES = 128
SUBLANES = 8
PACK16 = 16
I16_MIN = -(2 ** 15)
MXU_N = 256

FRONT_PAD = LANES - N_META
QB = LANES
KB = MXU_N
HC = 128
N_LEVELS = 7

E_NPAD = 6656
E_COL_Q, E_COL_GB, E_COL_QI = 3072, 4096, 5120
E_COL_C, E_COL_KK, E_COL_WI = 6144, 6400, 6528

MIB = 1024 * 1024


def _cparams(sem, vmem_mib, flags=None):
    return pltpu.CompilerParams(dimension_semantics=sem, vmem_limit_bytes=vmem_mib * MIB, flags=flags)


def _sigmoid(x):
    return 1.0 / (1.0 + jnp.exp(-x))


def _silu(x):
    h = 0.5 * x
    return h + h * jnp.tanh(h)


_NN = (((1,), (0,)), ((), ()))
_NT = (((1,), (1,)), ((), ()))


def _dot(a, b, dims=_NN):
    return lax.dot_general(a, b, dims, preferred_element_type=F32)


def _rms_normed(x, gain):
    ms = jnp.mean(x * x, axis=-1, keepdims=True)
    return x * lax.rsqrt(ms + RMS_EPS) * gain


def _embed_kernel(x_ref, meta_ref, g_ref, h_ref, hn_ref, *, tt):
    first = FRONT_PAD + N_META

    @pl.when(pl.program_id(1) == 0)
    def _():
        d = h_ref.shape[-1]
        meta = meta_ref[...]
        x = x_ref[0, 0:tt - first, :]
        h_ref[0:FRONT_PAD, :] = jnp.zeros((FRONT_PAD, d), F32)
        h_ref[FRONT_PAD:first, :] = meta
        h_ref[first:, :] = x
        hn_ref[0:FRONT_PAD, :] = jnp.zeros((FRONT_PAD, d), BF16)
        hn_ref[FRONT_PAD:first, :] = _rms_normed(meta, g_ref[...]).astype(BF16)
        hn_ref[first:, :] = _rms_normed(x, g_ref[...]).astype(BF16)

    @pl.when(pl.program_id(1) > 0)
    def _():
        x = x_ref[0]
        h_ref[...] = x
        hn_ref[...] = _rms_normed(x, g_ref[...]).astype(BF16)


def embed(x, meta_tokens, gain, *, tt):
    bsz, seq, d = x.shape
    first = FRONT_PAD + N_META
    tp = first + seq
    x_spec = pl.BlockSpec((pl.Element(1), pl.Element(tt), pl.Element(d)),
                          lambda b, j: (b, pl.multiple_of(jnp.maximum(j * tt - first, 0), SUBLANES), 0))
    spec = pl.BlockSpec((None, tt, d), lambda b, j: (b, j, 0))
    return pl.pallas_call(
        functools.partial(_embed_kernel, tt=tt),
        grid=(bsz, tp // tt),
        in_specs=[
            x_spec,
            pl.BlockSpec((N_META, d), lambda b, j: (0, 0)),
            pl.BlockSpec((1, d), lambda b, j: (0, 0)),
        ],
        out_specs=[spec, spec],
        out_shape=[jax.ShapeDtypeStruct((bsz, tp, d), F32), jax.ShapeDtypeStruct((bsz, tp, d), BF16)],
        compiler_params=_cparams(("parallel", "parallel"), 40),
        name="embed",
    )(x.astype(F32), meta_tokens.astype(F32), gain.reshape(1, d).astype(F32))


def _matmul_kernel(x_ref, w_ref, o_ref):
    o_ref[...] = _dot(x_ref[...], w_ref[...]).astype(o_ref.dtype)


def matmul(hn, w, *, tt, tn):
    bsz, tp, d = hn.shape
    n = w.shape[1]
    return pl.pallas_call(
        _matmul_kernel,
        grid=(bsz, tp // tt, n // tn),
        in_specs=[
            pl.BlockSpec((None, tt, d), lambda b, i, j: (b, i, 0)),
            pl.BlockSpec((d, tn), lambda b, i, j: (0, j)),
        ],
        out_specs=pl.BlockSpec((None, tt, tn), lambda b, i, j: (b, i, j)),
        out_shape=jax.ShapeDtypeStruct((bsz, tp, n), BF16),
        compiler_params=_cparams(("parallel", "parallel", "parallel"), 56),
        name="in_proj",
    )(hn, w)


def _out_proj_kernel(*refs, n_in, tt, final):
    xs = refs[:n_in]
    w_ref, h_ref, g_ref = refs[n_in], refs[n_in + 1], refs[n_in + 2]
    tile = lambda ref: ref[0] if len(ref.shape) == 3 else ref[...]
    y = None
    off = 0
    for x_ref in xs:
        kdim = x_ref.shape[-1]
        part = _dot(tile(x_ref), w_ref[off:off + kdim, :])
        y = part if y is None else y + part
        off += kdim
    h_new = tile(h_ref) + y
    if final:
        refs[-1][...] = _rms_normed(h_new, g_ref[...])
    else:
        row = pl.program_id(1) * tt + lax.broadcasted_iota(I32, (tt, 1), 0)
        h_new = jnp.where(row >= FRONT_PAD, h_new, 0.0)
        refs[-2][...] = h_new
        refs[-1][...] = _rms_normed(h_new, g_ref[...]).astype(BF16)


def out_proj(xs, w, h, gain, *, tt, final):
    bsz, tp, d = h.shape
    if final:
        first = FRONT_PAD + N_META
        rows_out = tp - first
        row_spec = lambda k: pl.BlockSpec((pl.Element(1), pl.Element(tt), pl.Element(k)),
                                          lambda b, i: (b, pl.multiple_of(first + i * tt, LANES), 0))
    else:
        rows_out = tp
        row_spec = lambda k: pl.BlockSpec((None, tt, k), lambda b, i: (b, i, 0))
    in_specs = [row_spec(x.shape[-1]) for x in xs]
    in_specs += [pl.BlockSpec(w.shape, lambda b, i: (0, 0)), row_spec(d), pl.BlockSpec((1, d), lambda b, i: (0, 0))]
    out_spec = pl.BlockSpec((None, tt, d), lambda b, i: (b, i, 0))
    out_f32 = jax.ShapeDtypeStruct((bsz, rows_out, d), F32)
    return pl.pallas_call(
        functools.partial(_out_proj_kernel, n_in=len(xs), tt=tt, final=final),
        grid=(bsz, rows_out // tt),
        in_specs=in_specs,
        out_specs=out_spec if final else [out_spec, out_spec],
        out_shape=out_f32 if final else [out_f32, jax.ShapeDtypeStruct((bsz, rows_out, d), BF16)],
        compiler_params=_cparams(("parallel", "parallel"), 56),
        name="out_proj_final" if final else "out_proj",
    )(*xs, w, h, gain.reshape(1, d).astype(F32))


CONV_HALO = 32
CONV_RB = 16
CONV_CHAINS = 4


def _conv_kernel(v_ref, g_ref, ga_ref, w_ref, cb_ref, lg_ref, lb_ref, o_ref, us_ref, cv_ref, *, tt):
    j = pl.program_id(1)
    u_new = v_ref[...].astype(F32) * _sigmoid(g_ref[...].astype(F32))
    span = tt + CONV_HALO - SUBLANES
    n_rb = tt // CONV_RB
    for cb in range(D_A // LANES):
        lanes = slice(cb * LANES, (cb + 1) * LANES)

        @pl.when(j == 0)
        def _(cb=cb):
            us_ref[0, cb, 0:CONV_HALO, :] = jnp.zeros((CONV_HALO, LANES), F32)

        @pl.when(j > 0)
        def _(cb=cb):
            us_ref[0, cb, 0:CONV_HALO, :] = us_ref[0, cb, tt:tt + CONV_HALO, :]

        us_ref[0, cb, CONV_HALO:CONV_HALO + tt, :] = u_new[:, lanes]
        for r in range(1, SUBLANES):
            us_ref[r, cb, 0:span, :] = us_ref[0, cb, r:r + span, :]

        wb = [jnp.broadcast_to(w_ref[k:k + 1, lanes], (CONV_RB, LANES)) for k in range(CONV_WIDTH)]
        bias = jnp.broadcast_to(cb_ref[0:1, lanes], (CONV_RB, LANES))

        def rb_body(rb, carry, cb=cb, lanes=lanes, wb=wb, bias=bias):
            base = pl.multiple_of(rb * CONV_RB, CONV_RB)
            parts = [bias] + [None] * (CONV_CHAINS - 1)
            for k in range(CONV_WIDTH):
                a, r = divmod(k + CONV_HALO - (CONV_WIDTH - 1), SUBLANES)
                term = wb[k] * us_ref[r, cb, pl.ds(base + SUBLANES * a, CONV_RB), :]
                c = k % CONV_CHAINS
                parts[c] = term if parts[c] is None else parts[c] + term
            cv_ref[pl.ds(base, CONV_RB), lanes] = (parts[0] + parts[1]) + (parts[2] + parts[3])
            return carry

        lax.fori_loop(0, n_rb, rb_body, 0)

    x = cv_ref[...]
    mu = jnp.mean(x, axis=-1, keepdims=True)
    xc = x - mu
    var = jnp.mean(xc * xc, axis=-1, keepdims=True)
    y = xc * lax.rsqrt(var + LN_EPS) * lg_ref[...] + lb_ref[...]
    ga = ga_ref[...].astype(F32)
    o_ref[...] = (_silu(y) * _silu(ga)).astype(o_ref.dtype)


def conv_module(proj, conv_w, conv_b, ln_g, ln_b, *, tt):
    bsz, tp, _ = proj.shape
    row = lambda a: a.reshape(1, D_A).astype(F32)
    return pl.pallas_call(
        functools.partial(_conv_kernel, tt=tt),
        grid=(bsz, tp // tt),
        in_specs=[
            pl.BlockSpec((None, tt, D_A), lambda b, j: (b, j, 0)),
            pl.BlockSpec((None, tt, D_A), lambda b, j: (b, j, 1)),
            pl.BlockSpec((None, tt, D_A), lambda b, j: (b, j, 2)),
            pl.BlockSpec((CONV_WIDTH, D_A), lambda b, j: (0, 0)),
            pl.BlockSpec((1, D_A), lambda b, j: (0, 0)),
            pl.BlockSpec((1, D_A), lambda b, j: (0, 0)),
            pl.BlockSpec((1, D_A), lambda b, j: (0, 0)),
        ],
        out_specs=pl.BlockSpec((None, tt, D_A), lambda b, j: (b, j, 0)),
        out_shape=jax.ShapeDtypeStruct((bsz, tp, D_A), BF16),
        scratch_shapes=[
            pltpu.VMEM((SUBLANES, D_A // LANES, tt + CONV_HALO, LANES), F32),
            pltpu.VMEM((tt, D_A), F32),
        ],
        compiler_params=_cparams(("parallel", "arbitrary"), 40),
        name="conv_module",
    )(proj, proj, proj, conv_w.astype(F32), row(conv_b), row(ln_g), row(ln_b))


BIAS_ROWS = 640
BIAS_ORIGIN = 384


def _attn_kernel(c_ref, kk_ref, wi_ref, q_ref, qi_ref, gb_ref, kvg_ref, wuk_ref, wuvt_ref, bias_ref,
                 o_ref, cn_s, cnt_s, kk_s, st_s, hi_s, lo_s, qlat_s, lg_a, lg_b, acc_s, m_s, l_s,
                 tau_s, *, tp, topk):
    i = pl.program_id(1)
    n_kb = (i + 2) // 2
    n_kb_all = cn_s.shape[0]

    @pl.when(i == 0)
    def _prepare_keys():
        gain = kvg_ref[...]
        for kb in range(n_kb_all):
            lo = kb * KB
            nrow = min(KB, tp - lo)
            c = c_ref[lo:lo + nrow, :].astype(F32)
            cn = c * lax.rsqrt(jnp.mean(c * c, axis=-1, keepdims=True) + RMS_EPS) * gain
            cn_s[kb, 0:nrow, :] = cn.astype(BF16)
            kk = kk_ref[lo:lo + nrow, :]
            lane = lax.broadcasted_iota(I32, kk.shape, 1)
            kk_s[0, kb, 0:nrow, :] = jnp.where(lane < D_IDX, kk, jnp.zeros_like(kk))
            kk_s[1, kb, 0:nrow, :] = jnp.where(lane >= D_IDX, kk, jnp.zeros_like(kk))
            for rb in range(nrow // LANES):
                for cb in range(D_C // LANES):
                    tile = cn[rb * LANES:(rb + 1) * LANES, cb * LANES:(cb + 1) * LANES]
                    cnt_s[kb, cb * LANES:(cb + 1) * LANES, rb * LANES:(rb + 1) * LANES] = tile.T.astype(BF16)
            if nrow < KB:
                cn_s[kb, nrow:KB, :] = jnp.zeros((KB - nrow, D_C), BF16)
                kk_s[:, kb, nrow:KB, :] = jnp.zeros((2, KB - nrow, LANES), BF16)
                cnt_s[kb, :, nrow:KB] = jnp.zeros((D_C, KB - nrow), BF16)

    for h in range(H_B):
        qh = q_ref[:, h * DH_B:(h + 1) * DH_B]
        ql = _dot(qh, wuk_ref[h]) * (DH_B ** -0.5 * LOG2_E)
        qlat_s[h * QB:(h + 1) * QB, :] = ql.astype(BF16)

    wit = wi_ref[...].astype(F32).T

    def score_block(kb):
        k_even = kk_s[0, kb]
        k_odd = kk_s[1, kb]
        acc = jnp.zeros((KB, QB), F32)
        for p in range(H_IDX // 2):
            qp = qi_ref[:, p * LANES:(p + 1) * LANES]
            acc = acc + jnp.maximum(_dot(k_even, qp, _NT), 0.0) * wit[2 * p:2 * p + 1, :]
            acc = acc + jnp.maximum(_dot(k_odd, qp, _NT), 0.0) * wit[2 * p + 1:2 * p + 2, :]
        bits = lax.bitcast_convert_type(acc, I32)
        key = bits ^ ((bits >> 31) & 0x7FFFFFFF)
        s_idx = kb * KB + lax.broadcasted_iota(I32, (KB, QB), 0)
        t_idx = i * QB + lax.broadcasted_iota(I32, (KB, QB), 1)
        adm = (s_idx >= FRONT_PAD) & (jnp.maximum(s_idx >> 6, 1) <= jnp.maximum(t_idx >> 6, 1))
        key = jnp.where(adm, key, INT_MIN)
        st_s[kb] = key
        hi_s[kb] = key >> 16

    def score_pair(j, carry):
        score_block(2 * j)
        score_block(jnp.minimum(2 * j + 1, n_kb_all - 1))
        return carry

    lax.fori_loop(0, (n_kb + 1) // 2, score_pair, 0)

    def search16(ref, need, n_blocks):
        def bit_body(it, carry):
            t, above = carry
            cand = t + lax.shift_left(jnp.int32(1), jnp.int32(15) - it)
            cand8 = jnp.broadcast_to(cand, (SUBLANES, QB))[None]
            parts = []
            for kb in range(n_blocks):
                below = (ref[kb].reshape(KB // SUBLANES, SUBLANES, QB) - cand8) >> 31
                parts += [below[g] for g in range(KB // SUBLANES)]
            while len(parts) > 1:
                nxt = [parts[g] + parts[g + 1] for g in range(0, len(parts) - 1, 2)]
                parts = nxt + parts[len(parts) - len(parts) % 2:]
            cnt = n_blocks * KB + jnp.sum(parts[0], axis=0, keepdims=True)
            ok = cnt >= need
            return jnp.where(ok, cand, t), jnp.where(ok, above, cnt)

        init = (jnp.full((1, QB), I16_MIN, I32), jnp.zeros((1, QB), I32))
        return lax.fori_loop(0, 16, bit_body, init)

    st_s[n_kb] = jnp.full((KB, QB), INT_MIN, I32)
    hi_s[n_kb] = jnp.full((KB, QB), I16_MIN, I32)
    n_pairs = (n_kb + 1) // 2
    few_keys = N_META + QB * i <= topk

    @pl.when(few_keys)
    def _():
        tau_s[...] = jnp.full((1, QB), INT_MIN, I32)

    for pairs in range(1, (n_kb_all + 1) // 2 + 1):
        @pl.when(jnp.logical_and(n_pairs == pairs, jnp.logical_not(few_keys)))
        def _(n_blocks=2 * pairs):
            tau_hi, n_above = search16(hi_s, jnp.full((1, QB), topk, I32), n_blocks)
            for kb in range(n_blocks):
                key = st_s[kb]
                low = (key & 0xFFFF) + I16_MIN
                lo_s[kb] = jnp.where((key >> 16) == tau_hi, low, I16_MIN)
            tau_lo, _ = search16(lo_s, topk - n_above, n_blocks)
            tau_s[...] = lax.shift_left(tau_hi, jnp.int32(16)) | (tau_lo - I16_MIN)

    tau = jnp.maximum(tau_s[...], INT_MIN + 1)

    m_s[...] = jnp.full(m_s.shape, NEG_INF, F32)
    l_s[...] = jnp.zeros(l_s.shape, F32)
    acc_s[...] = jnp.zeros(acc_s.shape, F32)

    def logits_stage(kb, lg_ref):
        lg_ref[...] = _dot(cn_s[kb], qlat_s[...], _NT)

    def softmax_stage(kb, lg_ref):
        neg = jnp.where(st_s[kb] >= tau, 0.0, NEG_INF)
        v0 = pl.multiple_of(jnp.maximum(KB * kb - QB * i + BIAS_ORIGIN, 0), LANES)
        ps, alphas = [], []
        for h in range(H_B):
            lh = lg_ref[:, h * QB:(h + 1) * QB] + bias_ref[h, pl.ds(v0, KB), :] + neg
            m_old = m_s[h:h + 1, :]
            m_new = jnp.maximum(m_old, jnp.max(lh, axis=0, keepdims=True))
            p = jnp.exp2(lh - m_new)
            alpha = jnp.exp2(m_old - m_new)
            l_s[h:h + 1, :] = alpha * l_s[h:h + 1, :] + jnp.sum(p, axis=0, keepdims=True)
            m_s[h:h + 1, :] = m_new
            ps.append(p.astype(BF16))
            alphas.append(alpha)
        pv = _dot(cnt_s[kb], jnp.concatenate(ps, axis=1))
        acc_s[...] = acc_s[...] * jnp.concatenate(alphas, axis=1) + pv

    logits_stage(0, lg_a)

    def att_pair(j, carry):
        kb = 2 * j
        logits_stage(kb + 1, lg_b)
        softmax_stage(kb, lg_a)
        logits_stage(jnp.minimum(kb + 2, n_kb - 1), lg_a)
        softmax_stage(kb + 1, lg_b)
        return carry

    lax.fori_loop(0, n_kb // 2, att_pair, 0)

    @pl.when(n_kb % 2 == 1)
    def _():
        softmax_stage(n_kb - 1, lg_a)

    for h in range(H_B):
        cols = slice(h * QB, (h + 1) * QB)
        ot = (acc_s[:, cols] * (1.0 / l_s[h:h + 1, :])).astype(BF16)
        bt = _dot(wuvt_ref[h], ot)
        gate = gb_ref[:, cols].astype(F32)
        o_ref[:, cols] = (bt.T * _silu(gate)).astype(o_ref.dtype)


def dsa_attention(proj, kv_gain, wuk_t, wuv_t, bias_tab, *, topk):
    bsz, tp, _ = proj.shape
    n_qb = tp // QB
    n_kb_all = -(-tp // KB)
    return pl.pallas_call(
        functools.partial(_attn_kernel, tp=tp, topk=topk),
        grid=(bsz, n_qb),
        in_specs=[
            pl.BlockSpec((None, tp, D_C), lambda b, i: (b, 0, E_COL_C // D_C)),
            pl.BlockSpec((None, tp, LANES), lambda b, i: (b, 0, E_COL_KK // LANES)),
            pl.BlockSpec((None, QB, LANES), lambda b, i: (b, i, E_COL_WI // LANES)),
            pl.BlockSpec((None, QB, H_B * DH_B), lambda b, i: (b, i, E_COL_Q // (H_B * DH_B))),
            pl.BlockSpec((None, QB, H_IDX * D_IDX), lambda b, i: (b, i, E_COL_QI // (H_IDX * D_IDX))),
            pl.BlockSpec((None, QB, H_B * DH_B), lambda b, i: (b, i, E_COL_GB // (H_B * DH_B))),
            pl.BlockSpec((1, D_C), lambda b, i: (0, 0)),
            pl.BlockSpec((H_B, DH_B, D_C), lambda b, i: (0, 0, 0)),
            pl.BlockSpec((H_B, DH_B, D_C), lambda b, i: (0, 0, 0)),
            pl.BlockSpec((H_B, BIAS_ROWS, LANES), lambda b, i: (0, 0, 0)),
        ],
        out_specs=pl.BlockSpec((None, QB, H_B * DH_B), lambda b, i: (b, i, 0)),
        out_shape=jax.ShapeDtypeStruct((bsz, tp, H_B * DH_B), BF16),
        scratch_shapes=[
            pltpu.VMEM((n_kb_all, KB, D_C), BF16),
            pltpu.VMEM((n_kb_all, D_C, KB), BF16),
            pltpu.VMEM((2, n_kb_all, KB, LANES), BF16),
            pltpu.VMEM((n_kb_all + 1, KB, QB), I32),
            pltpu.VMEM((n_kb_all + 1, KB, QB), I32),
            pltpu.VMEM((n_kb_all + 1, KB, QB), I32),
            pltpu.VMEM((H_B * QB, D_C), BF16),
            pltpu.VMEM((KB, H_B * QB), F32),
            pltpu.VMEM((KB, H_B * QB), F32),
            pltpu.VMEM((D_C, H_B * QB), F32),
            pltpu.VMEM((H_B, QB), F32),
            pltpu.VMEM((H_B, QB), F32),
            pltpu.VMEM((1, QB), I32),
        ],
        compiler_params=_cparams(("parallel", "arbitrary"), 40),
        name="dsa_attention",
    )(proj, proj, proj, proj, proj, proj, kv_gain.reshape(1, D_C).astype(F32), wuk_t, wuv_t, bias_tab)


def _hgrn_constants():
    n = HC
    t = np.arange(n)[:, None]
    j = np.arange(n)[None, :]
    mats = [(j <= t)]
    for lvl in range(N_LEVELS):
        upper = ((t >> lvl) & 1) == 1
        m_up = (t >> lvl) << lvl
        m_lo = ((t >> lvl) + 1) << lvl
        mats.append(np.where(upper, (j >= m_up) & (j <= t), (j > t) & (j < m_lo)))
    m = np.concatenate(mats, axis=0).astype(np.float32)
    mcat = np.concatenate([m, m], axis=1)
    x = t ^ j
    lev = np.where(j < t, np.floor(np.log2(np.maximum(x, 1))).astype(np.int32), np.where(j == t, -1, -2))
    masks = np.stack([(lev == lvl) for lvl in range(-1, N_LEVELS)]).astype(np.float32)
    return mcat, masks


HGRN_HEADS_PER_STEP = 4


def _role_select(qf, k, lvl, up_small):
    half = 1 << lvl
    if half >= SUBLANES:
        pieces = []
        for blk in range(HC // (2 * half)):
            lo = 2 * half * blk
            pieces += [k[lo:lo + half], qf[lo + half:lo + 2 * half]]
        return jnp.concatenate(pieces, axis=0)
    g, w = HC // SUBLANES, qf.shape[-1]
    sel = jnp.where(up_small[lvl], qf.reshape(g, SUBLANES, w), k.reshape(g, SUBLANES, w))
    return sel.reshape(HC, w)


_SLOT_QE, _SLOT_KT, _SLOT_QB, _SLOT_KB, _SLOT_Y0 = 0, 1, 2, 3, 4
_N_SLOTS = _SLOT_Y0 + N_LEVELS


def _hgrn_kernel(q_ref, fz_ref, i_ref, g_ref, lbp_ref, ng_ref, mcat_ref, mask_ref, o_ref,
                 s_ref, stage_a, stage_b, dec_a, dec_b, *, n_steps, nh):
    wide = nh * DK_C
    row8 = lax.broadcasted_iota(I32, (1, SUBLANES, wide), 1)
    up_small = [((row8 >> lvl) & 1) == 1 for lvl in range(3)]
    s_ref[...] = jnp.zeros(s_ref.shape, F32)

    def gate_stage(c):
        rows = pl.ds(pl.multiple_of(c * HC, HC), HC)
        lb = lbp_ref[0:1, :]
        one_m_lb = lbp_ref[1:2, :]
        qf = _silu(q_ref[rows, :].astype(F32))
        z = fz_ref[rows, :].astype(F32)
        w = one_m_lb * (0.5 + 0.5 * jnp.tanh(0.5 * z))
        k = one_m_lb - w
        lf2 = jnp.log(lb + w) * LOG2_E
        hi = lf2.astype(BF16)
        lo = (lf2 - hi.astype(F32)).astype(BF16)
        return qf, k, _dot(mcat_ref[...], jnp.concatenate([hi, lo], axis=0))

    def operand_stage(gates, stage, dec_ref):
        qf, k, gsum = gates
        b = gsum[0:HC]
        e_b = jnp.exp2(b)
        e_s = jnp.exp2(b[HC - 1:HC, :] - b)
        stage[_SLOT_QE] = (qf * e_b).astype(BF16)
        stage[_SLOT_KT] = (k * e_s).astype(BF16)
        stage[_SLOT_QB] = qf.astype(BF16)
        stage[_SLOT_KB] = k.astype(BF16)
        for lvl in range(N_LEVELS):
            y = _role_select(qf, k, lvl, up_small) * jnp.exp2(gsum[(1 + lvl) * HC:(2 + lvl) * HC])
            stage[_SLOT_Y0 + lvl] = y.astype(BF16)
        dec_ref[...] = e_b[HC - 1:HC, :]

    def level_stage(c, stage):
        lns = [slice(hh * DK_C, (hh + 1) * DK_C) for hh in range(nh)]
        sts = [s_ref[hh] for hh in range(nh)]
        outs = [_dot(stage[_SLOT_QE, :, ln], st.astype(BF16), _NT) for ln, st in zip(lns, sts)]
        accs = [_dot(stage[_SLOT_QB, :, ln], stage[_SLOT_KB, :, ln], _NT) * mask_ref[0] for ln in lns]
        for lvl in range(N_LEVELS):
            for hh, ln in enumerate(lns):
                y = stage[_SLOT_Y0 + lvl, :, ln]
                accs[hh] = accs[hh] + _dot(y, y, _NT) * mask_ref[lvl + 1]
        return sts, outs, accs

    def output_stage(c, partial, stage, dec_ref):
        sts, outs, accs = partial
        rows = pl.ds(pl.multiple_of(c * HC, HC), HC)
        lns = [slice(hh * DK_C, (hh + 1) * DK_C) for hh in range(nh)]
        vhs = [i_ref[rows, ln] for ln in lns]
        for hh, ln in enumerate(lns):
            vt = vhs[hh].astype(F32).T.astype(BF16)
            s_ref[hh] = sts[hh] * dec_ref[:, ln] + _dot(vt, stage[_SLOT_KT, :, ln])
        normed = []
        for hh in range(nh):
            o = outs[hh] + _dot(accs[hh].astype(BF16), vhs[hh])
            normed.append(o * lax.rsqrt(jnp.mean(o * o, axis=-1, keepdims=True) + RMS_EPS))
        on = jnp.concatenate(normed, axis=1) * ng_ref[...]
        gx = g_ref[rows, :].astype(F32)
        o_ref[rows, :] = (on * _silu(gx)).astype(o_ref.dtype)

    last = n_steps - 1
    operand_stage(gate_stage(0), stage_a, dec_a)

    def pair(j, carry):
        c = 2 * j
        partial = level_stage(c, stage_a)
        gates = gate_stage(c + 1)
        output_stage(c, partial, stage_a, dec_a)
        operand_stage(gates, stage_b, dec_b)
        partial = level_stage(c + 1, stage_b)
        gates = gate_stage(jnp.minimum(c + 2, last))
        output_stage(c + 1, partial, stage_b, dec_b)
        operand_stage(gates, stage_a, dec_a)
        return carry

    lax.fori_loop(0, n_steps // 2, pair, 0)
    if n_steps % 2:
        output_stage(last, level_stage(last, stage_a), stage_a, dec_a)


def hgrn2(proj, lb_params, norm_g):
    bsz, tp, _ = proj.shape
    mcat, masks = _hgrn_constants()
    nh = HGRN_HEADS_PER_STEP
    wide = nh * DK_C
    n_groups = H_C // nh
    col = lambda off: pl.BlockSpec((None, tp, wide), lambda b, h, off=off: (b, 0, off + h))
    return pl.pallas_call(
        functools.partial(_hgrn_kernel, n_steps=tp // HC, nh=nh),
        grid=(bsz, n_groups),
        in_specs=[
            col(0), col(n_groups), col(2 * n_groups), col(3 * n_groups),
            pl.BlockSpec((2, wide), lambda b, h: (0, h)),
            pl.BlockSpec((1, wide), lambda b, h: (0, h)),
            pl.BlockSpec(mcat.shape, lambda b, h: (0, 0)),
            pl.BlockSpec(masks.shape, lambda b, h: (0, 0, 0)),
        ],
        out_specs=pl.BlockSpec((None, tp, wide), lambda b, h: (b, 0, h)),
        out_shape=jax.ShapeDtypeStruct((bsz, tp, H_C * DK_C), BF16),
        scratch_shapes=[
            pltpu.VMEM((nh, DK_C, DK_C), F32),
            pltpu.VMEM((_N_SLOTS, HC, wide), BF16),
            pltpu.VMEM((_N_SLOTS, HC, wide), BF16),
            pltpu.VMEM((1, wide), F32),
            pltpu.VMEM((1, wide), F32),
        ],
        compiler_params=_cparams(("parallel", "parallel"), 40),
        name="hgrn2",
    )(proj, proj, proj, proj, lb_params, norm_g.reshape(1, -1).astype(F32),
      jnp.asarray(mcat, BF16), jnp.asarray(masks, F32))


def _t5_bucket(rel):
    nb = N_BUCKETS // 2
    ret = jnp.where(rel > 0, nb, 0)
    n = jnp.abs(rel)
    max_exact = nb // 2
    nf = jnp.maximum(n, 1).astype(F32)
    large = max_exact + (jnp.log(nf / max_exact) / math.log(MAX_DISTANCE / max_exact)
                         * (nb - max_exact)).astype(I32)
    large = jnp.minimum(large, nb - 1)
    return ret + jnp.where(n < max_exact, n, large)


def _bias_table(rel_bias_table):
    v = jnp.arange(BIAS_ROWS, dtype=I32)[:, None]
    q = jnp.arange(LANES, dtype=I32)[None, :]
    onehot = jax.nn.one_hot(_t5_bucket(v - BIAS_ORIGIN - q), N_BUCKETS, dtype=F32)
    tab = jnp.einsum("vqb,bh->hvq", onehot, rel_bias_table.astype(F32), precision=lax.Precision.HIGHEST)
    return tab * LOG2_E


def _even_in_weight(w):
    glu_v, glu_g, gate_a, q, c, gate_b, qi, ki, wi = jnp.split(
        w, np.cumsum([D_A, D_A, D_A, H_B * DH_B, D_C, H_B * DH_B, H_IDX * D_IDX, D_IDX])[:8].tolist(), axis=1)
    pad = jnp.zeros((w.shape[0], E_NPAD - E_COL_WI - H_IDX), w.dtype)
    cols = [glu_v, glu_g, gate_a, q, gate_b, qi * (D_IDX ** -0.5), c, ki, ki, wi * (H_IDX ** -0.5), pad]
    return jnp.concatenate([col.astype(BF16) for col in cols], axis=1)


def kernel(x, meta_tokens, norm_gain, final_norm_gain, rel_bias_table, w_in_even, conv_w, conv_b,
           conv_ln_gain, conv_ln_bias, kv_norm_gain, w_uk, w_uv, w_out_even, w_in_odd, lb_logits,
           rec_norm_gain, w_out_odd):
    bsz, seq, d = x.shape
    assert d == D_MODEL and seq % LANES == 0
    depth = norm_gain.shape[0]
    tp = FRONT_PAD + N_META + seq
    topk = min(TOPK_MAX, seq // 4)
    tt_mm = tp // 2 if (tp // 2) % 16 == 0 else tp
    tt_out = tp // 4 if (tp // 4) % 16 == 0 else tp
    tt_last = seq // 4
    tt_conv = tp // 8 if (tp // 8) % CONV_RB == 0 else tp // 2 if (tp // 2) % CONV_RB == 0 else tp
    h, hn = embed(x, meta_tokens, norm_gain[0], tt=tt_out)

    lb_soft = jax.nn.softmax(lb_logits.astype(F32), axis=0)
    lower_bounds = jnp.cumsum(lb_soft, axis=0) - lb_soft[0]
    bias_tab = _bias_table(rel_bias_table)

    for layer in range(depth):
        last = layer == depth - 1
        if layer % 2 == 0:
            e = layer // 2
            w_in, tn = _even_in_weight(w_in_even[e]), E_NPAD // 2
        else:
            o = layer // 2
            w_in, tn = w_in_odd[o].astype(BF16), 2048
        proj = matmul(hn, w_in, tt=tt_mm, tn=tn)
        if layer % 2 == 0:
            a_out = conv_module(proj, conv_w[e], conv_b[e], conv_ln_gain[e], conv_ln_bias[e], tt=tt_conv)
            wuk_t = jnp.transpose(w_uk[e], (0, 2, 1)).astype(BF16)
            wuv_t = jnp.transpose(w_uv[e], (0, 2, 1)).astype(BF16)
            b_out = dsa_attention(proj, kv_norm_gain[e], wuk_t, wuv_t, bias_tab, topk=topk)
            mixed, w_out = [a_out, b_out], w_out_even[e].astype(BF16)
        else:
            lb = lower_bounds[layer]
            mixed = [hgrn2(proj, jnp.stack([lb, 1.0 - lb], axis=0), rec_norm_gain[o])]
            w_out = w_out_odd[o].astype(BF16)
        if last:
            return out_proj(mixed, w_out, h, final_norm_gain, tt=tt_last, final=True)
        h, hn = out_proj(mixed, w_out, h, norm_gain[layer + 1], tt=tt_out, final=False)
```

```python
import functools
import math

import numpy as np
import jax
import jax.numpy as jnp
from jax import lax
from jax.experimental import pallas as pl
from jax.experimental.pallas import tpu as pltpu

F32 = jnp.float32
BF16 = jnp.bfloat16
I32 = jnp.int32

D_MODEL = 2048
CHUNK_LOG2 = 6
N_META = 16
D_A = 1024
CONV_WIDTH = 31
H_B = 8
DH_B = 128
D_C = 256
H_IDX = 16
D_IDX = 64
TOPK_MAX = 256
N_BUCKETS = 32
MAX_DISTANCE = 128
H_C = 16
DK_C = 128
RMS_EPS = 1e-6
LN_EPS = 1e-5
NEG_INF = -1e30
INT_MIN = -(2 ** 31)
LOG2_E = 1.4426950408889634

I16_MIN = -(2 ** 15)

LANES = 128
SUBLANES = 8
MXU_N = 256

FRONT_PAD = LANES - N_META
QB = LANES
KB = MXU_N
HC = 128
N_LEVELS = 7
FINE_LEVELS = 3

E_COL_Q = 3 * D_A
E_COL_GB = E_COL_Q + H_B * DH_B
E_COL_QI = E_COL_GB + H_B * DH_B
E_COL_C = E_COL_QI + H_IDX * D_IDX
E_COL_KK = E_COL_C + D_C
E_COL_WI = E_COL_KK + 2 * D_IDX
E_NPAD = E_COL_WI + LANES
assert 2 * D_IDX == LANES and E_NPAD % MXU_N == 0

MIB = 1024 * 1024
VMEM_PROJ = 56
VMEM_MIXER = 40


def _cparams(sem, vmem_mib):
    return pltpu.CompilerParams(dimension_semantics=sem, vmem_limit_bytes=vmem_mib * MIB)


def _sigmoid(x):
    return 1.0 / (1.0 + jnp.exp(-x))


def _silu(x):
    h = 0.5 * x
    return h + h * jnp.tanh(h)


_NN = (((1,), (0,)), ((), ()))
_NT = (((1,), (1,)), ((), ()))


def _dot(a, b, dims=_NN):
    return lax.dot_general(a, b, dims, preferred_element_type=F32)


def _rms_normed(x, gain):
    ms = jnp.mean(x * x, axis=-1, keepdims=True)
    return x * lax.rsqrt(ms + RMS_EPS) * gain


def _embed_kernel(x_ref, meta_ref, g_ref, h_ref, hn_ref, *, tt):
    first = FRONT_PAD + N_META

    @pl.when(pl.program_id(1) == 0)
    def _():
        d = h_ref.shape[-1]
        meta = meta_ref[...]
        x = x_ref[0, 0:tt - first, :]
        h_ref[0:FRONT_PAD, :] = jnp.zeros((FRONT_PAD, d), F32)
        h_ref[FRONT_PAD:first, :] = meta
        h_ref[first:, :] = x
        hn_ref[0:FRONT_PAD, :] = jnp.zeros((FRONT_PAD, d), BF16)
        hn_ref[FRONT_PAD:first, :] = _rms_normed(meta, g_ref[...]).astype(BF16)
        hn_ref[first:, :] = _rms_normed(x, g_ref[...]).astype(BF16)

    @pl.when(pl.program_id(1) > 0)
    def _():
        x = x_ref[0]
        h_ref[...] = x
        hn_ref[...] = _rms_normed(x, g_ref[...]).astype(BF16)


def embed(x, meta_tokens, gain, *, tt):
    bsz, seq, d = x.shape
    first = FRONT_PAD + N_META
    tp = first + seq
    x_spec = pl.BlockSpec((pl.Element(1), pl.Element(tt), pl.Element(d)),
                          lambda b, j: (b, pl.multiple_of(jnp.maximum(j * tt - first, 0), SUBLANES), 0))
    spec = pl.BlockSpec((None, tt, d), lambda b, j: (b, j, 0))
    return pl.pallas_call(
        functools.partial(_embed_kernel, tt=tt),
        grid=(bsz, tp // tt),
        in_specs=[
            x_spec,
            pl.BlockSpec((N_META, d), lambda b, j: (0, 0)),
            pl.BlockSpec((1, d), lambda b, j: (0, 0)),
        ],
        out_specs=[spec, spec],
        out_shape=[jax.ShapeDtypeStruct((bsz, tp, d), F32), jax.ShapeDtypeStruct((bsz, tp, d), BF16)],
        compiler_params=_cparams(("parallel", "parallel"), VMEM_MIXER),
        name="embed",
    )(x.astype(F32), meta_tokens.astype(F32), gain.reshape(1, d).astype(F32))


def _matmul_kernel(x_ref, w_ref, o_ref):
    o_ref[...] = _dot(x_ref[...], w_ref[...]).astype(o_ref.dtype)


def matmul(hn, w, *, tt, tn):
    bsz, tp, d = hn.shape
    n = w.shape[1]
    return pl.pallas_call(
        _matmul_kernel,
        grid=(bsz, tp // tt, n // tn),
        in_specs=[
            pl.BlockSpec((None, tt, d), lambda b, i, j: (b, i, 0)),
            pl.BlockSpec((d, tn), lambda b, i, j: (0, j)),
        ],
        out_specs=pl.BlockSpec((None, tt, tn), lambda b, i, j: (b, i, j)),
        out_shape=jax.ShapeDtypeStruct((bsz, tp, n), BF16),
        compiler_params=_cparams(("parallel", "parallel", "parallel"), VMEM_PROJ),
        name="in_proj",
    )(hn, w)


def _out_proj_kernel(*refs, n_in, tt, final):
    xs = refs[:n_in]
    w_ref, h_ref, g_ref = refs[n_in], refs[n_in + 1], refs[n_in + 2]
    tile = lambda ref: ref[0] if len(ref.shape) == 3 else ref[...]
    y = None
    off = 0
    for x_ref in xs:
        kdim = x_ref.shape[-1]
        part = _dot(tile(x_ref), w_ref[off:off + kdim, :])
        y = part if y is None else y + part
        off += kdim
    h_new = tile(h_ref) + y
    if final:
        refs[-1][...] = _rms_normed(h_new, g_ref[...])
    else:
        row = pl.program_id(1) * tt + lax.broadcasted_iota(I32, (tt, 1), 0)
        h_new = jnp.where(row >= FRONT_PAD, h_new, 0.0)
        refs[-2][...] = h_new
        refs[-1][...] = _rms_normed(h_new, g_ref[...]).astype(BF16)


def out_proj(xs, w, h, gain, *, tt, final):
    bsz, tp, d = h.shape
    if final:
        first = FRONT_PAD + N_META
        rows_out = tp - first
        row_spec = lambda k: pl.BlockSpec((pl.Element(1), pl.Element(tt), pl.Element(k)),
                                          lambda b, i: (b, pl.multiple_of(first + i * tt, LANES), 0))
    else:
        rows_out = tp
        row_spec = lambda k: pl.BlockSpec((None, tt, k), lambda b, i: (b, i, 0))
    in_specs = [row_spec(x.shape[-1]) for x in xs]
    in_specs += [pl.BlockSpec(w.shape, lambda b, i: (0, 0)), row_spec(d), pl.BlockSpec((1, d), lambda b, i: (0, 0))]
    out_spec = pl.BlockSpec((None, tt, d), lambda b, i: (b, i, 0))
    out_f32 = jax.ShapeDtypeStruct((bsz, rows_out, d), F32)
    return pl.pallas_call(
        functools.partial(_out_proj_kernel, n_in=len(xs), tt=tt, final=final),
        grid=(bsz, rows_out // tt),
        in_specs=in_specs,
        out_specs=out_spec if final else [out_spec, out_spec],
        out_shape=out_f32 if final else [out_f32, jax.ShapeDtypeStruct((bsz, rows_out, d), BF16)],
        compiler_params=_cparams(("parallel", "parallel"), VMEM_PROJ),
        name="out_proj_final" if final else "out_proj",
    )(*xs, w, h, gain.reshape(1, d).astype(F32))


CONV_HALO = 32
CONV_RB = 16
CONV_CHAINS = 4


def _conv_kernel(v_ref, g_ref, ga_ref, w_ref, cb_ref, lg_ref, lb_ref, o_ref, us_ref, cv_ref, *, tt):
    j = pl.program_id(1)
    u_new = v_ref[...].astype(F32) * _sigmoid(g_ref[...].astype(F32))
    span = tt + CONV_HALO - SUBLANES
    n_rb = tt // CONV_RB
    for cb in range(D_A // LANES):
        lanes = slice(cb * LANES, (cb + 1) * LANES)

        @pl.when(j == 0)
        def _(cb=cb):
            us_ref[0, cb, 0:CONV_HALO, :] = jnp.zeros((CONV_HALO, LANES), F32)

        @pl.when(j > 0)
        def _(cb=cb):
            us_ref[0, cb, 0:CONV_HALO, :] = us_ref[0, cb, tt:tt + CONV_HALO, :]

        us_ref[0, cb, CONV_HALO:CONV_HALO + tt, :] = u_new[:, lanes]
        for r in range(1, SUBLANES):
            us_ref[r, cb, 0:span, :] = us_ref[0, cb, r:r + span, :]

        wb = [jnp.broadcast_to(w_ref[k:k + 1, lanes], (CONV_RB, LANES)) for k in range(CONV_WIDTH)]
        bias = jnp.broadcast_to(cb_ref[0:1, lanes], (CONV_RB, LANES))

        def rb_body(rb, carry, cb=cb, lanes=lanes, wb=wb, bias=bias):
            base = pl.multiple_of(rb * CONV_RB, CONV_RB)
            parts = [bias] + [None] * (CONV_CHAINS - 1)
            for k in range(CONV_WIDTH):
                a, r = divmod(k + CONV_HALO - (CONV_WIDTH - 1), SUBLANES)
                term = wb[k] * us_ref[r, cb, pl.ds(base + SUBLANES * a, CONV_RB), :]
                c = k % CONV_CHAINS
                parts[c] = term if parts[c] is None else parts[c] + term
            cv_ref[pl.ds(base, CONV_RB), lanes] = (parts[0] + parts[1]) + (parts[2] + parts[3])
            return carry

        lax.fori_loop(0, n_rb, rb_body, 0)

    x = cv_ref[...]
    mu = jnp.mean(x, axis=-1, keepdims=True)
    xc = x - mu
    var = jnp.mean(xc * xc, axis=-1, keepdims=True)
    y = xc * lax.rsqrt(var + LN_EPS) * lg_ref[...] + lb_ref[...]
    ga = ga_ref[...].astype(F32)
    o_ref[...] = (_silu(y) * _silu(ga)).astype(o_ref.dtype)


def conv_module(proj, conv_w, conv_b, ln_g, ln_b, *, tt):
    bsz, tp, _ = proj.shape
    row = lambda a: a.reshape(1, D_A).astype(F32)
    return pl.pallas_call(
        functools.partial(_conv_kernel, tt=tt),
        grid=(bsz, tp // tt),
        in_specs=[
            pl.BlockSpec((None, tt, D_A), lambda b, j: (b, j, 0)),
            pl.BlockSpec((None, tt, D_A), lambda b, j: (b, j, 1)),
            pl.BlockSpec((None, tt, D_A), lambda b, j: (b, j, 2)),
            pl.BlockSpec((CONV_WIDTH, D_A), lambda b, j: (0, 0)),
            pl.BlockSpec((1, D_A), lambda b, j: (0, 0)),
            pl.BlockSpec((1, D_A), lambda b, j: (0, 0)),
            pl.BlockSpec((1, D_A), lambda b, j: (0, 0)),
        ],
        out_specs=pl.BlockSpec((None, tt, D_A), lambda b, j: (b, j, 0)),
        out_shape=jax.ShapeDtypeStruct((bsz, tp, D_A), BF16),
        scratch_shapes=[
            pltpu.VMEM((SUBLANES, D_A // LANES, tt + CONV_HALO, LANES), F32),
            pltpu.VMEM((tt, D_A), F32),
        ],
        compiler_params=_cparams(("parallel", "arbitrary"), VMEM_MIXER),
        name="conv_module",
    )(proj, proj, proj, conv_w.astype(F32), row(conv_b), row(ln_g), row(ln_b))


BIAS_ORIGIN = KB + QB
BIAS_ROWS = BIAS_ORIGIN + KB


def _attn_kernel(c_ref, kk_ref, wi_ref, q_ref, qi_ref, gb_ref, kvg_ref, wuk_ref, wuvt_ref, bias_ref,
                 o_ref, cn_s, cnt_s, kk_s, st_s, hi_s, lo_s, qlat_s, lg_a, lg_b, acc_s, m_s, l_s,
                 tau_s, *, tp, topk):
    i = pl.program_id(1)
    n_kb = (i + 2) // 2
    n_kb_all = cn_s.shape[0]

    @pl.when(i == 0)
    def _prepare_keys():
        gain = kvg_ref[...]
        for kb in range(n_kb_all):
            lo = kb * KB
            nrow = min(KB, tp - lo)
            c = c_ref[lo:lo + nrow, :].astype(F32)
            cn = c * lax.rsqrt(jnp.mean(c * c, axis=-1, keepdims=True) + RMS_EPS) * gain
            cn_s[kb, 0:nrow, :] = cn.astype(BF16)
            kk = kk_ref[lo:lo + nrow, :]
            lane = lax.broadcasted_iota(I32, kk.shape, 1)
            kk_s[0, kb, 0:nrow, :] = jnp.where(lane < D_IDX, kk, jnp.zeros_like(kk))
            kk_s[1, kb, 0:nrow, :] = jnp.where(lane >= D_IDX, kk, jnp.zeros_like(kk))
            for rb in range(nrow // LANES):
                for cb in range(D_C // LANES):
                    tile = cn[rb * LANES:(rb + 1) * LANES, cb * LANES:(cb + 1) * LANES]
                    cnt_s[kb, cb * LANES:(cb + 1) * LANES, rb * LANES:(rb + 1) * LANES] = tile.T.astype(BF16)
            if nrow < KB:
                cn_s[kb, nrow:KB, :] = jnp.zeros((KB - nrow, D_C), BF16)
                kk_s[:, kb, nrow:KB, :] = jnp.zeros((2, KB - nrow, LANES), BF16)
                cnt_s[kb, :, nrow:KB] = jnp.zeros((D_C, KB - nrow), BF16)

    for h in range(H_B):
        qh = q_ref[:, h * DH_B:(h + 1) * DH_B]
        ql = _dot(qh, wuk_ref[h]) * (DH_B ** -0.5 * LOG2_E)
        qlat_s[h * QB:(h + 1) * QB, :] = ql.astype(BF16)

    wit = wi_ref[...].astype(F32).T

    def score_block(kb):
        k_even = kk_s[0, kb]
        k_odd = kk_s[1, kb]
        acc = jnp.zeros((KB, QB), F32)
        for p in range(H_IDX // 2):
            qp = qi_ref[:, p * LANES:(p + 1) * LANES]
            acc = acc + jnp.maximum(_dot(k_even, qp, _NT), 0.0) * wit[2 * p:2 * p + 1, :]
            acc = acc + jnp.maximum(_dot(k_odd, qp, _NT), 0.0) * wit[2 * p + 1:2 * p + 2, :]
        bits = lax.bitcast_convert_type(acc, I32)
        key = bits ^ ((bits >> 31) & 0x7FFFFFFF)
        s_idx = kb * KB + lax.broadcasted_iota(I32, (KB, QB), 0)
        t_idx = i * QB + lax.broadcasted_iota(I32, (KB, QB), 1)
        adm = (s_idx >= FRONT_PAD) & (jnp.maximum(s_idx >> CHUNK_LOG2, 1) <= jnp.maximum(t_idx >> CHUNK_LOG2, 1))
        key = jnp.where(adm, key, INT_MIN)
        st_s[kb] = key
        hi_s[kb] = key >> 16

    def score_pair(j, carry):
        score_block(2 * j)
        score_block(jnp.minimum(2 * j + 1, n_kb_all - 1))
        return carry

    lax.fori_loop(0, (n_kb + 1) // 2, score_pair, 0)

    def search16(ref, need, n_blocks):
        def bit_body(it, carry):
            t, above = carry
            cand = t + lax.shift_left(jnp.int32(1), jnp.int32(15) - it)
            cand8 = jnp.broadcast_to(cand, (SUBLANES, QB))[None]
            parts = []
            for kb in range(n_blocks):
                below = (ref[kb].reshape(KB // SUBLANES, SUBLANES, QB) - cand8) >> 31
                parts += [below[g] for g in range(KB // SUBLANES)]
            while len(parts) > 1:
                nxt = [parts[g] + parts[g + 1] for g in range(0, len(parts) - 1, 2)]
                parts = nxt + parts[len(parts) - len(parts) % 2:]
            cnt = n_blocks * KB + jnp.sum(parts[0], axis=0, keepdims=True)
            ok = cnt >= need
            return jnp.where(ok, cand, t), jnp.where(ok, above, cnt)

        init = (jnp.full((1, QB), I16_MIN, I32), jnp.zeros((1, QB), I32))
        return lax.fori_loop(0, 16, bit_body, init)

    st_s[n_kb] = jnp.full((KB, QB), INT_MIN, I32)
    hi_s[n_kb] = jnp.full((KB, QB), I16_MIN, I32)
    n_pairs = (n_kb + 1) // 2
    few_keys = N_META + QB * i <= topk

    @pl.when(few_keys)
    def _():
        tau_s[...] = jnp.full((1, QB), INT_MIN, I32)

    for pairs in range(1, (n_kb_all + 1) // 2 + 1):
        @pl.when(jnp.logical_and(n_pairs == pairs, jnp.logical_not(few_keys)))
        def _(n_blocks=2 * pairs):
            tau_hi, n_above = search16(hi_s, jnp.full((1, QB), topk, I32), n_blocks)
            for kb in range(n_blocks):
                key = st_s[kb]
                low = (key & 0xFFFF) + I16_MIN
                lo_s[kb] = jnp.where((key >> 16) == tau_hi, low, I16_MIN)
            tau_lo, _ = search16(lo_s, topk - n_above, n_blocks)
            tau_s[...] = lax.shift_left(tau_hi, jnp.int32(16)) | (tau_lo - I16_MIN)

    tau = jnp.maximum(tau_s[...], INT_MIN + 1)

    m_s[...] = jnp.full(m_s.shape, NEG_INF, F32)
    l_s[...] = jnp.zeros(l_s.shape, F32)
    acc_s[...] = jnp.zeros(acc_s.shape, F32)

    def logits_stage(kb, lg_ref):
        lg_ref[...] = _dot(cn_s[kb], qlat_s[...], _NT)

    def softmax_stage(kb, lg_ref):
        neg = jnp.where(st_s[kb] >= tau, 0.0, NEG_INF)
        v0 = pl.multiple_of(jnp.maximum(KB * kb - QB * i + BIAS_ORIGIN, 0), LANES)
        ps, alphas = [], []
        for h in range(H_B):
            lh = lg_ref[:, h * QB:(h + 1) * QB] + bias_ref[h, pl.ds(v0, KB), :] + neg
            m_old = m_s[h:h + 1, :]
            m_new = jnp.maximum(m_old, jnp.max(lh, axis=0, keepdims=True))
            p = jnp.exp2(lh - m_new)
            alpha = jnp.exp2(m_old - m_new)
            l_s[h:h + 1, :] = alpha * l_s[h:h + 1, :] + jnp.sum(p, axis=0, keepdims=True)
            m_s[h:h + 1, :] = m_new
            ps.append(p.astype(BF16))
            alphas.append(alpha)
        pv = _dot(cnt_s[kb], jnp.concatenate(ps, axis=1))
        acc_s[...] = acc_s[...] * jnp.concatenate(alphas, axis=1) + pv

    logits_stage(0, lg_a)

    def att_pair(j, carry):
        kb = 2 * j
        logits_stage(kb + 1, lg_b)
        softmax_stage(kb, lg_a)
        logits_stage(jnp.minimum(kb + 2, n_kb - 1), lg_a)
        softmax_stage(kb + 1, lg_b)
        return carry

    lax.fori_loop(0, n_kb // 2, att_pair, 0)

    @pl.when(n_kb % 2 == 1)
    def _():
        softmax_stage(n_kb - 1, lg_a)

    for h in range(H_B):
        cols = slice(h * QB, (h + 1) * QB)
        ot = (acc_s[:, cols] * (1.0 / l_s[h:h + 1, :])).astype(BF16)
        bt = _dot(wuvt_ref[h], ot)
        gate = gb_ref[:, cols].astype(F32)
        o_ref[:, cols] = (bt.T * _silu(gate)).astype(o_ref.dtype)


def dsa_attention(proj, kv_gain, wuk_t, wuv_t, bias_tab, *, topk):
    bsz, tp, _ = proj.shape
    n_qb = tp // QB
    n_kb_all = -(-tp // KB)
    return pl.pallas_call(
        functools.partial(_attn_kernel, tp=tp, topk=topk),
        grid=(bsz, n_qb),
        in_specs=[
            pl.BlockSpec((None, tp, D_C), lambda b, i: (b, 0, E_COL_C // D_C)),
            pl.BlockSpec((None, tp, LANES), lambda b, i: (b, 0, E_COL_KK // LANES)),
            pl.BlockSpec((None, QB, LANES), lambda b, i: (b, i, E_COL_WI // LANES)),
            pl.BlockSpec((None, QB, H_B * DH_B), lambda b, i: (b, i, E_COL_Q // (H_B * DH_B))),
            pl.BlockSpec((None, QB, H_IDX * D_IDX), lambda b, i: (b, i, E_COL_QI // (H_IDX * D_IDX))),
            pl.BlockSpec((None, QB, H_B * DH_B), lambda b, i: (b, i, E_COL_GB // (H_B * DH_B))),
            pl.BlockSpec((1, D_C), lambda b, i: (0, 0)),
            pl.BlockSpec((H_B, DH_B, D_C), lambda b, i: (0, 0, 0)),
            pl.BlockSpec((H_B, DH_B, D_C), lambda b, i: (0, 0, 0)),
            pl.BlockSpec((H_B, BIAS_ROWS, LANES), lambda b, i: (0, 0, 0)),
        ],
        out_specs=pl.BlockSpec((None, QB, H_B * DH_B), lambda b, i: (b, i, 0)),
        out_shape=jax.ShapeDtypeStruct((bsz, tp, H_B * DH_B), BF16),
        scratch_shapes=[
            pltpu.VMEM((n_kb_all, KB, D_C), BF16),
            pltpu.VMEM((n_kb_all, D_C, KB), BF16),
            pltpu.VMEM((2, n_kb_all, KB, LANES), BF16),
            pltpu.VMEM((n_kb_all + 1, KB, QB), I32),
            pltpu.VMEM((n_kb_all + 1, KB, QB), I32),
            pltpu.VMEM((n_kb_all + 1, KB, QB), I32),
            pltpu.VMEM((H_B * QB, D_C), BF16),
            pltpu.VMEM((KB, H_B * QB), F32),
            pltpu.VMEM((KB, H_B * QB), F32),
            pltpu.VMEM((D_C, H_B * QB), F32),
            pltpu.VMEM((H_B, QB), F32),
            pltpu.VMEM((H_B, QB), F32),
            pltpu.VMEM((1, QB), I32),
        ],
        compiler_params=_cparams(("parallel", "arbitrary"), VMEM_MIXER),
        name="dsa_attention",
    )(proj, proj, proj, proj, proj, proj, kv_gain.reshape(1, D_C).astype(F32), wuk_t, wuv_t, bias_tab)


def _hgrn_constants():
    n = HC
    t = np.arange(n)[:, None]
    j = np.arange(n)[None, :]
    mats = [(j <= t)]
    for lvl in range(FINE_LEVELS):
        upper = ((t >> lvl) & 1) == 1
        m_up = (t >> lvl) << lvl
        m_lo = ((t >> lvl) + 1) << lvl
        mats.append(np.where(upper, (j >= m_up) & (j <= t), (j > t) & (j < m_lo)))
    m = np.concatenate(mats, axis=0).astype(np.float32)
    mcat = np.concatenate([m, m], axis=1)
    x = t ^ j
    lev = np.where(j < t, np.floor(np.log2(np.maximum(x, 1))).astype(np.int32), np.where(j == t, -1, -2))
    masks = np.stack([(lev == lvl) for lvl in range(-1, N_LEVELS)]).astype(np.float32)
    return mcat, masks


HGRN_HEADS_PER_STEP = 4


def _role_select(qf, k, lvl, up_small):
    half = 1 << lvl
    if half >= SUBLANES:
        pieces = []
        for blk in range(HC // (2 * half)):
            lo = 2 * half * blk
            pieces += [k[lo:lo + half], qf[lo + half:lo + 2 * half]]
        return jnp.concatenate(pieces, axis=0)
    g, w = HC // SUBLANES, qf.shape[-1]
    sel = jnp.where(up_small[lvl], qf.reshape(g, SUBLANES, w), k.reshape(g, SUBLANES, w))
    return sel.reshape(HC, w)


_SLOT_QE, _SLOT_KT, _SLOT_QB, _SLOT_KB, _SLOT_Y0 = 0, 1, 2, 3, 4
_N_SLOTS = _SLOT_Y0 + N_LEVELS


def _hgrn_kernel(q_ref, fz_ref, i_ref, g_ref, lbp_ref, ng_ref, mcat_ref, mask_ref, o_ref,
                 s_ref, stage_a, stage_b, dec_a, dec_b, *, n_steps, nh):
    wide = nh * DK_C
    row8 = lax.broadcasted_iota(I32, (1, SUBLANES, wide), 1)
    up_small = [((row8 >> lvl) & 1) == 1 for lvl in range(3)]
    s_ref[...] = jnp.zeros(s_ref.shape, F32)

    def gate_stage(c):
        rows = pl.ds(pl.multiple_of(c * HC, HC), HC)
        lb = lbp_ref[0:1, :]
        one_m_lb = lbp_ref[1:2, :]
        qf = _silu(q_ref[rows, :].astype(F32))
        z = fz_ref[rows, :].astype(F32)
        w = one_m_lb * (0.5 + 0.5 * jnp.tanh(0.5 * z))
        k = one_m_lb - w
        lf2 = jnp.log(lb + w) * LOG2_E
        hi = lf2.astype(BF16)
        lo = (lf2 - hi.astype(F32)).astype(BF16)
        return qf, k, _dot(mcat_ref[...], jnp.concatenate([hi, lo], axis=0))

    def operand_stage(gates, stage, dec_ref):
        qf, k, gsum = gates
        b = gsum[0:HC]
        e_b = jnp.exp2(b)
        e_s = jnp.exp2(b[HC - 1:HC, :] - b)
        stage[_SLOT_QE] = (qf * e_b).astype(BF16)
        stage[_SLOT_KT] = (k * e_s).astype(BF16)
        stage[_SLOT_QB] = qf.astype(BF16)
        stage[_SLOT_KB] = k.astype(BF16)
        for lvl in range(N_LEVELS):
            if lvl < FINE_LEVELS:
                expo = gsum[(1 + lvl) * HC:(2 + lvl) * HC]
            else:
                half, pieces = 1 << lvl, []
                for lo in range(0, HC, 2 << lvl):
                    ref = b[lo + half - 1:lo + half, :]
                    pieces += [ref - b[lo:lo + half], b[lo + half:lo + 2 * half] - ref]
                expo = jnp.concatenate(pieces, axis=0)
            y = _role_select(qf, k, lvl, up_small) * jnp.exp2(expo)
            stage[_SLOT_Y0 + lvl] = y.astype(BF16)
        dec_ref[...] = e_b[HC - 1:HC, :]

    def level_stage(c, stage):
        lns = [slice(hh * DK_C, (hh + 1) * DK_C) for hh in range(nh)]
        sts = [s_ref[hh] for hh in range(nh)]
        outs = [_dot(stage[_SLOT_QE, :, ln], st.astype(BF16), _NT) for ln, st in zip(lns, sts)]
        accs = [_dot(stage[_SLOT_QB, :, ln], stage[_SLOT_KB, :, ln], _NT) * mask_ref[0] for ln in lns]
        for lvl in range(N_LEVELS):
            for hh, ln in enumerate(lns):
                y = stage[_SLOT_Y0 + lvl, :, ln]
                accs[hh] = accs[hh] + _dot(y, y, _NT) * mask_ref[lvl + 1]
        return sts, outs, accs

    def output_stage(c, partial, stage, dec_ref):
        sts, outs, accs = partial
        rows = pl.ds(pl.multiple_of(c * HC, HC), HC)
        lns = [slice(hh * DK_C, (hh + 1) * DK_C) for hh in range(nh)]
        vhs = [i_ref[rows, ln] for ln in lns]
        for hh, ln in enumerate(lns):
            vt = vhs[hh].astype(F32).T.astype(BF16)
            s_ref[hh] = sts[hh] * dec_ref[:, ln] + _dot(vt, stage[_SLOT_KT, :, ln])
        normed = []
        for hh in range(nh):
            o = outs[hh] + _dot(accs[hh].astype(BF16), vhs[hh])
            normed.append(o * lax.rsqrt(jnp.mean(o * o, axis=-1, keepdims=True) + RMS_EPS))
        on = jnp.concatenate(normed, axis=1) * ng_ref[...]
        gx = g_ref[rows, :].astype(F32)
        o_ref[rows, :] = (on * _silu(gx)).astype(o_ref.dtype)

    last = n_steps - 1
    operand_stage(gate_stage(0), stage_a, dec_a)

    def pair(j, carry):
        c = 2 * j
        partial = level_stage(c, stage_a)
        gates = gate_stage(c + 1)
        output_stage(c, partial, stage_a, dec_a)
        operand_stage(gates, stage_b, dec_b)
        partial = level_stage(c + 1, stage_b)
        gates = gate_stage(jnp.minimum(c + 2, last))
        output_stage(c + 1, partial, stage_b, dec_b)
        operand_stage(gates, stage_a, dec_a)
        return carry

    lax.fori_loop(0, n_steps // 2, pair, 0)
    if n_steps % 2:
        output_stage(last, level_stage(last, stage_a), stage_a, dec_a)


def hgrn2(proj, lb_params, norm_g):
    bsz, tp, _ = proj.shape
    mcat, masks = _hgrn_constants()
    nh = HGRN_HEADS_PER_STEP
    wide = nh * DK_C
    n_groups = H_C // nh
    col = lambda off: pl.BlockSpec((None, tp, wide), lambda b, h, off=off: (b, 0, off + h))
    return pl.pallas_call(
        functools.partial(_hgrn_kernel, n_steps=tp // HC, nh=nh),
        grid=(bsz, n_groups),
        in_specs=[
            col(0), col(n_groups), col(2 * n_groups), col(3 * n_groups),
            pl.BlockSpec((2, wide), lambda b, h: (0, h)),
            pl.BlockSpec((1, wide), lambda b, h: (0, h)),
            pl.BlockSpec(mcat.shape, lambda b, h: (0, 0)),
            pl.BlockSpec(masks.shape, lambda b, h: (0, 0, 0)),
        ],
        out_specs=pl.BlockSpec((None, tp, wide), lambda b, h: (b, 0, h)),
        out_shape=jax.ShapeDtypeStruct((bsz, tp, H_C * DK_C), BF16),
        scratch_shapes=[
            pltpu.VMEM((nh, DK_C, DK_C), F32),
            pltpu.VMEM((_N_SLOTS, HC, wide), BF16),
            pltpu.VMEM((_N_SLOTS, HC, wide), BF16),
            pltpu.VMEM((1, wide), F32),
            pltpu.VMEM((1, wide), F32),
        ],
        compiler_params=_cparams(("parallel", "parallel"), VMEM_MIXER),
        name="hgrn2",
    )(proj, proj, proj, proj, lb_params, norm_g.reshape(1, -1).astype(F32),
      jnp.asarray(mcat, BF16), jnp.asarray(masks, F32))


def _t5_bucket(rel):
    nb = N_BUCKETS // 2
    ret = jnp.where(rel > 0, nb, 0)
    n = jnp.abs(rel)
    max_exact = nb // 2
    nf = jnp.maximum(n, 1).astype(F32)
    large = max_exact + (jnp.log(nf / max_exact) / math.log(MAX_DISTANCE / max_exact)
                         * (nb - max_exact)).astype(I32)
    large = jnp.minimum(large, nb - 1)
    return ret + jnp.where(n < max_exact, n, large)


def _bias_table(rel_bias_table):
    v = jnp.arange(BIAS_ROWS, dtype=I32)[:, None]
    q = jnp.arange(LANES, dtype=I32)[None, :]
    onehot = jax.nn.one_hot(_t5_bucket(v - BIAS_ORIGIN - q), N_BUCKETS, dtype=F32)
    tab = jnp.einsum("vqb,bh->hvq", onehot, rel_bias_table.astype(F32), precision=lax.Precision.HIGHEST)
    return tab * LOG2_E


def _even_in_weight(w):
    glu_v, glu_g, gate_a, q, c, gate_b, qi, ki, wi = jnp.split(
        w, np.cumsum([D_A, D_A, D_A, H_B * DH_B, D_C, H_B * DH_B, H_IDX * D_IDX, D_IDX])[:8].tolist(), axis=1)
    pad = jnp.zeros((w.shape[0], E_NPAD - E_COL_WI - H_IDX), w.dtype)
    cols = [glu_v, glu_g, gate_a, q, gate_b, qi * (D_IDX ** -0.5), c, ki, ki, wi * (H_IDX ** -0.5), pad]
    return jnp.concatenate([col.astype(BF16) for col in cols], axis=1)


def kernel(x, meta_tokens, norm_gain, final_norm_gain, rel_bias_table, w_in_even, conv_w, conv_b,
           conv_ln_gain, conv_ln_bias, kv_norm_gain, w_uk, w_uv, w_out_even, w_in_odd, lb_logits,
           rec_norm_gain, w_out_odd):
    bsz, seq, d = x.shape
    assert d == D_MODEL and seq % LANES == 0
    depth = norm_gain.shape[0]
    tp = FRONT_PAD + N_META + seq
    topk = min(TOPK_MAX, seq // 4)
    tt_mm = tp // 2 if (tp // 2) % 16 == 0 else tp
    tt_out = tp // 4 if (tp // 4) % 16 == 0 else tp
    tt_last = seq // 4
    tt_conv = tp // 8 if (tp // 8) % CONV_RB == 0 else tp // 2 if (tp // 2) % CONV_RB == 0 else tp
    h, hn = embed(x, meta_tokens, norm_gain[0], tt=tt_out)

    lb_soft = jax.nn.softmax(lb_logits.astype(F32), axis=0)
    lower_bounds = jnp.cumsum(lb_soft, axis=0) - lb_soft[0]
    bias_tab = _bias_table(rel_bias_table)

    for layer in range(depth):
        last = layer == depth - 1
        if layer % 2 == 0:
            e = layer // 2
            w_in, tn = _even_in_weight(w_in_even[e]), E_NPAD // 2
        else:
            o = layer // 2
            w_in, tn = w_in_odd[o].astype(BF16), 2048
        proj = matmul(hn, w_in, tt=tt_mm, tn=tn)
        if layer % 2 == 0:
            a_out = conv_module(proj, conv_w[e], conv_b[e], conv_ln_gain[e], conv_ln_bias[e], tt=tt_conv)
            wuk_t = jnp.transpose(w_uk[e], (0, 2, 1)).astype(BF16)
            wuv_t = jnp.transpose(w_uv[e], (0, 2, 1)).astype(BF16)
            b_out = dsa_attention(proj, kv_norm_gain[e], wuk_t, wuv_t, bias_tab, topk=topk)
            mixed, w_out = [a_out, b_out], w_out_even[e].astype(BF16)
        else:
            lb = lower_bounds[layer]
            mixed = [hgrn2(proj, jnp.stack([lb, 1.0 - lb], axis=0), rec_norm_gain[o])]
            w_out = w_out_odd[o].astype(BF16)
        if last:
            return out_proj(mixed, w_out, h, final_norm_gain, tt=tt_last, final=True)
        h, hn = out_proj(mixed, w_out, h, norm_gain[layer + 1], tt=tt_out, final=False)
```

```python
import functools
import math

import numpy as np
import jax
import jax.numpy as jnp
from jax import lax
from jax.experimental import pallas as pl
from jax.experimental.pallas import tpu as pltpu

F32 = jnp.float32
BF16 = jnp.bfloat16
I32 = jnp.int32

D_MODEL = 2048
CHUNK_LOG2 = 6
N_META = 16
D_A = 1024
CONV_WIDTH = 31
H_B = 8
DH_B = 128
D_C = 256
H_IDX = 16
D_IDX = 64
TOPK_MAX = 256
N_BUCKETS = 32
MAX_DISTANCE = 128
H_C = 16
DK_C = 128
RMS_EPS = 1e-6
LN_EPS = 1e-5
NEG_INF = -1e30
INT_MIN = -(2 ** 31)
LOG2_E = 1.4426950408889634

I16_MIN = -(2 ** 15)

LANES = 128
SUBLANES = 8
MXU_N = 256

FRONT_PAD = LANES - N_META
QB = LANES
KB = MXU_N
HC = 128
N_LEVELS = 7

E_COL_Q = 3 * D_A
E_COL_GB = E_COL_Q + H_B * DH_B
E_COL_QI = E_COL_GB + H_B * DH_B
E_COL_C = E_COL_QI + H_IDX * D_IDX
E_COL_KK = E_COL_C + D_C
E_COL_WI = E_COL_KK + 2 * D_IDX
E_NPAD = E_COL_WI + LANES
assert 2 * D_IDX == LANES and E_NPAD % MXU_N == 0

MIB = 1024 * 1024
VMEM_PROJ = 56
VMEM_MIXER = 40


def _cparams(sem, vmem_mib):
    return pltpu.CompilerParams(dimension_semantics=sem, vmem_limit_bytes=vmem_mib * MIB)


def _sigmoid(x):
    return 1.0 / (1.0 + jnp.exp(-x))


def _silu(x):
    h = 0.5 * x
    return h + h * jnp.tanh(h)


_NN = (((1,), (0,)), ((), ()))
_NT = (((1,), (1,)), ((), ()))


def _dot(a, b, dims=_NN):
    return lax.dot_general(a, b, dims, preferred_element_type=F32)


def _rms_normed(x, gain):
    ms = jnp.mean(x * x, axis=-1, keepdims=True)
    return x * lax.rsqrt(ms + RMS_EPS) * gain


def _embed_kernel(x_ref, meta_ref, g_ref, h_ref, hn_ref, *, tt):
    first = FRONT_PAD + N_META

    @pl.when(pl.program_id(1) == 0)
    def _():
        d = h_ref.shape[-1]
        meta = meta_ref[...]
        x = x_ref[0, 0:tt - first, :]
        h_ref[0:FRONT_PAD, :] = jnp.zeros((FRONT_PAD, d), F32)
        h_ref[FRONT_PAD:first, :] = meta
        h_ref[first:, :] = x
        hn_ref[0:FRONT_PAD, :] = jnp.zeros((FRONT_PAD, d), BF16)
        hn_ref[FRONT_PAD:first, :] = _rms_normed(meta, g_ref[...]).astype(BF16)
        hn_ref[first:, :] = _rms_normed(x, g_ref[...]).astype(BF16)

    @pl.when(pl.program_id(1) > 0)
    def _():
        x = x_ref[0]
        h_ref[...] = x
        hn_ref[...] = _rms_normed(x, g_ref[...]).astype(BF16)


def embed(x, meta_tokens, gain, *, tt):
    bsz, seq, d = x.shape
    first = FRONT_PAD + N_META
    tp = first + seq
    x_spec = pl.BlockSpec((pl.Element(1), pl.Element(tt), pl.Element(d)),
                          lambda b, j: (b, pl.multiple_of(jnp.maximum(j * tt - first, 0), SUBLANES), 0))
    spec = pl.BlockSpec((None, tt, d), lambda b, j: (b, j, 0))
    return pl.pallas_call(
        functools.partial(_embed_kernel, tt=tt),
        grid=(bsz, tp // tt),
        in_specs=[
            x_spec,
            pl.BlockSpec((N_META, d), lambda b, j: (0, 0)),
            pl.BlockSpec((1, d), lambda b, j: (0, 0)),
        ],
        out_specs=[spec, spec],
        out_shape=[jax.ShapeDtypeStruct((bsz, tp, d), F32), jax.ShapeDtypeStruct((bsz, tp, d), BF16)],
        compiler_params=_cparams(("parallel", "parallel"), VMEM_MIXER),
        name="embed",
    )(x.astype(F32), meta_tokens.astype(F32), gain.reshape(1, d).astype(F32))


def _matmul_kernel(x_ref, w_ref, o_ref):
    o_ref[...] = _dot(x_ref[...], w_ref[...]).astype(o_ref.dtype)


def matmul(hn, w, *, tt, tn):
    bsz, tp, d = hn.shape
    n = w.shape[1]
    return pl.pallas_call(
        _matmul_kernel,
        grid=(bsz, tp // tt, n // tn),
        in_specs=[
            pl.BlockSpec((None, tt, d), lambda b, i, j: (b, i, 0)),
            pl.BlockSpec((d, tn), lambda b, i, j: (0, j)),
        ],
        out_specs=pl.BlockSpec((None, tt, tn), lambda b, i, j: (b, i, j)),
        out_shape=jax.ShapeDtypeStruct((bsz, tp, n), BF16),
        compiler_params=_cparams(("parallel", "parallel", "parallel"), VMEM_PROJ),
        name="in_proj",
    )(hn, w)


def _out_proj_kernel(*refs, n_in, tt, final):
    xs = refs[:n_in]
    w_ref, h_ref, g_ref = refs[n_in], refs[n_in + 1], refs[n_in + 2]
    tile = lambda ref: ref[0] if len(ref.shape) == 3 else ref[...]
    y = None
    off = 0
    for x_ref in xs:
        kdim = x_ref.shape[-1]
        part = _dot(tile(x_ref), w_ref[off:off + kdim, :])
        y = part if y is None else y + part
        off += kdim
    h_new = tile(h_ref) + y
    if final:
        refs[-1][...] = _rms_normed(h_new, g_ref[...])
    else:
        row = pl.program_id(1) * tt + lax.broadcasted_iota(I32, (tt, 1), 0)
        h_new = jnp.where(row >= FRONT_PAD, h_new, 0.0)
        refs[-2][...] = h_new
        refs[-1][...] = _rms_normed(h_new, g_ref[...]).astype(BF16)


def out_proj(xs, w, h, gain, *, tt, final):
    bsz, tp, d = h.shape
    if final:
        first = FRONT_PAD + N_META
        rows_out = tp - first
        row_spec = lambda k: pl.BlockSpec((pl.Element(1), pl.Element(tt), pl.Element(k)),
                                          lambda b, i: (b, pl.multiple_of(first + i * tt, LANES), 0))
    else:
        rows_out = tp
        row_spec = lambda k: pl.BlockSpec((None, tt, k), lambda b, i: (b, i, 0))
    in_specs = [row_spec(x.shape[-1]) for x in xs]
    in_specs += [pl.BlockSpec(w.shape, lambda b, i: (0, 0)), row_spec(d), pl.BlockSpec((1, d), lambda b, i: (0, 0))]
    out_spec = pl.BlockSpec((None, tt, d), lambda b, i: (b, i, 0))
    out_f32 = jax.ShapeDtypeStruct((bsz, rows_out, d), F32)
    return pl.pallas_call(
        functools.partial(_out_proj_kernel, n_in=len(xs), tt=tt, final=final),
        grid=(bsz, rows_out // tt),
        in_specs=in_specs,
        out_specs=out_spec if final else [out_spec, out_spec],
        out_shape=out_f32 if final else [out_f32, jax.ShapeDtypeStruct((bsz, rows_out, d), BF16)],
        compiler_params=_cparams(("parallel", "parallel"), VMEM_PROJ),
        name="out_proj_final" if final else "out_proj",
    )(*xs, w, h, gain.reshape(1, d).astype(F32))


CONV_HALO = 32
CONV_RB = 16
CONV_CHAINS = 4


def _conv_kernel(v_ref, g_ref, ga_ref, w_ref, cb_ref, lg_ref, lb_ref, o_ref, us_ref, cv_ref, *, tt):
    j = pl.program_id(1)
    u_new = v_ref[...].astype(F32) * _sigmoid(g_ref[...].astype(F32))
    span = tt + CONV_HALO - SUBLANES
    n_rb = tt // CONV_RB
    for cb in range(D_A // LANES):
        lanes = slice(cb * LANES, (cb + 1) * LANES)

        @pl.when(j == 0)
        def _(cb=cb):
            us_ref[0, cb, 0:CONV_HALO, :] = jnp.zeros((CONV_HALO, LANES), F32)

        @pl.when(j > 0)
        def _(cb=cb):
            us_ref[0, cb, 0:CONV_HALO, :] = us_ref[0, cb, tt:tt + CONV_HALO, :]

        us_ref[0, cb, CONV_HALO:CONV_HALO + tt, :] = u_new[:, lanes]
        for r in range(1, SUBLANES):
            us_ref[r, cb, 0:span, :] = us_ref[0, cb, r:r + span, :]

        wb = [jnp.broadcast_to(w_ref[k:k + 1, lanes], (CONV_RB, LANES)) for k in range(CONV_WIDTH)]
        bias = jnp.broadcast_to(cb_ref[0:1, lanes], (CONV_RB, LANES))

        def rb_body(rb, carry, cb=cb, lanes=lanes, wb=wb, bias=bias):
            base = pl.multiple_of(rb * CONV_RB, CONV_RB)
            parts = [bias] + [None] * (CONV_CHAINS - 1)
            for k in range(CONV_WIDTH):
                a, r = divmod(k + CONV_HALO - (CONV_WIDTH - 1), SUBLANES)
                term = wb[k] * us_ref[r, cb, pl.ds(base + SUBLANES * a, CONV_RB), :]
                c = k % CONV_CHAINS
                parts[c] = term if parts[c] is None else parts[c] + term
            cv_ref[pl.ds(base, CONV_RB), lanes] = (parts[0] + parts[1]) + (parts[2] + parts[3])
            return carry

        lax.fori_loop(0, n_rb, rb_body, 0)

    x = cv_ref[...]
    mu = jnp.mean(x, axis=-1, keepdims=True)
    xc = x - mu
    var = jnp.mean(xc * xc, axis=-1, keepdims=True)
    y = xc * lax.rsqrt(var + LN_EPS) * lg_ref[...] + lb_ref[...]
    ga = ga_ref[...].astype(F32)
    o_ref[...] = (_silu(y) * _silu(ga)).astype(o_ref.dtype)


def conv_module(proj, conv_w, conv_b, ln_g, ln_b, *, tt):
    bsz, tp, _ = proj.shape
    row = lambda a: a.reshape(1, D_A).astype(F32)
    return pl.pallas_call(
        functools.partial(_conv_kernel, tt=tt),
        grid=(bsz, tp // tt),
        in_specs=[
            pl.BlockSpec((None, tt, D_A), lambda b, j: (b, j, 0)),
            pl.BlockSpec((None, tt, D_A), lambda b, j: (b, j, 1)),
            pl.BlockSpec((None, tt, D_A), lambda b, j: (b, j, 2)),
            pl.BlockSpec((CONV_WIDTH, D_A), lambda b, j: (0, 0)),
            pl.BlockSpec((1, D_A), lambda b, j: (0, 0)),
            pl.BlockSpec((1, D_A), lambda b, j: (0, 0)),
            pl.BlockSpec((1, D_A), lambda b, j: (0, 0)),
        ],
        out_specs=pl.BlockSpec((None, tt, D_A), lambda b, j: (b, j, 0)),
        out_shape=jax.ShapeDtypeStruct((bsz, tp, D_A), BF16),
        scratch_shapes=[
            pltpu.VMEM((SUBLANES, D_A // LANES, tt + CONV_HALO, LANES), F32),
            pltpu.VMEM((tt, D_A), F32),
        ],
        compiler_params=_cparams(("parallel", "arbitrary"), VMEM_MIXER),
        name="conv_module",
    )(proj, proj, proj, conv_w.astype(F32), row(conv_b), row(ln_g), row(ln_b))


BIAS_ORIGIN = KB + QB
BIAS_ROWS = BIAS_ORIGIN + KB


def _attn_kernel(c_ref, kk_ref, wi_ref, q_ref, qi_ref, gb_ref, kvg_ref, wuk_ref, wuvt_ref, bias_ref,
                 o_ref, cn_s, cnt_s, kk_s, st_s, hi_s, lo_s, qlat_s, lg_a, lg_b, acc_s, m_s, l_s,
                 tau_s, *, tp, topk):
    i = pl.program_id(1)
    n_kb = (i + 2) // 2
    n_kb_all = cn_s.shape[0]

    @pl.when(i == 0)
    def _prepare_keys():
        gain = kvg_ref[...]
        for kb in range(n_kb_all):
            lo = kb * KB
            nrow = min(KB, tp - lo)
            c = c_ref[lo:lo + nrow, :].astype(F32)
            cn = c * lax.rsqrt(jnp.mean(c * c, axis=-1, keepdims=True) + RMS_EPS) * gain
            cn_s[kb, 0:nrow, :] = cn.astype(BF16)
            kk = kk_ref[lo:lo + nrow, :]
            lane = lax.broadcasted_iota(I32, kk.shape, 1)
            kk_s[0, kb, 0:nrow, :] = jnp.where(lane < D_IDX, kk, jnp.zeros_like(kk))
            kk_s[1, kb, 0:nrow, :] = jnp.where(lane >= D_IDX, kk, jnp.zeros_like(kk))
            for rb in range(nrow // LANES):
                for cb in range(D_C // LANES):
                    tile = cn[rb * LANES:(rb + 1) * LANES, cb * LANES:(cb + 1) * LANES]
                    cnt_s[kb, cb * LANES:(cb + 1) * LANES, rb * LANES:(rb + 1) * LANES] = tile.T.astype(BF16)
            if nrow < KB:
                cn_s[kb, nrow:KB, :] = jnp.zeros((KB - nrow, D_C), BF16)
                kk_s[:, kb, nrow:KB, :] = jnp.zeros((2, KB - nrow, LANES), BF16)
                cnt_s[kb, :, nrow:KB] = jnp.zeros((D_C, KB - nrow), BF16)

    for h in range(H_B):
        qh = q_ref[:, h * DH_B:(h + 1) * DH_B]
        ql = _dot(qh, wuk_ref[h]) * (DH_B ** -0.5 * LOG2_E)
        qlat_s[h * QB:(h + 1) * QB, :] = ql.astype(BF16)

    wit = wi_ref[...].astype(F32).T

    def score_block(kb):
        k_even = kk_s[0, kb]
        k_odd = kk_s[1, kb]
        acc = jnp.zeros((KB, QB), F32)
        for p in range(H_IDX // 2):
            qp = qi_ref[:, p * LANES:(p + 1) * LANES]
            acc = acc + jnp.maximum(_dot(k_even, qp, _NT), 0.0) * wit[2 * p:2 * p + 1, :]
            acc = acc + jnp.maximum(_dot(k_odd, qp, _NT), 0.0) * wit[2 * p + 1:2 * p + 2, :]
        bits = lax.bitcast_convert_type(acc, I32)
        key = bits ^ ((bits >> 31) & 0x7FFFFFFF)
        s_idx = kb * KB + lax.broadcasted_iota(I32, (KB, QB), 0)
        t_idx = i * QB + lax.broadcasted_iota(I32, (KB, QB), 1)
        adm = (s_idx >= FRONT_PAD) & (jnp.maximum(s_idx >> CHUNK_LOG2, 1) <= jnp.maximum(t_idx >> CHUNK_LOG2, 1))
        key = jnp.where(adm, key, INT_MIN)
        st_s[kb] = key
        hi_s[kb] = key >> 16

    def score_pair(j, carry):
        score_block(2 * j)
        score_block(jnp.minimum(2 * j + 1, n_kb_all - 1))
        return carry

    lax.fori_loop(0, (n_kb + 1) // 2, score_pair, 0)

    def search16(ref, need, n_blocks):
        def bit_body(it, carry):
            t, above = carry
            cand = t + lax.shift_left(jnp.int32(1), jnp.int32(15) - it)
            cand8 = jnp.broadcast_to(cand, (SUBLANES, QB))[None]
            parts = []
            for kb in range(n_blocks):
                below = (ref[kb].reshape(KB // SUBLANES, SUBLANES, QB) - cand8) >> 31
                parts += [below[g] for g in range(KB // SUBLANES)]
            while len(parts) > 1:
                nxt = [parts[g] + parts[g + 1] for g in range(0, len(parts) - 1, 2)]
                parts = nxt + parts[len(parts) - len(parts) % 2:]
            cnt = n_blocks * KB + jnp.sum(parts[0], axis=0, keepdims=True)
            ok = cnt >= need
            return jnp.where(ok, cand, t), jnp.where(ok, above, cnt)

        init = (jnp.full((1, QB), I16_MIN, I32), jnp.zeros((1, QB), I32))
        return lax.fori_loop(0, 16, bit_body, init)

    st_s[n_kb] = jnp.full((KB, QB), INT_MIN, I32)
    hi_s[n_kb] = jnp.full((KB, QB), I16_MIN, I32)
    n_pairs = (n_kb + 1) // 2
    few_keys = N_META + QB * i <= topk

    @pl.when(few_keys)
    def _():
        tau_s[...] = jnp.full((1, QB), INT_MIN, I32)

    for pairs in range(1, (n_kb_all + 1) // 2 + 1):
        @pl.when(jnp.logical_and(n_pairs == pairs, jnp.logical_not(few_keys)))
        def _(n_blocks=2 * pairs):
            tau_hi, n_above = search16(hi_s, jnp.full((1, QB), topk, I32), n_blocks)
            for kb in range(n_blocks):
                key = st_s[kb]
                low = (key & 0xFFFF) + I16_MIN
                lo_s[kb] = jnp.where((key >> 16) == tau_hi, low, I16_MIN)
            tau_lo, _ = search16(lo_s, topk - n_above, n_blocks)
            tau_s[...] = lax.shift_left(tau_hi, jnp.int32(16)) | (tau_lo - I16_MIN)

    tau = jnp.maximum(tau_s[...], INT_MIN + 1)

    m_s[...] = jnp.full(m_s.shape, NEG_INF, F32)
    l_s[...] = jnp.zeros(l_s.shape, F32)
    acc_s[...] = jnp.zeros(acc_s.shape, F32)

    def logits_stage(kb, lg_ref):
        lg_ref[...] = _dot(cn_s[kb], qlat_s[...], _NT)

    def softmax_stage(kb, lg_ref):
        neg = jnp.where(st_s[kb] >= tau, 0.0, NEG_INF)
        v0 = pl.multiple_of(jnp.maximum(KB * kb - QB * i + BIAS_ORIGIN, 0), LANES)
        ps, alphas = [], []
        for h in range(H_B):
            lh = lg_ref[:, h * QB:(h + 1) * QB] + bias_ref[h, pl.ds(v0, KB), :] + neg
            m_old = m_s[h:h + 1, :]
            m_new = jnp.maximum(m_old, jnp.max(lh, axis=0, keepdims=True))
            p = jnp.exp2(lh - m_new)
            alpha = jnp.exp2(m_old - m_new)
            l_s[h:h + 1, :] = alpha * l_s[h:h + 1, :] + jnp.sum(p, axis=0, keepdims=True)
            m_s[h:h + 1, :] = m_new
            ps.append(p.astype(BF16))
            alphas.append(alpha)
        pv = _dot(cnt_s[kb], jnp.concatenate(ps, axis=1))
        acc_s[...] = acc_s[...] * jnp.concatenate(alphas, axis=1) + pv

    logits_stage(0, lg_a)

    def att_pair(j, carry):
        kb = 2 * j
        logits_stage(kb + 1, lg_b)
        softmax_stage(kb, lg_a)
        logits_stage(jnp.minimum(kb + 2, n_kb - 1), lg_a)
        softmax_stage(kb + 1, lg_b)
        return carry

    lax.fori_loop(0, n_kb // 2, att_pair, 0)

    @pl.when(n_kb % 2 == 1)
    def _():
        softmax_stage(n_kb - 1, lg_a)

    for h in range(H_B):
        cols = slice(h * QB, (h + 1) * QB)
        ot = (acc_s[:, cols] * (1.0 / l_s[h:h + 1, :])).astype(BF16)
        bt = _dot(wuvt_ref[h], ot)
        gate = gb_ref[:, cols].astype(F32)
        o_ref[:, cols] = (bt.T * _silu(gate)).astype(o_ref.dtype)


def dsa_attention(proj, kv_gain, wuk_t, wuv_t, bias_tab, *, topk):
    bsz, tp, _ = proj.shape
    n_qb = tp // QB
    n_kb_all = -(-tp // KB)
    return pl.pallas_call(
        functools.partial(_attn_kernel, tp=tp, topk=topk),
        grid=(bsz, n_qb),
        in_specs=[
            pl.BlockSpec((None, tp, D_C), lambda b, i: (b, 0, E_COL_C // D_C)),
            pl.BlockSpec((None, tp, LANES), lambda b, i: (b, 0, E_COL_KK // LANES)),
            pl.BlockSpec((None, QB, LANES), lambda b, i: (b, i, E_COL_WI // LANES)),
            pl.BlockSpec((None, QB, H_B * DH_B), lambda b, i: (b, i, E_COL_Q // (H_B * DH_B))),
            pl.BlockSpec((None, QB, H_IDX * D_IDX), lambda b, i: (b, i, E_COL_QI // (H_IDX * D_IDX))),
            pl.BlockSpec((None, QB, H_B * DH_B), lambda b, i: (b, i, E_COL_GB // (H_B * DH_B))),
            pl.BlockSpec((1, D_C), lambda b, i: (0, 0)),
            pl.BlockSpec((H_B, DH_B, D_C), lambda b, i: (0, 0, 0)),
            pl.BlockSpec((H_B, DH_B, D_C), lambda b, i: (0, 0, 0)),
            pl.BlockSpec((H_B, BIAS_ROWS, LANES), lambda b, i: (0, 0, 0)),
        ],
        out_specs=pl.BlockSpec((None, QB, H_B * DH_B), lambda b, i: (b, i, 0)),
        out_shape=jax.ShapeDtypeStruct((bsz, tp, H_B * DH_B), BF16),
        scratch_shapes=[
            pltpu.VMEM((n_kb_all, KB, D_C), BF16),
            pltpu.VMEM((n_kb_all, D_C, KB), BF16),
            pltpu.VMEM((2, n_kb_all, KB, LANES), BF16),
            pltpu.VMEM((n_kb_all + 1, KB, QB), I32),
            pltpu.VMEM((n_kb_all + 1, KB, QB), I32),
            pltpu.VMEM((n_kb_all + 1, KB, QB), I32),
            pltpu.VMEM((H_B * QB, D_C), BF16),
            pltpu.VMEM((KB, H_B * QB), F32),
            pltpu.VMEM((KB, H_B * QB), F32),
            pltpu.VMEM((D_C, H_B * QB), F32),
            pltpu.VMEM((H_B, QB), F32),
            pltpu.VMEM((H_B, QB), F32),
            pltpu.VMEM((1, QB), I32),
        ],
        compiler_params=_cparams(("parallel", "arbitrary"), VMEM_MIXER),
        name="dsa_attention",
    )(proj, proj, proj, proj, proj, proj, kv_gain.reshape(1, D_C).astype(F32), wuk_t, wuv_t, bias_tab)


def _hgrn_constants():
    n = HC
    t = np.arange(n)[:, None]
    j = np.arange(n)[None, :]
    mats = [(j <= t)]
    for lvl in range(N_LEVELS):
        upper = ((t >> lvl) & 1) == 1
        m_up = (t >> lvl) << lvl
        m_lo = ((t >> lvl) + 1) << lvl
        mats.append(np.where(upper, (j >= m_up) & (j <= t), (j > t) & (j < m_lo)))
    m = np.concatenate(mats, axis=0).astype(np.float32)
    mcat = np.concatenate([m, m], axis=1)
    x = t ^ j
    lev = np.where(j < t, np.floor(np.log2(np.maximum(x, 1))).astype(np.int32), np.where(j == t, -1, -2))
    masks = np.stack([(lev == lvl) for lvl in range(-1, N_LEVELS)]).astype(np.float32)
    return mcat, masks


HGRN_HEADS_PER_STEP = 2


def _role_select(qf, k, lvl, up_small):
    half = 1 << lvl
    if half >= SUBLANES:
        pieces = []
        for blk in range(HC // (2 * half)):
            lo = 2 * half * blk
            pieces += [k[lo:lo + half], qf[lo + half:lo + 2 * half]]
        return jnp.concatenate(pieces, axis=0)
    g, w = HC // SUBLANES, qf.shape[-1]
    sel = jnp.where(up_small[lvl], qf.reshape(g, SUBLANES, w), k.reshape(g, SUBLANES, w))
    return sel.reshape(HC, w)


_SLOT_QE, _SLOT_KT, _SLOT_QB, _SLOT_KB, _SLOT_Y0 = 0, 1, 2, 3, 4
_N_SLOTS = _SLOT_Y0 + N_LEVELS


def _hgrn_kernel(q_ref, fz_ref, i_ref, g_ref, lbp_ref, ng_ref, mcat_ref, mask_ref, o_ref,
                 s_ref, stage_a, stage_b, dec_a, dec_b, *, n_steps, nh):
    wide = nh * DK_C
    row8 = lax.broadcasted_iota(I32, (1, SUBLANES, wide), 1)
    up_small = [((row8 >> lvl) & 1) == 1 for lvl in range(3)]
    s_ref[...] = jnp.zeros(s_ref.shape, F32)

    def gate_stage(c):
        rows = pl.ds(pl.multiple_of(c * HC, HC), HC)
        lb = lbp_ref[0:1, :]
        one_m_lb = lbp_ref[1:2, :]
        qf = _silu(q_ref[rows, :].astype(F32))
        z = fz_ref[rows, :].astype(F32)
        w = one_m_lb * (0.5 + 0.5 * jnp.tanh(0.5 * z))
        k = one_m_lb - w
        lf2 = jnp.log(lb + w) * LOG2_E
        hi = lf2.astype(BF16)
        lo = (lf2 - hi.astype(F32)).astype(BF16)
        return qf, k, _dot(mcat_ref[...], jnp.concatenate([hi, lo], axis=0))

    def operand_stage(gates, stage, dec_ref):
        qf, k, gsum = gates
        b = gsum[0:HC]
        e_b = jnp.exp2(b)
        e_s = jnp.exp2(b[HC - 1:HC, :] - b)
        stage[_SLOT_QE] = (qf * e_b).astype(BF16)
        stage[_SLOT_KT] = (k * e_s).astype(BF16)
        stage[_SLOT_QB] = qf.astype(BF16)
        stage[_SLOT_KB] = k.astype(BF16)
        for lvl in range(N_LEVELS):
            y = _role_select(qf, k, lvl, up_small) * jnp.exp2(gsum[(1 + lvl) * HC:(2 + lvl) * HC])
            stage[_SLOT_Y0 + lvl] = y.astype(BF16)
        dec_ref[...] = e_b[HC - 1:HC, :]

    def level_stage(c, stage):
        lns = [slice(hh * DK_C, (hh + 1) * DK_C) for hh in range(nh)]
        sts = [s_ref[hh] for hh in range(nh)]
        outs = [_dot(stage[_SLOT_QE, :, ln], st.astype(BF16), _NT) for ln, st in zip(lns, sts)]
        accs = [_dot(stage[_SLOT_QB, :, ln], stage[_SLOT_KB, :, ln], _NT) * mask_ref[0] for ln in lns]
        for lvl in range(N_LEVELS):
            for hh, ln in enumerate(lns):
                y = stage[_SLOT_Y0 + lvl, :, ln]
                accs[hh] = accs[hh] + _dot(y, y, _NT) * mask_ref[lvl + 1]
        return sts, outs, accs

    def output_stage(c, partial, stage, dec_ref):
        sts, outs, accs = partial
        rows = pl.ds(pl.multiple_of(c * HC, HC), HC)
        lns = [slice(hh * DK_C, (hh + 1) * DK_C) for hh in range(nh)]
        vhs = [i_ref[rows, ln] for ln in lns]
        for hh, ln in enumerate(lns):
            vt = vhs[hh].astype(F32).T.astype(BF16)
            s_ref[hh] = sts[hh] * dec_ref[:, ln] + _dot(vt, stage[_SLOT_KT, :, ln])
        normed = []
        for hh in range(nh):
            o = outs[hh] + _dot(accs[hh].astype(BF16), vhs[hh])
            normed.append(o * lax.rsqrt(jnp.mean(o * o, axis=-1, keepdims=True) + RMS_EPS))
        on = jnp.concatenate(normed, axis=1) * ng_ref[...]
        gx = g_ref[rows, :].astype(F32)
        o_ref[rows, :] = (on * _silu(gx)).astype(o_ref.dtype)

    last = n_steps - 1
    operand_stage(gate_stage(0), stage_a, dec_a)

    def pair(j, carry):
        c = 2 * j
        partial = level_stage(c, stage_a)
        operand_stage(gate_stage(c + 1), stage_b, dec_b)
        output_stage(c, partial, stage_a, dec_a)
        partial = level_stage(c + 1, stage_b)
        operand_stage(gate_stage(jnp.minimum(c + 2, last)), stage_a, dec_a)
        output_stage(c + 1, partial, stage_b, dec_b)
        return carry

    lax.fori_loop(0, n_steps // 2, pair, 0)
    if n_steps % 2:
        output_stage(last, level_stage(last, stage_a), stage_a, dec_a)


def hgrn2(proj, lb_params, norm_g):
    bsz, tp, _ = proj.shape
    mcat, masks = _hgrn_constants()
    nh = HGRN_HEADS_PER_STEP
    wide = nh * DK_C
    n_groups = H_C // nh
    col = lambda off: pl.BlockSpec((None, tp, wide), lambda b, h, off=off: (b, 0, off + h))
    return pl.pallas_call(
        functools.partial(_hgrn_kernel, n_steps=tp // HC, nh=nh),
        grid=(bsz, n_groups),
        in_specs=[
            col(0), col(n_groups), col(2 * n_groups), col(3 * n_groups),
            pl.BlockSpec((2, wide), lambda b, h: (0, h)),
            pl.BlockSpec((1, wide), lambda b, h: (0, h)),
            pl.BlockSpec(mcat.shape, lambda b, h: (0, 0)),
            pl.BlockSpec(masks.shape, lambda b, h: (0, 0, 0)),
        ],
        out_specs=pl.BlockSpec((None, tp, wide), lambda b, h: (b, 0, h)),
        out_shape=jax.ShapeDtypeStruct((bsz, tp, H_C * DK_C), BF16),
        scratch_shapes=[
            pltpu.VMEM((nh, DK_C, DK_C), F32),
            pltpu.VMEM((_N_SLOTS, HC, wide), BF16),
            pltpu.VMEM((_N_SLOTS, HC, wide), BF16),
            pltpu.VMEM((1, wide), F32),
            pltpu.VMEM((1, wide), F32),
        ],
        compiler_params=_cparams(("parallel", "parallel"), VMEM_MIXER),
        name="hgrn2",
    )(proj, proj, proj, proj, lb_params, norm_g.reshape(1, -1).astype(F32),
      jnp.asarray(mcat, BF16), jnp.asarray(masks, F32))


def _t5_bucket(rel):
    nb = N_BUCKETS // 2
    ret = jnp.where(rel > 0, nb, 0)
    n = jnp.abs(rel)
    max_exact = nb // 2
    nf = jnp.maximum(n, 1).astype(F32)
    large = max_exact + (jnp.log(nf / max_exact) / math.log(MAX_DISTANCE / max_exact)
                         * (nb - max_exact)).astype(I32)
    large = jnp.minimum(large, nb - 1)
    return ret + jnp.where(n < max_exact, n, large)


def _bias_table(rel_bias_table):
    v = jnp.arange(BIAS_ROWS, dtype=I32)[:, None]
    q = jnp.arange(LANES, dtype=I32)[None, :]
    onehot = jax.nn.one_hot(_t5_bucket(v - BIAS_ORIGIN - q), N_BUCKETS, dtype=F32)
    tab = jnp.einsum("vqb,bh->hvq", onehot, rel_bias_table.astype(F32), precision=lax.Precision.HIGHEST)
    return tab * LOG2_E


def _even_in_weight(w):
    glu_v, glu_g, gate_a, q, c, gate_b, qi, ki, wi = jnp.split(
        w, np.cumsum([D_A, D_A, D_A, H_B * DH_B, D_C, H_B * DH_B, H_IDX * D_IDX, D_IDX])[:8].tolist(), axis=1)
    pad = jnp.zeros((w.shape[0], E_NPAD - E_COL_WI - H_IDX), w.dtype)
    cols = [glu_v, glu_g, gate_a, q, gate_b, qi * (D_IDX ** -0.5), c, ki, ki, wi * (H_IDX ** -0.5), pad]
    return jnp.concatenate([col.astype(BF16) for col in cols], axis=1)


def kernel(x, meta_tokens, norm_gain, final_norm_gain, rel_bias_table, w_in_even, conv_w, conv_b,
           conv_ln_gain, conv_ln_bias, kv_norm_gain, w_uk, w_uv, w_out_even, w_in_odd, lb_logits,
           rec_norm_gain, w_out_odd):
    bsz, seq, d = x.shape
    assert d == D_MODEL and seq % LANES == 0
    depth = norm_gain.shape[0]
    tp = FRONT_PAD + N_META + seq
    topk = min(TOPK_MAX, seq // 4)
    tt_mm = tp // 2 if (tp // 2) % 16 == 0 else tp
    tt_out = tp // 4 if (tp // 4) % 16 == 0 else tp
    tt_last = seq // 4
    tt_conv = tp // 8 if (tp // 8) % CONV_RB == 0 else tp // 2 if (tp // 2) % CONV_RB == 0 else tp
    h, hn = embed(x, meta_tokens, norm_gain[0], tt=tt_out)

    lb_soft = jax.nn.softmax(lb_logits.astype(F32), axis=0)
    lower_bounds = jnp.cumsum(lb_soft, axis=0) - lb_soft[0]
    bias_tab = _bias_table(rel_bias_table)

    for layer in range(depth):
        last = layer == depth - 1
        if layer % 2 == 0:
            e = layer // 2
            w_in, tn = _even_in_weight(w_in_even[e]), E_NPAD // 2
        else:
            o = layer // 2
            w_in, tn = w_in_odd[o].astype(BF16), 2048
        proj = matmul(hn, w_in, tt=tt_mm, tn=tn)
        if layer % 2 == 0:
            a_out = conv_module(proj, conv_w[e], conv_b[e], conv_ln_gain[e], conv_ln_bias[e], tt=tt_conv)
            wuk_t = jnp.transpose(w_uk[e], (0, 2, 1)).astype(BF16)
            wuv_t = jnp.transpose(w_uv[e], (0, 2, 1)).astype(BF16)
            b_out = dsa_attention(proj, kv_norm_gain[e], wuk_t, wuv_t, bias_tab, topk=topk)
            mixed, w_out = [a_out, b_out], w_out_even[e].astype(BF16)
        else:
            lb = lower_bounds[layer]
            mixed = [hgrn2(proj, jnp.stack([lb, 1.0 - lb], axis=0), rec_norm_gain[o])]
            w_out = w_out_odd[o].astype(BF16)
        if last:
            return out_proj(mixed, w_out, h, final_norm_gain, tt=tt_last, final=True)
        h, hn = out_proj(mixed, w_out, h, norm_gain[layer + 1], tt=tt_out, final=False)
```

```python
import functools
import math

import numpy as np
import jax
import jax.numpy as jnp
from jax import lax
from jax.experimental import pallas as pl
from jax.experimental.pallas import tpu as pltpu

F32 = jnp.float32
BF16 = jnp.bfloat16
I32 = jnp.int32

D_MODEL = 2048
CHUNK_LOG2 = 6
N_META = 16
D_A = 1024
CONV_WIDTH = 31
H_B = 8
DH_B = 128
D_C = 256
H_IDX = 16
D_IDX = 64
TOPK_MAX = 256
N_BUCKETS = 32
MAX_DISTANCE = 128
H_C = 16
DK_C = 128
RMS_EPS = 1e-6
LN_EPS = 1e-5
NEG_INF = -1e30
INT_MIN = -(2 ** 31)
LOG2_E = 1.4426950408889634

I16_MIN = -(2 ** 15)

LANES = 128
SUBLANES = 8
MXU_N = 256

FRONT_PAD = LANES - N_META
QB = LANES
KB = MXU_N
HC = 128
N_LEVELS = 7

E_COL_Q = 3 * D_A
E_COL_GB = E_COL_Q + H_B * DH_B
E_COL_QI = E_COL_GB + H_B * DH_B
E_COL_C = E_COL_QI + H_IDX * D_IDX
E_COL_KK = E_COL_C + D_C
E_COL_WI = E_COL_KK + 2 * D_IDX
E_NPAD = E_COL_WI + LANES
assert 2 * D_IDX == LANES and E_NPAD % MXU_N == 0

MIB = 1024 * 1024
VMEM_PROJ = 56
VMEM_MIXER = 40


def _cparams(sem, vmem_mib):
    return pltpu.CompilerParams(dimension_semantics=sem, vmem_limit_bytes=vmem_mib * MIB)


def _sigmoid(x):
    return 1.0 / (1.0 + jnp.exp(-x))


def _silu(x):
    h = 0.5 * x
    return h + h * jnp.tanh(h)


_NN = (((1,), (0,)), ((), ()))
_NT = (((1,), (1,)), ((), ()))


def _dot(a, b, dims=_NN):
    return lax.dot_general(a, b, dims, preferred_element_type=F32)


def _rms_normed(x, gain):
    ms = jnp.mean(x * x, axis=-1, keepdims=True)
    return x * lax.rsqrt(ms + RMS_EPS) * gain


def _embed_kernel(x_ref, meta_ref, g_ref, h_ref, hn_ref, *, tt):
    first = FRONT_PAD + N_META

    @pl.when(pl.program_id(1) == 0)
    def _():
        d = h_ref.shape[-1]
        meta = meta_ref[...]
        x = x_ref[0, 0:tt - first, :]
        h_ref[0:FRONT_PAD, :] = jnp.zeros((FRONT_PAD, d), F32)
        h_ref[FRONT_PAD:first, :] = meta
        h_ref[first:, :] = x
        hn_ref[0:FRONT_PAD, :] = jnp.zeros((FRONT_PAD, d), BF16)
        hn_ref[FRONT_PAD:first, :] = _rms_normed(meta, g_ref[...]).astype(BF16)
        hn_ref[first:, :] = _rms_normed(x, g_ref[...]).astype(BF16)

    @pl.when(pl.program_id(1) > 0)
    def _():
        x = x_ref[0]
        h_ref[...] = x
        hn_ref[...] = _rms_normed(x, g_ref[...]).astype(BF16)


def embed(x, meta_tokens, gain, *, tt):
    bsz, seq, d = x.shape
    first = FRONT_PAD + N_META
    tp = first + seq
    x_spec = pl.BlockSpec((pl.Element(1), pl.Element(tt), pl.Element(d)),
                          lambda b, j: (b, pl.multiple_of(jnp.maximum(j * tt - first, 0), SUBLANES), 0))
    spec = pl.BlockSpec((None, tt, d), lambda b, j: (b, j, 0))
    return pl.pallas_call(
        functools.partial(_embed_kernel, tt=tt),
        grid=(bsz, tp // tt),
        in_specs=[
            x_spec,
            pl.BlockSpec((N_META, d), lambda b, j: (0, 0)),
            pl.BlockSpec((1, d), lambda b, j: (0, 0)),
        ],
        out_specs=[spec, spec],
        out_shape=[jax.ShapeDtypeStruct((bsz, tp, d), F32), jax.ShapeDtypeStruct((bsz, tp, d), BF16)],
        compiler_params=_cparams(("parallel", "parallel"), VMEM_MIXER),
        name="embed",
    )(x.astype(F32), meta_tokens.astype(F32), gain.reshape(1, d).astype(F32))


def _matmul_kernel(x_ref, w_ref, o_ref):
    o_ref[...] = _dot(x_ref[...], w_ref[...]).astype(o_ref.dtype)


def matmul(hn, w, *, tt, tn):
    bsz, tp, d = hn.shape
    n = w.shape[1]
    return pl.pallas_call(
        _matmul_kernel,
        grid=(bsz, tp // tt, n // tn),
        in_specs=[
            pl.BlockSpec((None, tt, d), lambda b, i, j: (b, i, 0)),
            pl.BlockSpec((d, tn), lambda b, i, j: (0, j)),
        ],
        out_specs=pl.BlockSpec((None, tt, tn), lambda b, i, j: (b, i, j)),
        out_shape=jax.ShapeDtypeStruct((bsz, tp, n), BF16),
        compiler_params=_cparams(("parallel", "parallel", "parallel"), VMEM_PROJ),
        name="in_proj",
    )(hn, w)


def _out_proj_kernel(*refs, n_in, tt, final):
    xs = refs[:n_in]
    w_ref, h_ref, g_ref = refs[n_in], refs[n_in + 1], refs[n_in + 2]
    tile = lambda ref: ref[0] if len(ref.shape) == 3 else ref[...]
    y = None
    off = 0
    for x_ref in xs:
        kdim = x_ref.shape[-1]
        part = _dot(tile(x_ref), w_ref[off:off + kdim, :])
        y = part if y is None else y + part
        off += kdim
    h_new = tile(h_ref) + y
    if final:
        refs[-1][...] = _rms_normed(h_new, g_ref[...])
    else:
        row = pl.program_id(1) * tt + lax.broadcasted_iota(I32, (tt, 1), 0)
        h_new = jnp.where(row >= FRONT_PAD, h_new, 0.0)
        refs[-2][...] = h_new
        refs[-1][...] = _rms_normed(h_new, g_ref[...]).astype(BF16)


def out_proj(xs, w, h, gain, *, tt, final):
    bsz, tp, d = h.shape
    if final:
        first = FRONT_PAD + N_META
        rows_out = tp - first
        row_spec = lambda k: pl.BlockSpec((pl.Element(1), pl.Element(tt), pl.Element(k)),
                                          lambda b, i: (b, pl.multiple_of(first + i * tt, LANES), 0))
    else:
        rows_out = tp
        row_spec = lambda k: pl.BlockSpec((None, tt, k), lambda b, i: (b, i, 0))
    in_specs = [row_spec(x.shape[-1]) for x in xs]
    in_specs += [pl.BlockSpec(w.shape, lambda b, i: (0, 0)), row_spec(d), pl.BlockSpec((1, d), lambda b, i: (0, 0))]
    out_spec = pl.BlockSpec((None, tt, d), lambda b, i: (b, i, 0))
    out_f32 = jax.ShapeDtypeStruct((bsz, rows_out, d), F32)
    return pl.pallas_call(
        functools.partial(_out_proj_kernel, n_in=len(xs), tt=tt, final=final),
        grid=(bsz, rows_out // tt),
        in_specs=in_specs,
        out_specs=out_spec if final else [out_spec, out_spec],
        out_shape=out_f32 if final else [out_f32, jax.ShapeDtypeStruct((bsz, rows_out, d), BF16)],
        compiler_params=_cparams(("parallel", "parallel"), VMEM_PROJ),
        name="out_proj_final" if final else "out_proj",
    )(*xs, w, h, gain.reshape(1, d).astype(F32))


CONV_HALO = 32
CONV_RB = 16
CONV_CHAINS = 4


def _conv_kernel(v_ref, g_ref, ga_ref, w_ref, cb_ref, lg_ref, lb_ref, o_ref, us_ref, cv_ref, *, tt):
    j = pl.program_id(1)
    u_new = v_ref[...].astype(F32) * _sigmoid(g_ref[...].astype(F32))
    span = tt + CONV_HALO - SUBLANES
    n_rb = tt // CONV_RB
    for cb in range(D_A // LANES):
        lanes = slice(cb * LANES, (cb + 1) * LANES)

        @pl.when(j == 0)
        def _(cb=cb):
            us_ref[0, cb, 0:CONV_HALO, :] = jnp.zeros((CONV_HALO, LANES), F32)

        @pl.when(j > 0)
        def _(cb=cb):
            us_ref[0, cb, 0:CONV_HALO, :] = us_ref[0, cb, tt:tt + CONV_HALO, :]

        us_ref[0, cb, CONV_HALO:CONV_HALO + tt, :] = u_new[:, lanes]
        for r in range(1, SUBLANES):
            us_ref[r, cb, 0:span, :] = us_ref[0, cb, r:r + span, :]

        wb = [jnp.broadcast_to(w_ref[k:k + 1, lanes], (CONV_RB, LANES)) for k in range(CONV_WIDTH)]
        bias = jnp.broadcast_to(cb_ref[0:1, lanes], (CONV_RB, LANES))

        def rb_body(rb, carry, cb=cb, lanes=lanes, wb=wb, bias=bias):
            base = pl.multiple_of(rb * CONV_RB, CONV_RB)
            parts = [bias] + [None] * (CONV_CHAINS - 1)
            for k in range(CONV_WIDTH):
                a, r = divmod(k + CONV_HALO - (CONV_WIDTH - 1), SUBLANES)
                term = wb[k] * us_ref[r, cb, pl.ds(base + SUBLANES * a, CONV_RB), :]
                c = k % CONV_CHAINS
                parts[c] = term if parts[c] is None else parts[c] + term
            cv_ref[pl.ds(base, CONV_RB), lanes] = (parts[0] + parts[1]) + (parts[2] + parts[3])
            return carry

        lax.fori_loop(0, n_rb, rb_body, 0)

    x = cv_ref[...]
    mu = jnp.mean(x, axis=-1, keepdims=True)
    xc = x - mu
    var = jnp.mean(xc * xc, axis=-1, keepdims=True)
    y = xc * lax.rsqrt(var + LN_EPS) * lg_ref[...] + lb_ref[...]
    ga = ga_ref[...].astype(F32)
    o_ref[...] = (_silu(y) * _silu(ga)).astype(o_ref.dtype)


def conv_module(proj, conv_w, conv_b, ln_g, ln_b, *, tt):
    bsz, tp, _ = proj.shape
    row = lambda a: a.reshape(1, D_A).astype(F32)
    return pl.pallas_call(
        functools.partial(_conv_kernel, tt=tt),
        grid=(bsz, tp // tt),
        in_specs=[
            pl.BlockSpec((None, tt, D_A), lambda b, j: (b, j, 0)),
            pl.BlockSpec((None, tt, D_A), lambda b, j: (b, j, 1)),
            pl.BlockSpec((None, tt, D_A), lambda b, j: (b, j, 2)),
            pl.BlockSpec((CONV_WIDTH, D_A), lambda b, j: (0, 0)),
            pl.BlockSpec((1, D_A), lambda b, j: (0, 0)),
            pl.BlockSpec((1, D_A), lambda b, j: (0, 0)),
            pl.BlockSpec((1, D_A), lambda b, j: (0, 0)),
        ],
        out_specs=pl.BlockSpec((None, tt, D_A), lambda b, j: (b, j, 0)),
        out_shape=jax.ShapeDtypeStruct((bsz, tp, D_A), BF16),
        scratch_shapes=[
            pltpu.VMEM((SUBLANES, D_A // LANES, tt + CONV_HALO, LANES), F32),
            pltpu.VMEM((tt, D_A), F32),
        ],
        compiler_params=_cparams(("parallel", "arbitrary"), VMEM_MIXER),
        name="conv_module",
    )(proj, proj, proj, conv_w.astype(F32), row(conv_b), row(ln_g), row(ln_b))


BIAS_ORIGIN = KB + QB
BIAS_ROWS = BIAS_ORIGIN + KB


def _attn_kernel(c_ref, kk_ref, wi_ref, q_ref, qi_ref, gb_ref, kvg_ref, wuk_ref, wuvt_ref, bias_ref,
                 o_ref, cn_s, cnt_s, kk_s, st_s, hi_s, lo_s, qlat_s, lg_a, lg_b, acc_s, m_s, l_s,
                 tau_s, *, tp, topk):
    i = pl.program_id(1)
    n_kb = (i + 2) // 2
    n_kb_all = cn_s.shape[0]

    @pl.when(i == 0)
    def _prepare_keys():
        gain = kvg_ref[...]
        for kb in range(n_kb_all):
            lo = kb * KB
            nrow = min(KB, tp - lo)
            c = c_ref[lo:lo + nrow, :].astype(F32)
            cn = c * lax.rsqrt(jnp.mean(c * c, axis=-1, keepdims=True) + RMS_EPS) * gain
            cn_s[kb, 0:nrow, :] = cn.astype(BF16)
            kk = kk_ref[lo:lo + nrow, :]
            lane = lax.broadcasted_iota(I32, kk.shape, 1)
            kk_s[0, kb, 0:nrow, :] = jnp.where(lane < D_IDX, kk, jnp.zeros_like(kk))
            kk_s[1, kb, 0:nrow, :] = jnp.where(lane >= D_IDX, kk, jnp.zeros_like(kk))
            for rb in range(nrow // LANES):
                for cb in range(D_C // LANES):
                    tile = cn[rb * LANES:(rb + 1) * LANES, cb * LANES:(cb + 1) * LANES]
                    cnt_s[kb, cb * LANES:(cb + 1) * LANES, rb * LANES:(rb + 1) * LANES] = tile.T.astype(BF16)
            if nrow < KB:
                cn_s[kb, nrow:KB, :] = jnp.zeros((KB - nrow, D_C), BF16)
                kk_s[:, kb, nrow:KB, :] = jnp.zeros((2, KB - nrow, LANES), BF16)
                cnt_s[kb, :, nrow:KB] = jnp.zeros((D_C, KB - nrow), BF16)

    for h in range(H_B):
        qh = q_ref[:, h * DH_B:(h + 1) * DH_B]
        ql = _dot(qh, wuk_ref[h]) * (DH_B ** -0.5 * LOG2_E)
        qlat_s[h * QB:(h + 1) * QB, :] = ql.astype(BF16)

    wit = wi_ref[...].astype(F32).T

    def score_block(kb):
        k_even = kk_s[0, kb]
        k_odd = kk_s[1, kb]
        acc = jnp.zeros((KB, QB), F32)
        for p in range(H_IDX // 2):
            qp = qi_ref[:, p * LANES:(p + 1) * LANES]
            acc = acc + jnp.maximum(_dot(k_even, qp, _NT), 0.0) * wit[2 * p:2 * p + 1, :]
            acc = acc + jnp.maximum(_dot(k_odd, qp, _NT), 0.0) * wit[2 * p + 1:2 * p + 2, :]
        bits = lax.bitcast_convert_type(acc, I32)
        key = bits ^ ((bits >> 31) & 0x7FFFFFFF)
        s_idx = kb * KB + lax.broadcasted_iota(I32, (KB, QB), 0)
        t_idx = i * QB + lax.broadcasted_iota(I32, (KB, QB), 1)
        adm = (s_idx >= FRONT_PAD) & (jnp.maximum(s_idx >> CHUNK_LOG2, 1) <= jnp.maximum(t_idx >> CHUNK_LOG2, 1))
        key = jnp.where(adm, key, INT_MIN)
        st_s[kb] = key
        hi_s[kb] = key >> 16

    def score_pair(j, carry):
        score_block(2 * j)
        score_block(jnp.minimum(2 * j + 1, n_kb_all - 1))
        return carry

    lax.fori_loop(0, (n_kb + 1) // 2, score_pair, 0)

    def search16(ref, need, n_blocks):
        def bit_body(it, carry):
            t, above = carry
            cand = t + lax.shift_left(jnp.int32(1), jnp.int32(15) - it)
            cand8 = jnp.broadcast_to(cand, (SUBLANES, QB))[None]
            parts = []
            for kb in range(n_blocks):
                below = (ref[kb].reshape(KB // SUBLANES, SUBLANES, QB) - cand8) >> 31
                parts += [below[g] for g in range(KB // SUBLANES)]
            while len(parts) > 1:
                nxt = [parts[g] + parts[g + 1] for g in range(0, len(parts) - 1, 2)]
                parts = nxt + parts[len(parts) - len(parts) % 2:]
            cnt = n_blocks * KB + jnp.sum(parts[0], axis=0, keepdims=True)
            ok = cnt >= need
            return jnp.where(ok, cand, t), jnp.where(ok, above, cnt)

        init = (jnp.full((1, QB), I16_MIN, I32), jnp.zeros((1, QB), I32))
        return lax.fori_loop(0, 16, bit_body, init)

    st_s[n_kb] = jnp.full((KB, QB), INT_MIN, I32)
    hi_s[n_kb] = jnp.full((KB, QB), I16_MIN, I32)
    n_pairs = (n_kb + 1) // 2
    few_keys = N_META + QB * i <= topk

    @pl.when(few_keys)
    def _():
        tau_s[...] = jnp.full((1, QB), INT_MIN, I32)

    for pairs in range(1, (n_kb_all + 1) // 2 + 1):
        @pl.when(jnp.logical_and(n_pairs == pairs, jnp.logical_not(few_keys)))
        def _(n_blocks=2 * pairs):
            tau_hi, n_above = search16(hi_s, jnp.full((1, QB), topk, I32), n_blocks)
            for kb in range(n_blocks):
                key = st_s[kb]
                low = (key & 0xFFFF) + I16_MIN
                lo_s[kb] = jnp.where((key >> 16) == tau_hi, low, I16_MIN)
            tau_lo, _ = search16(lo_s, topk - n_above, n_blocks)
            tau_s[...] = lax.shift_left(tau_hi, jnp.int32(16)) | (tau_lo - I16_MIN)

    tau = jnp.maximum(tau_s[...], INT_MIN + 1)

    m_s[...] = jnp.full(m_s.shape, NEG_INF, F32)
    l_s[...] = jnp.zeros(l_s.shape, F32)
    acc_s[...] = jnp.zeros(acc_s.shape, F32)

    def logits_stage(kb, lg_ref):
        lg_ref[...] = _dot(cn_s[kb], qlat_s[...], _NT)

    def softmax_stage(kb, lg_ref):
        neg = jnp.where(st_s[kb] >= tau, 0.0, NEG_INF)
        v0 = pl.multiple_of(jnp.maximum(KB * kb - QB * i + BIAS_ORIGIN, 0), LANES)
        ps, alphas = [], []
        for h in range(H_B):
            lh = lg_ref[:, h * QB:(h + 1) * QB] + bias_ref[h, pl.ds(v0, KB), :] + neg
            m_old = m_s[h:h + 1, :]
            m_new = jnp.maximum(m_old, jnp.max(lh, axis=0, keepdims=True))
            p = jnp.exp2(lh - m_new)
            alpha = jnp.exp2(m_old - m_new)
            l_s[h:h + 1, :] = alpha * l_s[h:h + 1, :] + jnp.sum(p, axis=0, keepdims=True)
            m_s[h:h + 1, :] = m_new
            ps.append(p.astype(BF16))
            alphas.append(alpha)
        pv = _dot(cnt_s[kb], jnp.concatenate(ps, axis=1))
        acc_s[...] = acc_s[...] * jnp.concatenate(alphas, axis=1) + pv

    logits_stage(0, lg_a)

    def att_pair(j, carry):
        kb = 2 * j
        logits_stage(kb + 1, lg_b)
        softmax_stage(kb, lg_a)
        logits_stage(jnp.minimum(kb + 2, n_kb - 1), lg_a)
        softmax_stage(kb + 1, lg_b)
        return carry

    lax.fori_loop(0, n_kb // 2, att_pair, 0)

    @pl.when(n_kb % 2 == 1)
    def _():
        softmax_stage(n_kb - 1, lg_a)

    for h in range(H_B):
        cols = slice(h * QB, (h + 1) * QB)
        ot = (acc_s[:, cols] * (1.0 / l_s[h:h + 1, :])).astype(BF16)
        bt = _dot(wuvt_ref[h], ot)
        gate = gb_ref[:, cols].astype(F32)
        o_ref[:, cols] = (bt.T * _silu(gate)).astype(o_ref.dtype)


def dsa_attention(proj, kv_gain, wuk_t, wuv_t, bias_tab, *, topk):
    bsz, tp, _ = proj.shape
    n_qb = tp // QB
    n_kb_all = -(-tp // KB)
    return pl.pallas_call(
        functools.partial(_attn_kernel, tp=tp, topk=topk),
        grid=(bsz, n_qb),
        in_specs=[
            pl.BlockSpec((None, tp, D_C), lambda b, i: (b, 0, E_COL_C // D_C)),
            pl.BlockSpec((None, tp, LANES), lambda b, i: (b, 0, E_COL_KK // LANES)),
            pl.BlockSpec((None, QB, LANES), lambda b, i: (b, i, E_COL_WI // LANES)),
            pl.BlockSpec((None, QB, H_B * DH_B), lambda b, i: (b, i, E_COL_Q // (H_B * DH_B))),
            pl.BlockSpec((None, QB, H_IDX * D_IDX), lambda b, i: (b, i, E_COL_QI // (H_IDX * D_IDX))),
            pl.BlockSpec((None, QB, H_B * DH_B), lambda b, i: (b, i, E_COL_GB // (H_B * DH_B))),
            pl.BlockSpec((1, D_C), lambda b, i: (0, 0)),
            pl.BlockSpec((H_B, DH_B, D_C), lambda b, i: (0, 0, 0)),
            pl.BlockSpec((H_B, DH_B, D_C), lambda b, i: (0, 0, 0)),
            pl.BlockSpec((H_B, BIAS_ROWS, LANES), lambda b, i: (0, 0, 0)),
        ],
        out_specs=pl.BlockSpec((None, QB, H_B * DH_B), lambda b, i: (b, i, 0)),
        out_shape=jax.ShapeDtypeStruct((bsz, tp, H_B * DH_B), BF16),
        scratch_shapes=[
            pltpu.VMEM((n_kb_all, KB, D_C), BF16),
            pltpu.VMEM((n_kb_all, D_C, KB), BF16),
            pltpu.VMEM((2, n_kb_all, KB, LANES), BF16),
            pltpu.VMEM((n_kb_all + 1, KB, QB), I32),
            pltpu.VMEM((n_kb_all + 1, KB, QB), I32),
            pltpu.VMEM((n_kb_all + 1, KB, QB), I32),
            pltpu.VMEM((H_B * QB, D_C), BF16),
            pltpu.VMEM((KB, H_B * QB), F32),
            pltpu.VMEM((KB, H_B * QB), F32),
            pltpu.VMEM((D_C, H_B * QB), F32),
            pltpu.VMEM((H_B, QB), F32),
            pltpu.VMEM((H_B, QB), F32),
            pltpu.VMEM((1, QB), I32),
        ],
        compiler_params=_cparams(("parallel", "arbitrary"), VMEM_MIXER),
        name="dsa_attention",
    )(proj, proj, proj, proj, proj, proj, kv_gain.reshape(1, D_C).astype(F32), wuk_t, wuv_t, bias_tab)


def _hgrn_constants():
    n = HC
    t = np.arange(n)[:, None]
    j = np.arange(n)[None, :]
    mats = [(j <= t)]
    for lvl in range(N_LEVELS):
        upper = ((t >> lvl) & 1) == 1
        m_up = (t >> lvl) << lvl
        m_lo = ((t >> lvl) + 1) << lvl
        mats.append(np.where(upper, (j >= m_up) & (j <= t), (j > t) & (j < m_lo)))
    m = np.concatenate(mats, axis=0).astype(np.float32)
    mcat = np.concatenate([m, m], axis=1)
    x = t ^ j
    lev = np.where(j < t, np.floor(np.log2(np.maximum(x, 1))).astype(np.int32), np.where(j == t, -1, -2))
    masks = np.stack([(lev == lvl) for lvl in range(-1, N_LEVELS)]).astype(np.float32)
    return mcat, masks


HGRN_HEADS_PER_STEP = 4


def _role_select(qf, k, lvl, up_small):
    half = 1 << lvl
    if half >= SUBLANES:
        pieces = []
        for blk in range(HC // (2 * half)):
            lo = 2 * half * blk
            pieces += [k[lo:lo + half], qf[lo + half:lo + 2 * half]]
        return jnp.concatenate(pieces, axis=0)
    g, w = HC // SUBLANES, qf.shape[-1]
    sel = jnp.where(up_small[lvl], qf.reshape(g, SUBLANES, w), k.reshape(g, SUBLANES, w))
    return sel.reshape(HC, w)


_SLOT_QE, _SLOT_KT, _SLOT_QB, _SLOT_KB, _SLOT_Y0 = 0, 1, 2, 3, 4
_N_SLOTS = _SLOT_Y0 + N_LEVELS


def _hgrn_kernel(q_ref, fz_ref, i_ref, g_ref, lbp_ref, ng_ref, mcat_ref, mask_ref, o_ref,
                 s_ref, stage_a, stage_b, dec_a, dec_b, *, n_steps, nh):
    wide = nh * DK_C
    row8 = lax.broadcasted_iota(I32, (1, SUBLANES, wide), 1)
    up_small = [((row8 >> lvl) & 1) == 1 for lvl in range(3)]
    s_ref[...] = jnp.zeros(s_ref.shape, F32)

    def gate_stage(c):
        rows = pl.ds(pl.multiple_of(c * HC, HC), HC)
        lb = lbp_ref[0:1, :]
        one_m_lb = lbp_ref[1:2, :]
        qf = _silu(q_ref[rows, :].astype(F32))
        z = fz_ref[rows, :].astype(F32)
        w = one_m_lb * (0.5 + 0.5 * jnp.tanh(0.5 * z))
        k = one_m_lb - w
        lf2 = jnp.log(lb + w) * LOG2_E
        hi = lf2.astype(BF16)
        lo = (lf2 - hi.astype(F32)).astype(BF16)
        return qf, k, _dot(mcat_ref[...], jnp.concatenate([hi, lo], axis=0))

    def operand_stage(gates, stage, dec_ref):
        qf, k, gsum = gates
        b = gsum[0:HC]
        e_b = jnp.exp2(b)
        e_s = jnp.exp2(b[HC - 1:HC, :] - b)
        stage[_SLOT_QE] = (qf * e_b).astype(BF16)
        stage[_SLOT_KT] = (k * e_s).astype(BF16)
        stage[_SLOT_QB] = qf.astype(BF16)
        stage[_SLOT_KB] = k.astype(BF16)
        for lvl in range(N_LEVELS):
            y = _role_select(qf, k, lvl, up_small) * jnp.exp2(gsum[(1 + lvl) * HC:(2 + lvl) * HC])
            stage[_SLOT_Y0 + lvl] = y.astype(BF16)
        dec_ref[...] = e_b[HC - 1:HC, :]

    def level_stage(c, stage):
        lns = [slice(hh * DK_C, (hh + 1) * DK_C) for hh in range(nh)]
        sts = [s_ref[hh] for hh in range(nh)]
        outs = [_dot(stage[_SLOT_QE, :, ln], st.astype(BF16), _NT) for ln, st in zip(lns, sts)]
        accs = [_dot(stage[_SLOT_QB, :, ln], stage[_SLOT_KB, :, ln], _NT) * mask_ref[0] for ln in lns]
        for lvl in range(N_LEVELS):
            for hh, ln in enumerate(lns):
                y = stage[_SLOT_Y0 + lvl, :, ln]
                accs[hh] = accs[hh] + _dot(y, y, _NT) * mask_ref[lvl + 1]
        return sts, outs, accs

    def output_stage(c, partial, stage, dec_ref):
        sts, outs, accs = partial
        rows = pl.ds(pl.multiple_of(c * HC, HC), HC)
        lns = [slice(hh * DK_C, (hh + 1) * DK_C) for hh in range(nh)]
        vhs = [i_ref[rows, ln] for ln in lns]
        for hh, ln in enumerate(lns):
            vt = vhs[hh].astype(F32).T.astype(BF16)
            s_ref[hh] = sts[hh] * dec_ref[:, ln] + _dot(vt, stage[_SLOT_KT, :, ln])
        normed = []
        for hh in range(nh):
            o = outs[hh] + _dot(accs[hh].astype(BF16), vhs[hh])
            normed.append(o * lax.rsqrt(jnp.mean(o * o, axis=-1, keepdims=True) + RMS_EPS))
        on = jnp.concatenate(normed, axis=1) * ng_ref[...]
        gx = g_ref[rows, :].astype(F32)
        o_ref[rows, :] = (on * _silu(gx)).astype(o_ref.dtype)

    last = n_steps - 1
    operand_stage(gate_stage(0), stage_a, dec_a)

    def pair(j, carry):
        c = 2 * j
        partial = level_stage(c, stage_a)
        operand_stage(gate_stage(c + 1), stage_b, dec_b)
        output_stage(c, partial, stage_a, dec_a)
        partial = level_stage(c + 1, stage_b)
        operand_stage(gate_stage(jnp.minimum(c + 2, last)), stage_a, dec_a)
        output_stage(c + 1, partial, stage_b, dec_b)
        return carry

    lax.fori_loop(0, n_steps // 2, pair, 0)
    if n_steps % 2:
        output_stage(last, level_stage(last, stage_a), stage_a, dec_a)


def hgrn2(proj, lb_params, norm_g):
    bsz, tp, _ = proj.shape
    mcat, masks = _hgrn_constants()
    nh = HGRN_HEADS_PER_STEP
    wide = nh * DK_C
    n_groups = H_C // nh
    col = lambda off: pl.BlockSpec((None, tp, wide), lambda b, h, off=off: (b, 0, off + h))
    return pl.pallas_call(
        functools.partial(_hgrn_kernel, n_steps=tp // HC, nh=nh),
        grid=(bsz, n_groups),
        in_specs=[
            col(0), col(n_groups), col(2 * n_groups), col(3 * n_groups),
            pl.BlockSpec((2, wide), lambda b, h: (0, h)),
            pl.BlockSpec((1, wide), lambda b, h: (0, h)),
            pl.BlockSpec(mcat.shape, lambda b, h: (0, 0)),
            pl.BlockSpec(masks.shape, lambda b, h: (0, 0, 0)),
        ],
        out_specs=pl.BlockSpec((None, tp, wide), lambda b, h: (b, 0, h)),
        out_shape=jax.ShapeDtypeStruct((bsz, tp, H_C * DK_C), BF16),
        scratch_shapes=[
            pltpu.VMEM((nh, DK_C, DK_C), F32),
            pltpu.VMEM((_N_SLOTS, HC, wide), BF16),
            pltpu.VMEM((_N_SLOTS, HC, wide), BF16),
            pltpu.VMEM((1, wide), F32),
            pltpu.VMEM((1, wide), F32),
        ],
        compiler_params=_cparams(("parallel", "parallel"), VMEM_MIXER),
        name="hgrn2",
    )(proj, proj, proj, proj, lb_params, norm_g.reshape(1, -1).astype(F32),
      jnp.asarray(mcat, BF16), jnp.asarray(masks, F32))


def _t5_bucket(rel):
    nb = N_BUCKETS // 2
    ret = jnp.where(rel > 0, nb, 0)
    n = jnp.abs(rel)
    max_exact = nb // 2
    nf = jnp.maximum(n, 1).astype(F32)
    large = max_exact + (jnp.log(nf / max_exact) / math.log(MAX_DISTANCE / max_exact)
                         * (nb - max_exact)).astype(I32)
    large = jnp.minimum(large, nb - 1)
    return ret + jnp.where(n < max_exact, n, large)


def _bias_table(rel_bias_table):
    v = jnp.arange(BIAS_ROWS, dtype=I32)[:, None]
    q = jnp.arange(LANES, dtype=I32)[None, :]
    onehot = jax.nn.one_hot(_t5_bucket(v - BIAS_ORIGIN - q), N_BUCKETS, dtype=F32)
    tab = jnp.einsum("vqb,bh->hvq", onehot, rel_bias_table.astype(F32), precision=lax.Precision.HIGHEST)
    return tab * LOG2_E


def _even_in_weight(w):
    glu_v, glu_g, gate_a, q, c, gate_b, qi, ki, wi = jnp.split(
        w, np.cumsum([D_A, D_A, D_A, H_B * DH_B, D_C, H_B * DH_B, H_IDX * D_IDX, D_IDX])[:8].tolist(), axis=1)
    pad = jnp.zeros((w.shape[0], E_NPAD - E_COL_WI - H_IDX), w.dtype)
    cols = [glu_v, glu_g, gate_a, q, gate_b, qi * (D_IDX ** -0.5), c, ki, ki, wi * (H_IDX ** -0.5), pad]
    return jnp.concatenate([col.astype(BF16) for col in cols], axis=1)


def kernel(x, meta_tokens, norm_gain, final_norm_gain, rel_bias_table, w_in_even, conv_w, conv_b,
           conv_ln_gain, conv_ln_bias, kv_norm_gain, w_uk, w_uv, w_out_even, w_in_odd, lb_logits,
           rec_norm_gain, w_out_odd):
    bsz, seq, d = x.shape
    assert d == D_MODEL and seq % LANES == 0
    depth = norm_gain.shape[0]
    tp = FRONT_PAD + N_META + seq
    topk = min(TOPK_MAX, seq // 4)
    tt_mm = tp // 2 if (tp // 2) % 16 == 0 else tp
    tt_out = tp // 4 if (tp // 4) % 16 == 0 else tp
    tt_last = seq // 4
    tt_conv = tp // 4 if (tp // 4) % CONV_RB == 0 else tp // 2 if (tp // 2) % CONV_RB == 0 else tp
    h, hn = embed(x, meta_tokens, norm_gain[0], tt=tt_out)

    lb_soft = jax.nn.softmax(lb_logits.astype(F32), axis=0)
    lower_bounds = jnp.cumsum(lb_soft, axis=0) - lb_soft[0]
    bias_tab = _bias_table(rel_bias_table)

    for layer in range(depth):
        last = layer == depth - 1
        if layer % 2 == 0:
            e = layer // 2
            w_in, tn = _even_in_weight(w_in_even[e]), E_NPAD // 2
        else:
            o = layer // 2
            w_in, tn = w_in_odd[o].astype(BF16), 2048
        proj = matmul(hn, w_in, tt=tt_mm, tn=tn)
        if layer % 2 == 0:
            a_out = conv_module(proj, conv_w[e], conv_b[e], conv_ln_gain[e], conv_ln_bias[e], tt=tt_conv)
            wuk_t = jnp.transpose(w_uk[e], (0, 2, 1)).astype(BF16)
            wuv_t = jnp.transpose(w_uv[e], (0, 2, 1)).astype(BF16)
            b_out = dsa_attention(proj, kv_norm_gain[e], wuk_t, wuv_t, bias_tab, topk=topk)
            mixed, w_out = [a_out, b_out], w_out_even[e].astype(BF16)
        else:
            lb = lower_bounds[layer]
            mixed = [hgrn2(proj, jnp.stack([lb, 1.0 - lb], axis=0), rec_norm_gain[o])]
            w_out = w_out_odd[o].astype(BF16)
        if last:
            return out_proj(mixed, w_out, h, final_norm_gain, tt=tt_last, final=True)
        h, hn = out_proj(mixed, w_out, h, norm_gain[layer + 1], tt=tt_out, final=False)
```

```python
import functools
import math

import numpy as np
import jax
import jax.numpy as jnp
from jax import lax
from jax.experimental import pallas as pl
from jax.experimental.pallas import tpu as pltpu

F32 = jnp.float32
BF16 = jnp.bfloat16
I32 = jnp.int32

D_MODEL = 2048
CHUNK_LOG2 = 6
N_META = 16
D_A = 1024
CONV_WIDTH = 31
H_B = 8
DH_B = 128
D_C = 256
H_IDX = 16
D_IDX = 64
TOPK_MAX = 256
N_BUCKETS = 32
MAX_DISTANCE = 128
H_C = 16
DK_C = 128
RMS_EPS = 1e-6
LN_EPS = 1e-5
NEG_INF = -1e30
INT_MIN = -(2 ** 31)
LOG2_E = 1.4426950408889634

I16_MIN = -(2 ** 15)

LANES = 128
SUBLANES = 8
MXU_N = 256

FRONT_PAD = LANES - N_META
QB = LANES
KB = MXU_N
HC = 128
N_LEVELS = 7

E_COL_Q = 3 * D_A
E_COL_GB = E_COL_Q + H_B * DH_B
E_COL_QI = E_COL_GB + H_B * DH_B
E_COL_C = E_COL_QI + H_IDX * D_IDX
E_COL_KK = E_COL_C + D_C
E_COL_WI = E_COL_KK + 2 * D_IDX
E_NPAD = E_COL_WI + LANES
assert 2 * D_IDX == LANES and E_NPAD % MXU_N == 0

MIB = 1024 * 1024
VMEM_PROJ = 56
VMEM_MIXER = 40


def _cparams(sem, vmem_mib):
    return pltpu.CompilerParams(dimension_semantics=sem, vmem_limit_bytes=vmem_mib * MIB)


def _sigmoid(x):
    return 1.0 / (1.0 + jnp.exp(-x))


def _silu(x):
    h = 0.5 * x
    return h + h * jnp.tanh(h)


_NN = (((1,), (0,)), ((), ()))
_NT = (((1,), (1,)), ((), ()))


def _dot(a, b, dims=_NN):
    return lax.dot_general(a, b, dims, preferred_element_type=F32)


def _rms_normed(x, gain):
    ms = jnp.mean(x * x, axis=-1, keepdims=True)
    return x * lax.rsqrt(ms + RMS_EPS) * gain


def _embed_kernel(x_ref, meta_ref, g_ref, h_ref, hn_ref, *, tt):
    first = FRONT_PAD + N_META

    @pl.when(pl.program_id(1) == 0)
    def _():
        d = h_ref.shape[-1]
        meta = meta_ref[...]
        x = x_ref[0, 0:tt - first, :]
        h_ref[0:FRONT_PAD, :] = jnp.zeros((FRONT_PAD, d), F32)
        h_ref[FRONT_PAD:first, :] = meta
        h_ref[first:, :] = x
        hn_ref[0:FRONT_PAD, :] = jnp.zeros((FRONT_PAD, d), BF16)
        hn_ref[FRONT_PAD:first, :] = _rms_normed(meta, g_ref[...]).astype(BF16)
        hn_ref[first:, :] = _rms_normed(x, g_ref[...]).astype(BF16)

    @pl.when(pl.program_id(1) > 0)
    def _():
        x = x_ref[0]
        h_ref[...] = x
        hn_ref[...] = _rms_normed(x, g_ref[...]).astype(BF16)


def embed(x, meta_tokens, gain, *, tt):
    bsz, seq, d = x.shape
    first = FRONT_PAD + N_META
    tp = first + seq
    x_spec = pl.BlockSpec((pl.Element(1), pl.Element(tt), pl.Element(d)),
                          lambda b, j: (b, pl.multiple_of(jnp.maximum(j * tt - first, 0), SUBLANES), 0))
    spec = pl.BlockSpec((None, tt, d), lambda b, j: (b, j, 0))
    return pl.pallas_call(
        functools.partial(_embed_kernel, tt=tt),
        grid=(bsz, tp // tt),
        in_specs=[
            x_spec,
            pl.BlockSpec((N_META, d), lambda b, j: (0, 0)),
            pl.BlockSpec((1, d), lambda b, j: (0, 0)),
        ],
        out_specs=[spec, spec],
        out_shape=[jax.ShapeDtypeStruct((bsz, tp, d), F32), jax.ShapeDtypeStruct((bsz, tp, d), BF16)],
        compiler_params=_cparams(("parallel", "parallel"), VMEM_MIXER),
        name="embed",
    )(x.astype(F32), meta_tokens.astype(F32), gain.reshape(1, d).astype(F32))


def _matmul_kernel(x_ref, w_ref, o_ref):
    o_ref[...] = _dot(x_ref[...], w_ref[...]).astype(o_ref.dtype)


def matmul(hn, w, *, tt, tn):
    bsz, tp, d = hn.shape
    n = w.shape[1]
    return pl.pallas_call(
        _matmul_kernel,
        grid=(bsz, tp // tt, n // tn),
        in_specs=[
            pl.BlockSpec((None, tt, d), lambda b, i, j: (b, i, 0)),
            pl.BlockSpec((d, tn), lambda b, i, j: (0, j)),
        ],
        out_specs=pl.BlockSpec((None, tt, tn), lambda b, i, j: (b, i, j)),
        out_shape=jax.ShapeDtypeStruct((bsz, tp, n), BF16),
        compiler_params=_cparams(("parallel", "parallel", "parallel"), VMEM_PROJ),
        name="in_proj",
    )(hn, w)


def _out_proj_kernel(*refs, n_in, tt, final):
    xs = refs[:n_in]
    w_ref, h_ref, g_ref = refs[n_in], refs[n_in + 1], refs[n_in + 2]
    tile = lambda ref: ref[0] if len(ref.shape) == 3 else ref[...]
    y = None
    off = 0
    for x_ref in xs:
        kdim = x_ref.shape[-1]
        part = _dot(tile(x_ref), w_ref[off:off + kdim, :])
        y = part if y is None else y + part
        off += kdim
    h_new = tile(h_ref) + y
    if final:
        refs[-1][...] = _rms_normed(h_new, g_ref[...])
    else:
        row = pl.program_id(1) * tt + lax.broadcasted_iota(I32, (tt, 1), 0)
        h_new = jnp.where(row >= FRONT_PAD, h_new, 0.0)
        refs[-2][...] = h_new
        refs[-1][...] = _rms_normed(h_new, g_ref[...]).astype(BF16)


def out_proj(xs, w, h, gain, *, tt, final):
    bsz, tp, d = h.shape
    if final:
        first = FRONT_PAD + N_META
        rows_out = tp - first
        row_spec = lambda k: pl.BlockSpec((pl.Element(1), pl.Element(tt), pl.Element(k)),
                                          lambda b, i: (b, pl.multiple_of(first + i * tt, LANES), 0))
    else:
        rows_out = tp
        row_spec = lambda k: pl.BlockSpec((None, tt, k), lambda b, i: (b, i, 0))
    in_specs = [row_spec(x.shape[-1]) for x in xs]
    in_specs += [pl.BlockSpec(w.shape, lambda b, i: (0, 0)), row_spec(d), pl.BlockSpec((1, d), lambda b, i: (0, 0))]
    out_spec = pl.BlockSpec((None, tt, d), lambda b, i: (b, i, 0))
    out_f32 = jax.ShapeDtypeStruct((bsz, rows_out, d), F32)
    return pl.pallas_call(
        functools.partial(_out_proj_kernel, n_in=len(xs), tt=tt, final=final),
        grid=(bsz, rows_out // tt),
        in_specs=in_specs,
        out_specs=out_spec if final else [out_spec, out_spec],
        out_shape=out_f32 if final else [out_f32, jax.ShapeDtypeStruct((bsz, rows_out, d), BF16)],
        compiler_params=_cparams(("parallel", "parallel"), VMEM_PROJ),
        name="out_proj_final" if final else "out_proj",
    )(*xs, w, h, gain.reshape(1, d).astype(F32))


CONV_HALO = 32
CONV_RB = 16
CONV_CHAINS = 4


def _conv_kernel(v_ref, g_ref, ga_ref, w_ref, cb_ref, lg_ref, lb_ref, o_ref, us_ref, cv_ref, *, tt):
    j = pl.program_id(1)
    span = tt + CONV_HALO - SUBLANES
    n_rb = tt // CONV_RB
    for cb in range(D_A // LANES):
        lanes = slice(cb * LANES, (cb + 1) * LANES)

        @pl.when(j == 0)
        def _(cb=cb):
            us_ref[0, cb, 0:CONV_HALO, :] = jnp.zeros((CONV_HALO, LANES), F32)

        @pl.when(j > 0)
        def _(cb=cb):
            us_ref[0, cb, 0:CONV_HALO, :] = us_ref[0, cb, tt:tt + CONV_HALO, :]

        us_ref[0, cb, CONV_HALO:CONV_HALO + tt, :] = (
            v_ref[:, lanes].astype(F32) * _sigmoid(g_ref[:, lanes].astype(F32)))
        for r in range(1, SUBLANES):
            us_ref[r, cb, 0:span, :] = us_ref[0, cb, r:r + span, :]

        wb = [jnp.broadcast_to(w_ref[k:k + 1, lanes], (CONV_RB, LANES)) for k in range(CONV_WIDTH)]
        bias = jnp.broadcast_to(cb_ref[0:1, lanes], (CONV_RB, LANES))

        def rb_body(rb, carry, cb=cb, lanes=lanes, wb=wb, bias=bias):
            base = pl.multiple_of(rb * CONV_RB, CONV_RB)
            parts = [bias] + [None] * (CONV_CHAINS - 1)
            for k in range(CONV_WIDTH):
                a, r = divmod(k + CONV_HALO - (CONV_WIDTH - 1), SUBLANES)
                term = wb[k] * us_ref[r, cb, pl.ds(base + SUBLANES * a, CONV_RB), :]
                c = k % CONV_CHAINS
                parts[c] = term if parts[c] is None else parts[c] + term
            cv_ref[pl.ds(base, CONV_RB), lanes] = (parts[0] + parts[1]) + (parts[2] + parts[3])
            return carry

        lax.fori_loop(0, n_rb, rb_body, 0)

    ln_gain, ln_bias = lg_ref[...], lb_ref[...]
    ln_rows = tt // 2 if (tt // 2) % 16 == 0 else tt
    for lo in range(0, tt, ln_rows):
        x = cv_ref[lo:lo + ln_rows, :]
        xc = x - jnp.mean(x, axis=-1, keepdims=True)
        var = jnp.mean(xc * xc, axis=-1, keepdims=True)
        y = xc * lax.rsqrt(var + LN_EPS) * ln_gain + ln_bias
        ga = ga_ref[lo:lo + ln_rows, :].astype(F32)
        o_ref[lo:lo + ln_rows, :] = (_silu(y) * _silu(ga)).astype(o_ref.dtype)


def conv_module(proj, conv_w, conv_b, ln_g, ln_b, *, tt):
    bsz, tp, _ = proj.shape
    row = lambda a: a.reshape(1, D_A).astype(F32)
    return pl.pallas_call(
        functools.partial(_conv_kernel, tt=tt),
        grid=(bsz, tp // tt),
        in_specs=[
            pl.BlockSpec((None, tt, D_A), lambda b, j: (b, j, 0)),
            pl.BlockSpec((None, tt, D_A), lambda b, j: (b, j, 1)),
            pl.BlockSpec((None, tt, D_A), lambda b, j: (b, j, 2)),
            pl.BlockSpec((CONV_WIDTH, D_A), lambda b, j: (0, 0)),
            pl.BlockSpec((1, D_A), lambda b, j: (0, 0)),
            pl.BlockSpec((1, D_A), lambda b, j: (0, 0)),
            pl.BlockSpec((1, D_A), lambda b, j: (0, 0)),
        ],
        out_specs=pl.BlockSpec((None, tt, D_A), lambda b, j: (b, j, 0)),
        out_shape=jax.ShapeDtypeStruct((bsz, tp, D_A), BF16),
        scratch_shapes=[
            pltpu.VMEM((SUBLANES, D_A // LANES, tt + CONV_HALO, LANES), F32),
            pltpu.VMEM((tt, D_A), F32),
        ],
        compiler_params=_cparams(("parallel", "arbitrary"), VMEM_MIXER),
        name="conv_module",
    )(proj, proj, proj, conv_w.astype(F32), row(conv_b), row(ln_g), row(ln_b))


BIAS_ORIGIN = KB + QB
BIAS_ROWS = BIAS_ORIGIN + KB


def _attn_kernel(c_ref, kk_ref, wi_ref, q_ref, qi_ref, gb_ref, kvg_ref, wuk_ref, wuvt_ref, bias_ref,
                 o_ref, cn_s, cnt_s, kk_s, st_s, hi_s, lo_s, qlat_s, lg_a, lg_b, acc_s, m_s, l_s,
                 tau_s, *, tp, topk):
    i = pl.program_id(1)
    n_kb = (i + 2) // 2
    n_kb_all = cn_s.shape[0]

    @pl.when(i == 0)
    def _prepare_keys():
        gain = kvg_ref[...]
        for kb in range(n_kb_all):
            lo = kb * KB
            nrow = min(KB, tp - lo)
            c = c_ref[lo:lo + nrow, :].astype(F32)
            cn = c * lax.rsqrt(jnp.mean(c * c, axis=-1, keepdims=True) + RMS_EPS) * gain
            cn_s[kb, 0:nrow, :] = cn.astype(BF16)
            kk = kk_ref[lo:lo + nrow, :]
            lane = lax.broadcasted_iota(I32, kk.shape, 1)
            kk_s[0, kb, 0:nrow, :] = jnp.where(lane < D_IDX, kk, jnp.zeros_like(kk))
            kk_s[1, kb, 0:nrow, :] = jnp.where(lane >= D_IDX, kk, jnp.zeros_like(kk))
            for rb in range(nrow // LANES):
                for cb in range(D_C // LANES):
                    tile = cn[rb * LANES:(rb + 1) * LANES, cb * LANES:(cb + 1) * LANES]
                    cnt_s[kb, cb * LANES:(cb + 1) * LANES, rb * LANES:(rb + 1) * LANES] = tile.T.astype(BF16)
            if nrow < KB:
                cn_s[kb, nrow:KB, :] = jnp.zeros((KB - nrow, D_C), BF16)
                kk_s[:, kb, nrow:KB, :] = jnp.zeros((2, KB - nrow, LANES), BF16)
                cnt_s[kb, :, nrow:KB] = jnp.zeros((D_C, KB - nrow), BF16)

    for h in range(H_B):
        qh = q_ref[:, h * DH_B:(h + 1) * DH_B]
        ql = _dot(qh, wuk_ref[h]) * (DH_B ** -0.5 * LOG2_E)
        qlat_s[h * QB:(h + 1) * QB, :] = ql.astype(BF16)

    wit = wi_ref[...].astype(F32).T

    def score_block(kb):
        k_even = kk_s[0, kb]
        k_odd = kk_s[1, kb]
        acc = jnp.zeros((KB, QB), F32)
        for p in range(H_IDX // 2):
            qp = qi_ref[:, p * LANES:(p + 1) * LANES]
            acc = acc + jnp.maximum(_dot(k_even, qp, _NT), 0.0) * wit[2 * p:2 * p + 1, :]
            acc = acc + jnp.maximum(_dot(k_odd, qp, _NT), 0.0) * wit[2 * p + 1:2 * p + 2, :]
        bits = lax.bitcast_convert_type(acc, I32)
        key = bits ^ ((bits >> 31) & 0x7FFFFFFF)
        s_idx = kb * KB + lax.broadcasted_iota(I32, (KB, QB), 0)
        t_idx = i * QB + lax.broadcasted_iota(I32, (KB, QB), 1)
        adm = (s_idx >= FRONT_PAD) & (jnp.maximum(s_idx >> CHUNK_LOG2, 1) <= jnp.maximum(t_idx >> CHUNK_LOG2, 1))
        key = jnp.where(adm, key, INT_MIN)
        st_s[kb] = key
        hi_s[kb] = key >> 16

    def score_pair(j, carry):
        score_block(2 * j)
        score_block(jnp.minimum(2 * j + 1, n_kb_all - 1))
        return carry

    lax.fori_loop(0, (n_kb + 1) // 2, score_pair, 0)

    def search16(ref, need, n_blocks):
        def bit_body(it, carry):
            t, above = carry
            cand = t + lax.shift_left(jnp.int32(1), jnp.int32(15) - it)
            cand8 = jnp.broadcast_to(cand, (SUBLANES, QB))[None]
            parts = []
            for kb in range(n_blocks):
                below = (ref[kb].reshape(KB // SUBLANES, SUBLANES, QB) - cand8) >> 31
                parts += [below[g] for g in range(KB // SUBLANES)]
            while len(parts) > 1:
                nxt = [parts[g] + parts[g + 1] for g in range(0, len(parts) - 1, 2)]
                parts = nxt + parts[len(parts) - len(parts) % 2:]
            cnt = n_blocks * KB + jnp.sum(parts[0], axis=0, keepdims=True)
            ok = cnt >= need
            return jnp.where(ok, cand, t), jnp.where(ok, above, cnt)

        init = (jnp.full((1, QB), I16_MIN, I32), jnp.zeros((1, QB), I32))
        return lax.fori_loop(0, 16, bit_body, init)

    st_s[n_kb] = jnp.full((KB, QB), INT_MIN, I32)
    hi_s[n_kb] = jnp.full((KB, QB), I16_MIN, I32)
    n_pairs = (n_kb + 1) // 2
    few_keys = N_META + QB * i <= topk

    @pl.when(few_keys)
    def _():
        tau_s[...] = jnp.full((1, QB), INT_MIN, I32)

    for pairs in range(1, (n_kb_all + 1) // 2 + 1):
        @pl.when(jnp.logical_and(n_pairs == pairs, jnp.logical_not(few_keys)))
        def _(n_blocks=2 * pairs):
            tau_hi, n_above = search16(hi_s, jnp.full((1, QB), topk, I32), n_blocks)
            for kb in range(n_blocks):
                key = st_s[kb]
                low = (key & 0xFFFF) + I16_MIN
                lo_s[kb] = jnp.where((key >> 16) == tau_hi, low, I16_MIN)
            tau_lo, _ = search16(lo_s, topk - n_above, n_blocks)
            tau_s[...] = lax.shift_left(tau_hi, jnp.int32(16)) | (tau_lo - I16_MIN)

    tau = jnp.maximum(tau_s[...], INT_MIN + 1)

    m_s[...] = jnp.full(m_s.shape, NEG_INF, F32)
    l_s[...] = jnp.zeros(l_s.shape, F32)
    acc_s[...] = jnp.zeros(acc_s.shape, F32)

    def logits_stage(kb, lg_ref):
        lg_ref[...] = _dot(cn_s[kb], qlat_s[...], _NT)

    def softmax_stage(kb, lg_ref):
        neg = jnp.where(st_s[kb] >= tau, 0.0, NEG_INF)
        v0 = pl.multiple_of(jnp.maximum(KB * kb - QB * i + BIAS_ORIGIN, 0), LANES)
        ps, alphas = [], []
        for h in range(H_B):
            lh = lg_ref[:, h * QB:(h + 1) * QB] + bias_ref[h, pl.ds(v0, KB), :] + neg
            m_old = m_s[h:h + 1, :]
            m_new = jnp.maximum(m_old, jnp.max(lh, axis=0, keepdims=True))
            p = jnp.exp2(lh - m_new)
            alpha = jnp.exp2(m_old - m_new)
            l_s[h:h + 1, :] = alpha * l_s[h:h + 1, :] + jnp.sum(p, axis=0, keepdims=True)
            m_s[h:h + 1, :] = m_new
            ps.append(p.astype(BF16))
            alphas.append(alpha)
        pv = _dot(cnt_s[kb], jnp.concatenate(ps, axis=1))
        acc_s[...] = acc_s[...] * jnp.concatenate(alphas, axis=1) + pv

    logits_stage(0, lg_a)

    def att_pair(j, carry):
        kb = 2 * j
        logits_stage(kb + 1, lg_b)
        softmax_stage(kb, lg_a)
        logits_stage(jnp.minimum(kb + 2, n_kb - 1), lg_a)
        softmax_stage(kb + 1, lg_b)
        return carry

    lax.fori_loop(0, n_kb // 2, att_pair, 0)

    @pl.when(n_kb % 2 == 1)
    def _():
        softmax_stage(n_kb - 1, lg_a)

    for h in range(H_B):
        cols = slice(h * QB, (h + 1) * QB)
        ot = (acc_s[:, cols] * (1.0 / l_s[h:h + 1, :])).astype(BF16)
        bt = _dot(wuvt_ref[h], ot)
        gate = gb_ref[:, cols].astype(F32)
        o_ref[:, cols] = (bt.T * _silu(gate)).astype(o_ref.dtype)


def dsa_attention(proj, kv_gain, wuk_t, wuv_t, bias_tab, *, topk):
    bsz, tp, _ = proj.shape
    n_qb = tp // QB
    n_kb_all = -(-tp // KB)
    return pl.pallas_call(
        functools.partial(_attn_kernel, tp=tp, topk=topk),
        grid=(bsz, n_qb),
        in_specs=[
            pl.BlockSpec((None, tp, D_C), lambda b, i: (b, 0, E_COL_C // D_C)),
            pl.BlockSpec((None, tp, LANES), lambda b, i: (b, 0, E_COL_KK // LANES)),
            pl.BlockSpec((None, QB, LANES), lambda b, i: (b, i, E_COL_WI // LANES)),
            pl.BlockSpec((None, QB, H_B * DH_B), lambda b, i: (b, i, E_COL_Q // (H_B * DH_B))),
            pl.BlockSpec((None, QB, H_IDX * D_IDX), lambda b, i: (b, i, E_COL_QI // (H_IDX * D_IDX))),
            pl.BlockSpec((None, QB, H_B * DH_B), lambda b, i: (b, i, E_COL_GB // (H_B * DH_B))),
            pl.BlockSpec((1, D_C), lambda b, i: (0, 0)),
            pl.BlockSpec((H_B, DH_B, D_C), lambda b, i: (0, 0, 0)),
            pl.BlockSpec((H_B, DH_B, D_C), lambda b, i: (0, 0, 0)),
            pl.BlockSpec((H_B, BIAS_ROWS, LANES), lambda b, i: (0, 0, 0)),
        ],
        out_specs=pl.BlockSpec((None, QB, H_B * DH_B), lambda b, i: (b, i, 0)),
        out_shape=jax.ShapeDtypeStruct((bsz, tp, H_B * DH_B), BF16),
        scratch_shapes=[
            pltpu.VMEM((n_kb_all, KB, D_C), BF16),
            pltpu.VMEM((n_kb_all, D_C, KB), BF16),
            pltpu.VMEM((2, n_kb_all, KB, LANES), BF16),
            pltpu.VMEM((n_kb_all + 1, KB, QB), I32),
            pltpu.VMEM((n_kb_all + 1, KB, QB), I32),
            pltpu.VMEM((n_kb_all + 1, KB, QB), I32),
            pltpu.VMEM((H_B * QB, D_C), BF16),
            pltpu.VMEM((KB, H_B * QB), F32),
            pltpu.VMEM((KB, H_B * QB), F32),
            pltpu.VMEM((D_C, H_B * QB), F32),
            pltpu.VMEM((H_B, QB), F32),
            pltpu.VMEM((H_B, QB), F32),
            pltpu.VMEM((1, QB), I32),
        ],
        compiler_params=_cparams(("parallel", "arbitrary"), VMEM_MIXER),
        name="dsa_attention",
    )(proj, proj, proj, proj, proj, proj, kv_gain.reshape(1, D_C).astype(F32), wuk_t, wuv_t, bias_tab)


def _hgrn_constants():
    n = HC
    t = np.arange(n)[:, None]
    j = np.arange(n)[None, :]
    mats = [(j <= t)]
    for lvl in range(N_LEVELS):
        upper = ((t >> lvl) & 1) == 1
        m_up = (t >> lvl) << lvl
        m_lo = ((t >> lvl) + 1) << lvl
        mats.append(np.where(upper, (j >= m_up) & (j <= t), (j > t) & (j < m_lo)))
    m = np.concatenate(mats, axis=0).astype(np.float32)
    mcat = np.concatenate([m, m], axis=1)
    x = t ^ j
    lev = np.where(j < t, np.floor(np.log2(np.maximum(x, 1))).astype(np.int32), np.where(j == t, -1, -2))
    masks = np.stack([(lev == lvl) for lvl in range(-1, N_LEVELS)]).astype(np.float32)
    return mcat, masks


HGRN_HEADS_PER_STEP = 4


def _role_select(qf, k, lvl, up_small):
    half = 1 << lvl
    if half >= SUBLANES:
        pieces = []
        for blk in range(HC // (2 * half)):
            lo = 2 * half * blk
            pieces += [k[lo:lo + half], qf[lo + half:lo + 2 * half]]
        return jnp.concatenate(pieces, axis=0)
    g, w = HC // SUBLANES, qf.shape[-1]
    sel = jnp.where(up_small[lvl], qf.reshape(g, SUBLANES, w), k.reshape(g, SUBLANES, w))
    return sel.reshape(HC, w)


_SLOT_QE, _SLOT_KT, _SLOT_QB, _SLOT_KB, _SLOT_Y0 = 0, 1, 2, 3, 4
_N_SLOTS = _SLOT_Y0 + N_LEVELS


def _hgrn_kernel(q_ref, fz_ref, i_ref, g_ref, lbp_ref, ng_ref, mcat_ref, mask_ref, o_ref,
                 s_ref, stage_a, stage_b, dec_a, dec_b, *, n_steps, nh):
    wide = nh * DK_C
    row8 = lax.broadcasted_iota(I32, (1, SUBLANES, wide), 1)
    up_small = [((row8 >> lvl) & 1) == 1 for lvl in range(3)]
    s_ref[...] = jnp.zeros(s_ref.shape, F32)

    def gate_stage(c):
        rows = pl.ds(pl.multiple_of(c * HC, HC), HC)
        lb = lbp_ref[0:1, :]
        one_m_lb = lbp_ref[1:2, :]
        qf = _silu(q_ref[rows, :].astype(F32))
        z = fz_ref[rows, :].astype(F32)
        w = one_m_lb * (0.5 + 0.5 * jnp.tanh(0.5 * z))
        k = one_m_lb - w
        lf2 = jnp.log(lb + w) * LOG2_E
        hi = lf2.astype(BF16)
        lo = (lf2 - hi.astype(F32)).astype(BF16)
        return qf, k, _dot(mcat_ref[...], jnp.concatenate([hi, lo], axis=0))

    def operand_stage(gates, stage, dec_ref):
        qf, k, gsum = gates
        b = gsum[0:HC]
        e_b = jnp.exp2(b)
        e_s = jnp.exp2(b[HC - 1:HC, :] - b)
        stage[_SLOT_QE] = (qf * e_b).astype(BF16)
        stage[_SLOT_KT] = (k * e_s).astype(BF16)
        stage[_SLOT_QB] = qf.astype(BF16)
        stage[_SLOT_KB] = k.astype(BF16)
        for lvl in range(N_LEVELS):
            y = _role_select(qf, k, lvl, up_small) * jnp.exp2(gsum[(1 + lvl) * HC:(2 + lvl) * HC])
            stage[_SLOT_Y0 + lvl] = y.astype(BF16)
        dec_ref[...] = e_b[HC - 1:HC, :]

    def level_stage(c, stage):
        lns = [slice(hh * DK_C, (hh + 1) * DK_C) for hh in range(nh)]
        sts = [s_ref[hh] for hh in range(nh)]
        outs = [_dot(stage[_SLOT_QE, :, ln], st.astype(BF16), _NT) for ln, st in zip(lns, sts)]
        accs = [_dot(stage[_SLOT_QB, :, ln], stage[_SLOT_KB, :, ln], _NT) * mask_ref[0] for ln in lns]
        for lvl in range(N_LEVELS):
            for hh, ln in enumerate(lns):
                y = stage[_SLOT_Y0 + lvl, :, ln]
                accs[hh] = accs[hh] + _dot(y, y, _NT) * mask_ref[lvl + 1]
        return sts, outs, accs

    def output_stage(c, partial, stage, dec_ref):
        sts, outs, accs = partial
        rows = pl.ds(pl.multiple_of(c * HC, HC), HC)
        lns = [slice(hh * DK_C, (hh + 1) * DK_C) for hh in range(nh)]
        vhs = [i_ref[rows, ln] for ln in lns]
        for hh, ln in enumerate(lns):
            vt = vhs[hh].astype(F32).T.astype(BF16)
            s_ref[hh] = sts[hh] * dec_ref[:, ln] + _dot(vt, stage[_SLOT_KT, :, ln])
        normed = []
        for hh in range(nh):
            o = outs[hh] + _dot(accs[hh].astype(BF16), vhs[hh])
            normed.append(o * lax.rsqrt(jnp.mean(o * o, axis=-1, keepdims=True) + RMS_EPS))
        on = jnp.concatenate(normed, axis=1) * ng_ref[...]
        gx = g_ref[rows, :].astype(F32)
        o_ref[rows, :] = (on * _silu(gx)).astype(o_ref.dtype)

    last = n_steps - 1
    operand_stage(gate_stage(0), stage_a, dec_a)

    def pair(j, carry):
        c = 2 * j
        partial = level_stage(c, stage_a)
        operand_stage(gate_stage(c + 1), stage_b, dec_b)
        output_stage(c, partial, stage_a, dec_a)
        partial = level_stage(c + 1, stage_b)
        operand_stage(gate_stage(jnp.minimum(c + 2, last)), stage_a, dec_a)
        output_stage(c + 1, partial, stage_b, dec_b)
        return carry

    lax.fori_loop(0, n_steps // 2, pair, 0)
    if n_steps % 2:
        output_stage(last, level_stage(last, stage_a), stage_a, dec_a)


def hgrn2(proj, lb_params, norm_g):
    bsz, tp, _ = proj.shape
    mcat, masks = _hgrn_constants()
    nh = HGRN_HEADS_PER_STEP
    wide = nh * DK_C
    n_groups = H_C // nh
    col = lambda off: pl.BlockSpec((None, tp, wide), lambda b, h, off=off: (b, 0, off + h))
    return pl.pallas_call(
        functools.partial(_hgrn_kernel, n_steps=tp // HC, nh=nh),
        grid=(bsz, n_groups),
        in_specs=[
            col(0), col(n_groups), col(2 * n_groups), col(3 * n_groups),
            pl.BlockSpec((2, wide), lambda b, h: (0, h)),
            pl.BlockSpec((1, wide), lambda b, h: (0, h)),
            pl.BlockSpec(mcat.shape, lambda b, h: (0, 0)),
            pl.BlockSpec(masks.shape, lambda b, h: (0, 0, 0)),
        ],
        out_specs=pl.BlockSpec((None, tp, wide), lambda b, h: (b, 0, h)),
        out_shape=jax.ShapeDtypeStruct((bsz, tp, H_C * DK_C), BF16),
        scratch_shapes=[
            pltpu.VMEM((nh, DK_C, DK_C), F32),
            pltpu.VMEM((_N_SLOTS, HC, wide), BF16),
            pltpu.VMEM((_N_SLOTS, HC, wide), BF16),
            pltpu.VMEM((1, wide), F32),
            pltpu.VMEM((1, wide), F32),
        ],
        compiler_params=_cparams(("parallel", "parallel"), VMEM_MIXER),
        name="hgrn2",
    )(proj, proj, proj, proj, lb_params, norm_g.reshape(1, -1).astype(F32),
      jnp.asarray(mcat, BF16), jnp.asarray(masks, F32))


def _t5_bucket(rel):
    nb = N_BUCKETS // 2
    ret = jnp.where(rel > 0, nb, 0)
    n = jnp.abs(rel)
    max_exact = nb // 2
    nf = jnp.maximum(n, 1).astype(F32)
    large = max_exact + (jnp.log(nf / max_exact) / math.log(MAX_DISTANCE / max_exact)
                         * (nb - max_exact)).astype(I32)
    large = jnp.minimum(large, nb - 1)
    return ret + jnp.where(n < max_exact, n, large)


def _bias_table(rel_bias_table):
    v = jnp.arange(BIAS_ROWS, dtype=I32)[:, None]
    q = jnp.arange(LANES, dtype=I32)[None, :]
    onehot = jax.nn.one_hot(_t5_bucket(v - BIAS_ORIGIN - q), N_BUCKETS, dtype=F32)
    tab = jnp.einsum("vqb,bh->hvq", onehot, rel_bias_table.astype(F32), precision=lax.Precision.HIGHEST)
    return tab * LOG2_E


def _even_in_weight(w):
    glu_v, glu_g, gate_a, q, c, gate_b, qi, ki, wi = jnp.split(
        w, np.cumsum([D_A, D_A, D_A, H_B * DH_B, D_C, H_B * DH_B, H_IDX * D_IDX, D_IDX])[:8].tolist(), axis=1)
    pad = jnp.zeros((w.shape[0], E_NPAD - E_COL_WI - H_IDX), w.dtype)
    cols = [glu_v, glu_g, gate_a, q, gate_b, qi * (D_IDX ** -0.5), c, ki, ki, wi * (H_IDX ** -0.5), pad]
    return jnp.concatenate([col.astype(BF16) for col in cols], axis=1)


def kernel(x, meta_tokens, norm_gain, final_norm_gain, rel_bias_table, w_in_even, conv_w, conv_b,
           conv_ln_gain, conv_ln_bias, kv_norm_gain, w_uk, w_uv, w_out_even, w_in_odd, lb_logits,
           rec_norm_gain, w_out_odd):
    bsz, seq, d = x.shape
    assert d == D_MODEL and seq % LANES == 0
    depth = norm_gain.shape[0]
    tp = FRONT_PAD + N_META + seq
    topk = min(TOPK_MAX, seq // 4)
    tt_mm = tp // 2 if (tp // 2) % 16 == 0 else tp
    tt_out = tp // 4 if (tp // 4) % 16 == 0 else tp
    tt_last = seq // 4
    tt_conv = tp // 4 if (tp // 4) % CONV_RB == 0 else tp // 2 if (tp // 2) % CONV_RB == 0 else tp
    h, hn = embed(x, meta_tokens, norm_gain[0], tt=tt_out)

    lb_soft = jax.nn.softmax(lb_logits.astype(F32), axis=0)
    lower_bounds = jnp.cumsum(lb_soft, axis=0) - lb_soft[0]
    bias_tab = _bias_table(rel_bias_table)

    for layer in range(depth):
        last = layer == depth - 1
        if layer % 2 == 0:
            e = layer // 2
            w_in, tn = _even_in_weight(w_in_even[e]), E_NPAD // 2
        else:
            o = layer // 2
            w_in, tn = w_in_odd[o].astype(BF16), 2048
        proj = matmul(hn, w_in, tt=tt_mm, tn=tn)
        if layer % 2 == 0:
            a_out = conv_module(proj, conv_w[e], conv_b[e], conv_ln_gain[e], conv_ln_bias[e], tt=tt_conv)
            wuk_t = jnp.transpose(w_uk[e], (0, 2, 1)).astype(BF16)
            wuv_t = jnp.transpose(w_uv[e], (0, 2, 1)).astype(BF16)
            b_out = dsa_attention(proj, kv_norm_gain[e], wuk_t, wuv_t, bias_tab, topk=topk)
            mixed, w_out = [a_out, b_out], w_out_even[e].astype(BF16)
        else:
            lb = lower_bounds[layer]
            mixed = [hgrn2(proj, jnp.stack([lb, 1.0 - lb], axis=0), rec_norm_gain[o])]
            w_out = w_out_odd[o].astype(BF16)
        if last:
            return out_proj(mixed, w_out, h, final_norm_gain, tt=tt_last, final=True)
        h, hn = out_proj(mixed, w_out, h, norm_gain[layer + 1], tt=tt_out, final=False)
```

```python
import functools
import math

import numpy as np
import jax
import jax.numpy as jnp
from jax import lax
from jax.experimental import pallas as pl
from jax.experimental.pallas import tpu as pltpu

F32 = jnp.float32
BF16 = jnp.bfloat16
I32 = jnp.int32

D_MODEL = 2048
CHUNK_LOG2 = 6
N_META = 16
D_A = 1024
CONV_WIDTH = 31
H_B = 8
DH_B = 128
D_C = 256
H_IDX = 16
D_IDX = 64
TOPK_MAX = 256
N_BUCKETS = 32
MAX_DISTANCE = 128
H_C = 16
DK_C = 128
RMS_EPS = 1e-6
LN_EPS = 1e-5
NEG_INF = -1e30
INT_MIN = -(2 ** 31)
LOG2_E = 1.4426950408889634
F32_TINY = 1.1754943508222875e-38

I16_MIN = -(2 ** 15)

LANES = 128
SUBLANES = 8
MXU_N = 256

FRONT_PAD = LANES - N_META
QB = LANES
KB = MXU_N
HC = 128
N_LEVELS = 7

E_COL_Q = 3 * D_A
E_COL_GB = E_COL_Q + H_B * DH_B
E_COL_QI = E_COL_GB + H_B * DH_B
E_COL_C = E_COL_QI + H_IDX * D_IDX
E_COL_KK = E_COL_C + D_C
E_COL_WI = E_COL_KK + 2 * D_IDX
E_NPAD = E_COL_WI + LANES
assert 2 * D_IDX == LANES and E_NPAD % MXU_N == 0

MIB = 1024 * 1024
VMEM_PROJ = 56
VMEM_MIXER = 40


def _cparams(sem, vmem_mib):
    return pltpu.CompilerParams(dimension_semantics=sem, vmem_limit_bytes=vmem_mib * MIB)


def _sigmoid(x):
    return 1.0 / (1.0 + jnp.exp(-x))


def _silu(x):
    h = 0.5 * x
    return h + h * jnp.tanh(h)


_NN = (((1,), (0,)), ((), ()))
_NT = (((1,), (1,)), ((), ()))


def _dot(a, b, dims=_NN):
    return lax.dot_general(a, b, dims, preferred_element_type=F32)


def _rms_normed(x, gain):
    ms = jnp.mean(x * x, axis=-1, keepdims=True)
    return x * lax.rsqrt(ms + RMS_EPS) * gain


def _embed_kernel(x_ref, meta_ref, g_ref, h_ref, hn_ref, *, tt):
    first = FRONT_PAD + N_META

    @pl.when(pl.program_id(1) == 0)
    def _():
        d = h_ref.shape[-1]
        meta = meta_ref[...]
        x = x_ref[0, 0:tt - first, :]
        h_ref[0:FRONT_PAD, :] = jnp.zeros((FRONT_PAD, d), F32)
        h_ref[FRONT_PAD:first, :] = meta
        h_ref[first:, :] = x
        hn_ref[0:FRONT_PAD, :] = jnp.zeros((FRONT_PAD, d), BF16)
        hn_ref[FRONT_PAD:first, :] = _rms_normed(meta, g_ref[...]).astype(BF16)
        hn_ref[first:, :] = _rms_normed(x, g_ref[...]).astype(BF16)

    @pl.when(pl.program_id(1) > 0)
    def _():
        x = x_ref[0]
        h_ref[...] = x
        hn_ref[...] = _rms_normed(x, g_ref[...]).astype(BF16)


def embed(x, meta_tokens, gain, *, tt):
    bsz, seq, d = x.shape
    first = FRONT_PAD + N_META
    tp = first + seq
    assert tt > first and tp % tt == 0
    x_spec = pl.BlockSpec((pl.Element(1), pl.Element(tt), pl.Element(d)),
                          lambda b, j: (b, pl.multiple_of(jnp.maximum(j * tt - first, 0), SUBLANES), 0))
    spec = pl.BlockSpec((None, tt, d), lambda b, j: (b, j, 0))
    return pl.pallas_call(
        functools.partial(_embed_kernel, tt=tt),
        grid=(bsz, tp // tt),
        in_specs=[
            x_spec,
            pl.BlockSpec((N_META, d), lambda b, j: (0, 0)),
            pl.BlockSpec((1, d), lambda b, j: (0, 0)),
        ],
        out_specs=[spec, spec],
        out_shape=[jax.ShapeDtypeStruct((bsz, tp, d), F32), jax.ShapeDtypeStruct((bsz, tp, d), BF16)],
        compiler_params=_cparams(("parallel", "parallel"), VMEM_MIXER),
        name="embed",
    )(x.astype(F32), meta_tokens.astype(F32), gain.reshape(1, d).astype(F32))


def _matmul_kernel(x_ref, w_ref, o_ref):
    o_ref[...] = _dot(x_ref[...], w_ref[...]).astype(o_ref.dtype)


def matmul(hn, w, *, tt, tn):
    bsz, tp, d = hn.shape
    n = w.shape[1]
    return pl.pallas_call(
        _matmul_kernel,
        grid=(bsz, tp // tt, n // tn),
        in_specs=[
            pl.BlockSpec((None, tt, d), lambda b, i, j: (b, i, 0)),
            pl.BlockSpec((d, tn), lambda b, i, j: (0, j)),
        ],
        out_specs=pl.BlockSpec((None, tt, tn), lambda b, i, j: (b, i, j)),
        out_shape=jax.ShapeDtypeStruct((bsz, tp, n), BF16),
        compiler_params=_cparams(("parallel", "parallel", "parallel"), VMEM_PROJ),
        name="in_proj",
    )(hn, w)


def _out_proj_kernel(*refs, n_in, tt, final):
    xs = refs[:n_in]
    w_ref, h_ref, g_ref = refs[n_in], refs[n_in + 1], refs[n_in + 2]
    tile = lambda ref: ref[0] if len(ref.shape) == 3 else ref[...]
    y = None
    off = 0
    for x_ref in xs:
        kdim = x_ref.shape[-1]
        part = _dot(tile(x_ref), w_ref[off:off + kdim, :])
        y = part if y is None else y + part
        off += kdim
    h_new = tile(h_ref) + y
    if final:
        refs[-1][...] = _rms_normed(h_new, g_ref[...])
    else:
        row = pl.program_id(1) * tt + lax.broadcasted_iota(I32, (tt, 1), 0)
        h_new = jnp.where(row >= FRONT_PAD, h_new, 0.0)
        refs[-2][...] = h_new
        refs[-1][...] = _rms_normed(h_new, g_ref[...]).astype(BF16)


def out_proj(xs, w, h, gain, *, tt, final):
    bsz, tp, d = h.shape
    if final:
        first = FRONT_PAD + N_META
        rows_out = tp - first
        row_spec = lambda k: pl.BlockSpec((pl.Element(1), pl.Element(tt), pl.Element(k)),
                                          lambda b, i: (b, pl.multiple_of(first + i * tt, LANES), 0))
    else:
        rows_out = tp
        row_spec = lambda k: pl.BlockSpec((None, tt, k), lambda b, i: (b, i, 0))
    in_specs = [row_spec(x.shape[-1]) for x in xs]
    in_specs += [pl.BlockSpec(w.shape, lambda b, i: (0, 0)), row_spec(d), pl.BlockSpec((1, d), lambda b, i: (0, 0))]
    out_spec = pl.BlockSpec((None, tt, d), lambda b, i: (b, i, 0))
    out_f32 = jax.ShapeDtypeStruct((bsz, rows_out, d), F32)
    return pl.pallas_call(
        functools.partial(_out_proj_kernel, n_in=len(xs), tt=tt, final=final),
        grid=(bsz, rows_out // tt),
        in_specs=in_specs,
        out_specs=out_spec if final else [out_spec, out_spec],
        out_shape=out_f32 if final else [out_f32, jax.ShapeDtypeStruct((bsz, rows_out, d), BF16)],
        compiler_params=_cparams(("parallel", "parallel"), VMEM_PROJ),
        name="out_proj_final" if final else "out_proj",
    )(*xs, w, h, gain.reshape(1, d).astype(F32))


CONV_HALO = 32
CONV_RB = 16
CONV_CHAINS = 4


def _conv_kernel(v_ref, g_ref, ga_ref, w_ref, cb_ref, lg_ref, lb_ref, o_ref, us_ref, cv_ref, *, tt):
    j = pl.program_id(1)
    span = tt + CONV_HALO - SUBLANES
    n_rb = tt // CONV_RB
    for cb in range(D_A // LANES):
        lanes = slice(cb * LANES, (cb + 1) * LANES)

        @pl.when(j == 0)
        def _(cb=cb):
            us_ref[0, cb, 0:CONV_HALO, :] = jnp.zeros((CONV_HALO, LANES), F32)

        @pl.when(j > 0)
        def _(cb=cb):
            us_ref[0, cb, 0:CONV_HALO, :] = us_ref[0, cb, tt:tt + CONV_HALO, :]

        us_ref[0, cb, CONV_HALO:CONV_HALO + tt, :] = (
            v_ref[:, lanes].astype(F32) * _sigmoid(g_ref[:, lanes].astype(F32)))
        for r in range(1, SUBLANES):
            us_ref[r, cb, 0:span, :] = us_ref[0, cb, r:r + span, :]

        wb = [jnp.broadcast_to(w_ref[k:k + 1, lanes], (CONV_RB, LANES)) for k in range(CONV_WIDTH)]
        bias = jnp.broadcast_to(cb_ref[0:1, lanes], (CONV_RB, LANES))

        def rb_body(rb, carry, cb=cb, lanes=lanes, wb=wb, bias=bias):
            base = pl.multiple_of(rb * CONV_RB, CONV_RB)
            parts = [bias] + [None] * (CONV_CHAINS - 1)
            for k in range(CONV_WIDTH):
                a, r = divmod(k + CONV_HALO - (CONV_WIDTH - 1), SUBLANES)
                term = wb[k] * us_ref[r, cb, pl.ds(base + SUBLANES * a, CONV_RB), :]
                c = k % CONV_CHAINS
                parts[c] = term if parts[c] is None else parts[c] + term
            cv_ref[pl.ds(base, CONV_RB), lanes] = (parts[0] + parts[1]) + (parts[2] + parts[3])
            return carry

        lax.fori_loop(0, n_rb, rb_body, 0)

    ln_gain, ln_bias = lg_ref[...], lb_ref[...]
    ln_rows = tt // 2 if (tt // 2) % 16 == 0 else tt
    for lo in range(0, tt, ln_rows):
        x = cv_ref[lo:lo + ln_rows, :]
        xc = x - jnp.mean(x, axis=-1, keepdims=True)
        var = jnp.mean(xc * xc, axis=-1, keepdims=True)
        y = xc * lax.rsqrt(var + LN_EPS) * ln_gain + ln_bias
        ga = ga_ref[lo:lo + ln_rows, :].astype(F32)
        o_ref[lo:lo + ln_rows, :] = (_silu(y) * _silu(ga)).astype(o_ref.dtype)


def conv_module(proj, conv_w, conv_b, ln_g, ln_b, *, tt):
    bsz, tp, _ = proj.shape
    row = lambda a: a.reshape(1, D_A).astype(F32)
    return pl.pallas_call(
        functools.partial(_conv_kernel, tt=tt),
        grid=(bsz, tp // tt),
        in_specs=[
            pl.BlockSpec((None, tt, D_A), lambda b, j: (b, j, 0)),
            pl.BlockSpec((None, tt, D_A), lambda b, j: (b, j, 1)),
            pl.BlockSpec((None, tt, D_A), lambda b, j: (b, j, 2)),
            pl.BlockSpec((CONV_WIDTH, D_A), lambda b, j: (0, 0)),
            pl.BlockSpec((1, D_A), lambda b, j: (0, 0)),
            pl.BlockSpec((1, D_A), lambda b, j: (0, 0)),
            pl.BlockSpec((1, D_A), lambda b, j: (0, 0)),
        ],
        out_specs=pl.BlockSpec((None, tt, D_A), lambda b, j: (b, j, 0)),
        out_shape=jax.ShapeDtypeStruct((bsz, tp, D_A), BF16),
        scratch_shapes=[
            pltpu.VMEM((SUBLANES, D_A // LANES, tt + CONV_HALO, LANES), F32),
            pltpu.VMEM((tt, D_A), F32),
        ],
        compiler_params=_cparams(("parallel", "arbitrary"), VMEM_MIXER),
        name="conv_module",
    )(proj, proj, proj, conv_w.astype(F32), row(conv_b), row(ln_g), row(ln_b))


BIAS_ORIGIN = KB + QB
BIAS_ROWS = BIAS_ORIGIN + KB


def _attn_kernel(c_ref, kk_ref, wi_ref, q_ref, qi_ref, gb_ref, kvg_ref, wuk_ref, wuvt_ref, bias_ref,
                 o_ref, cn_s, cnt_s, kk_s, st_s, hi_s, lo_s, qlat_s, lg_a, lg_b, acc_s, m_s, l_s,
                 tau_s, *, tp, topk):
    i = pl.program_id(1)
    n_kb = (i + 2) // 2
    n_kb_all = cn_s.shape[0]

    @pl.when(i == 0)
    def _prepare_keys():
        gain = kvg_ref[...]
        for kb in range(n_kb_all):
            lo = kb * KB
            nrow = min(KB, tp - lo)
            c = c_ref[lo:lo + nrow, :].astype(F32)
            cn = c * lax.rsqrt(jnp.mean(c * c, axis=-1, keepdims=True) + RMS_EPS) * gain
            cn_s[kb, 0:nrow, :] = cn.astype(BF16)
            kk = kk_ref[lo:lo + nrow, :]
            lane = lax.broadcasted_iota(I32, kk.shape, 1)
            kk_s[0, kb, 0:nrow, :] = jnp.where(lane < D_IDX, kk, jnp.zeros_like(kk))
            kk_s[1, kb, 0:nrow, :] = jnp.where(lane >= D_IDX, kk, jnp.zeros_like(kk))
            for rb in range(nrow // LANES):
                for cb in range(D_C // LANES):
                    tile = cn[rb * LANES:(rb + 1) * LANES, cb * LANES:(cb + 1) * LANES]
                    cnt_s[kb, cb * LANES:(cb + 1) * LANES, rb * LANES:(rb + 1) * LANES] = tile.T.astype(BF16)
            if nrow < KB:
                cn_s[kb, nrow:KB, :] = jnp.zeros((KB - nrow, D_C), BF16)
                kk_s[:, kb, nrow:KB, :] = jnp.zeros((2, KB - nrow, LANES), BF16)
                cnt_s[kb, :, nrow:KB] = jnp.zeros((D_C, KB - nrow), BF16)

    for h in range(H_B):
        qh = q_ref[:, h * DH_B:(h + 1) * DH_B]
        ql = _dot(qh, wuk_ref[h]) * (DH_B ** -0.5 * LOG2_E)
        qlat_s[h * QB:(h + 1) * QB, :] = ql.astype(BF16)

    wit = wi_ref[...].astype(F32).T

    def score_block(kb):
        k_even = kk_s[0, kb]
        k_odd = kk_s[1, kb]
        acc = jnp.zeros((KB, QB), F32)
        for p in range(H_IDX // 2):
            qp = qi_ref[:, p * LANES:(p + 1) * LANES]
            acc = acc + jnp.maximum(_dot(k_even, qp, _NT), 0.0) * wit[2 * p:2 * p + 1, :]
            acc = acc + jnp.maximum(_dot(k_odd, qp, _NT), 0.0) * wit[2 * p + 1:2 * p + 2, :]
        bits = lax.bitcast_convert_type(acc, I32)
        key = bits ^ ((bits >> 31) & 0x7FFFFFFF)
        s_idx = kb * KB + lax.broadcasted_iota(I32, (KB, QB), 0)
        t_idx = i * QB + lax.broadcasted_iota(I32, (KB, QB), 1)
        adm = (s_idx >= FRONT_PAD) & (jnp.maximum(s_idx >> CHUNK_LOG2, 1) <= jnp.maximum(t_idx >> CHUNK_LOG2, 1))
        key = jnp.where(adm, key, INT_MIN)
        st_s[kb] = key
        hi_s[kb] = key >> 16

    def score_pair(j, carry):
        score_block(2 * j)
        score_block(jnp.minimum(2 * j + 1, n_kb_all - 1))
        return carry

    lax.fori_loop(0, (n_kb + 1) // 2, score_pair, 0)

    def search16(ref, need, n_blocks):
        def bit_body(it, carry):
            t, above = carry
            cand = t + lax.shift_left(jnp.int32(1), jnp.int32(15) - it)
            cand8 = jnp.broadcast_to(cand, (SUBLANES, QB))[None]
            parts = []
            for kb in range(n_blocks):
                below = (ref[kb].reshape(KB // SUBLANES, SUBLANES, QB) - cand8) >> 31
                parts += [below[g] for g in range(KB // SUBLANES)]
            while len(parts) > 1:
                nxt = [parts[g] + parts[g + 1] for g in range(0, len(parts) - 1, 2)]
                parts = nxt + parts[len(parts) - len(parts) % 2:]
            cnt = n_blocks * KB + jnp.sum(parts[0], axis=0, keepdims=True)
            ok = cnt >= need
            return jnp.where(ok, cand, t), jnp.where(ok, above, cnt)

        init = (jnp.full((1, QB), I16_MIN, I32), jnp.zeros((1, QB), I32))
        return lax.fori_loop(0, 16, bit_body, init)

    st_s[n_kb] = jnp.full((KB, QB), INT_MIN, I32)
    hi_s[n_kb] = jnp.full((KB, QB), I16_MIN, I32)
    n_pairs = (n_kb + 1) // 2
    few_keys = N_META + QB * i <= topk

    @pl.when(few_keys)
    def _():
        tau_s[...] = jnp.full((1, QB), INT_MIN, I32)

    for pairs in range(1, (n_kb_all + 1) // 2 + 1):
        @pl.when(jnp.logical_and(n_pairs == pairs, jnp.logical_not(few_keys)))
        def _(n_blocks=2 * pairs):
            tau_hi, n_above = search16(hi_s, jnp.full((1, QB), topk, I32), n_blocks)
            for kb in range(n_blocks):
                key = st_s[kb]
                low = (key & 0xFFFF) + I16_MIN
                lo_s[kb] = jnp.where((key >> 16) == tau_hi, low, I16_MIN)
            tau_lo, _ = search16(lo_s, topk - n_above, n_blocks)
            tau_s[...] = lax.shift_left(tau_hi, jnp.int32(16)) | (tau_lo - I16_MIN)

    tau = jnp.maximum(tau_s[...], INT_MIN + 1)

    m_s[...] = jnp.full(m_s.shape, NEG_INF, F32)
    l_s[...] = jnp.zeros(l_s.shape, F32)
    acc_s[...] = jnp.zeros(acc_s.shape, F32)

    def logits_stage(kb, lg_ref):
        lg_ref[...] = _dot(cn_s[kb], qlat_s[...], _NT)

    def softmax_stage(kb, lg_ref):
        neg = jnp.where(st_s[kb] >= tau, 0.0, NEG_INF)
        v0 = pl.multiple_of(jnp.maximum(KB * kb - QB * i + BIAS_ORIGIN, 0), LANES)
        ps, alphas = [], []
        for h in range(H_B):
            lh = lg_ref[:, h * QB:(h + 1) * QB] + bias_ref[h, pl.ds(v0, KB), :] + neg
            m_old = m_s[h:h + 1, :]
            m_new = jnp.maximum(m_old, jnp.max(lh, axis=0, keepdims=True))
            p = jnp.exp2(lh - m_new)
            alpha = jnp.exp2(m_old - m_new)
            l_s[h:h + 1, :] = alpha * l_s[h:h + 1, :] + jnp.sum(p, axis=0, keepdims=True)
            m_s[h:h + 1, :] = m_new
            ps.append(p.astype(BF16))
            alphas.append(alpha)
        pv = _dot(cnt_s[kb], jnp.concatenate(ps, axis=1))
        acc_s[...] = acc_s[...] * jnp.concatenate(alphas, axis=1) + pv

    logits_stage(0, lg_a)

    def att_pair(j, carry):
        kb = 2 * j
        logits_stage(kb + 1, lg_b)
        softmax_stage(kb, lg_a)
        logits_stage(jnp.minimum(kb + 2, n_kb - 1), lg_a)
        softmax_stage(kb + 1, lg_b)
        return carry

    lax.fori_loop(0, n_kb // 2, att_pair, 0)

    @pl.when(n_kb % 2 == 1)
    def _():
        softmax_stage(n_kb - 1, lg_a)

    for h in range(H_B):
        cols = slice(h * QB, (h + 1) * QB)
        ot = (acc_s[:, cols] * (1.0 / l_s[h:h + 1, :])).astype(BF16)
        bt = _dot(wuvt_ref[h], ot)
        gate = gb_ref[:, cols].astype(F32)
        o_ref[:, cols] = (bt.T * _silu(gate)).astype(o_ref.dtype)


def dsa_attention(proj, kv_gain, wuk_t, wuv_t, bias_tab, *, topk):
    bsz, tp, _ = proj.shape
    n_qb = tp // QB
    n_kb_all = -(-tp // KB)
    return pl.pallas_call(
        functools.partial(_attn_kernel, tp=tp, topk=topk),
        grid=(bsz, n_qb),
        in_specs=[
            pl.BlockSpec((None, tp, D_C), lambda b, i: (b, 0, E_COL_C // D_C)),
            pl.BlockSpec((None, tp, LANES), lambda b, i: (b, 0, E_COL_KK // LANES)),
            pl.BlockSpec((None, QB, LANES), lambda b, i: (b, i, E_COL_WI // LANES)),
            pl.BlockSpec((None, QB, H_B * DH_B), lambda b, i: (b, i, E_COL_Q // (H_B * DH_B))),
            pl.BlockSpec((None, QB, H_IDX * D_IDX), lambda b, i: (b, i, E_COL_QI // (H_IDX * D_IDX))),
            pl.BlockSpec((None, QB, H_B * DH_B), lambda b, i: (b, i, E_COL_GB // (H_B * DH_B))),
            pl.BlockSpec((1, D_C), lambda b, i: (0, 0)),
            pl.BlockSpec((H_B, DH_B, D_C), lambda b, i: (0, 0, 0)),
            pl.BlockSpec((H_B, DH_B, D_C), lambda b, i: (0, 0, 0)),
            pl.BlockSpec((H_B, BIAS_ROWS, LANES), lambda b, i: (0, 0, 0)),
        ],
        out_specs=pl.BlockSpec((None, QB, H_B * DH_B), lambda b, i: (b, i, 0)),
        out_shape=jax.ShapeDtypeStruct((bsz, tp, H_B * DH_B), BF16),
        scratch_shapes=[
            pltpu.VMEM((n_kb_all, KB, D_C), BF16),
            pltpu.VMEM((n_kb_all, D_C, KB), BF16),
            pltpu.VMEM((2, n_kb_all, KB, LANES), BF16),
            pltpu.VMEM((n_kb_all + 1, KB, QB), I32),
            pltpu.VMEM((n_kb_all + 1, KB, QB), I32),
            pltpu.VMEM((n_kb_all + 1, KB, QB), I32),
            pltpu.VMEM((H_B * QB, D_C), BF16),
            pltpu.VMEM((KB, H_B * QB), F32),
            pltpu.VMEM((KB, H_B * QB), F32),
            pltpu.VMEM((D_C, H_B * QB), F32),
            pltpu.VMEM((H_B, QB), F32),
            pltpu.VMEM((H_B, QB), F32),
            pltpu.VMEM((1, QB), I32),
        ],
        compiler_params=_cparams(("parallel", "arbitrary"), VMEM_MIXER),
        name="dsa_attention",
    )(proj, proj, proj, proj, proj, proj, kv_gain.reshape(1, D_C).astype(F32), wuk_t, wuv_t, bias_tab)


def _hgrn_constants():
    n = HC
    t = np.arange(n)[:, None]
    j = np.arange(n)[None, :]
    mats = [(j <= t)]
    for lvl in range(N_LEVELS):
        upper = ((t >> lvl) & 1) == 1
        m_up = (t >> lvl) << lvl
        m_lo = ((t >> lvl) + 1) << lvl
        mats.append(np.where(upper, (j >= m_up) & (j <= t), (j > t) & (j < m_lo)))
    m = np.concatenate(mats, axis=0).astype(np.float32)
    mcat = np.concatenate([m, m], axis=1)
    x = t ^ j
    lev = np.where(j < t, np.floor(np.log2(np.maximum(x, 1))).astype(np.int32), np.where(j == t, -1, -2))
    masks = np.stack([(lev == lvl) for lvl in range(-1, N_LEVELS)]).astype(np.float32)
    return mcat, masks


HGRN_HEADS_PER_STEP = 4


def _role_select(qf, k, lvl, up_small):
    half = 1 << lvl
    if half >= SUBLANES:
        pieces = []
        for blk in range(HC // (2 * half)):
            lo = 2 * half * blk
            pieces += [k[lo:lo + half], qf[lo + half:lo + 2 * half]]
        return jnp.concatenate(pieces, axis=0)
    g, w = HC // SUBLANES, qf.shape[-1]
    sel = jnp.where(up_small[lvl], qf.reshape(g, SUBLANES, w), k.reshape(g, SUBLANES, w))
    return sel.reshape(HC, w)


_SLOT_QE, _SLOT_KT, _SLOT_QB, _SLOT_KB, _SLOT_Y0 = 0, 1, 2, 3, 4
_N_SLOTS = _SLOT_Y0 + N_LEVELS


def _hgrn_kernel(q_ref, fz_ref, i_ref, g_ref, lbp_ref, ng_ref, mcat_ref, mask_ref, o_ref,
                 s_ref, stage_a, stage_b, dec_a, dec_b, *, n_steps, nh):
    wide = nh * DK_C
    row8 = lax.broadcasted_iota(I32, (1, SUBLANES, wide), 1)
    up_small = [((row8 >> lvl) & 1) == 1 for lvl in range(3)]
    s_ref[...] = jnp.zeros(s_ref.shape, F32)

    def gate_stage(c):
        rows = pl.ds(pl.multiple_of(c * HC, HC), HC)
        lb = lbp_ref[0:1, :]
        one_m_lb = lbp_ref[1:2, :]
        qf = _silu(q_ref[rows, :].astype(F32))
        z = fz_ref[rows, :].astype(F32)
        w = one_m_lb * (0.5 + 0.5 * jnp.tanh(0.5 * z))
        k = one_m_lb - w
        lf2 = jnp.log(jnp.maximum(lb + w, F32_TINY)) * LOG2_E
        hi = lf2.astype(BF16)
        lo = (lf2 - hi.astype(F32)).astype(BF16)
        return qf, k, _dot(mcat_ref[...], jnp.concatenate([hi, lo], axis=0))

    def operand_stage(gates, stage, dec_ref):
        qf, k, gsum = gates
        b = gsum[0:HC]
        e_b = jnp.exp2(b)
        e_s = jnp.exp2(b[HC - 1:HC, :] - b)
        stage[_SLOT_QE] = (qf * e_b).astype(BF16)
        stage[_SLOT_KT] = (k * e_s).astype(BF16)
        stage[_SLOT_QB] = qf.astype(BF16)
        stage[_SLOT_KB] = k.astype(BF16)
        for lvl in range(N_LEVELS):
            y = _role_select(qf, k, lvl, up_small) * jnp.exp2(gsum[(1 + lvl) * HC:(2 + lvl) * HC])
            stage[_SLOT_Y0 + lvl] = y.astype(BF16)
        dec_ref[...] = e_b[HC - 1:HC, :]

    def level_stage(c, stage):
        lns = [slice(hh * DK_C, (hh + 1) * DK_C) for hh in range(nh)]
        sts = [s_ref[hh] for hh in range(nh)]
        outs = [_dot(stage[_SLOT_QE, :, ln], st.astype(BF16), _NT) for ln, st in zip(lns, sts)]
        accs = [_dot(stage[_SLOT_QB, :, ln], stage[_SLOT_KB, :, ln], _NT) * mask_ref[0] for ln in lns]
        for lvl in range(N_LEVELS):
            for hh, ln in enumerate(lns):
                y = stage[_SLOT_Y0 + lvl, :, ln]
                accs[hh] = accs[hh] + _dot(y, y, _NT) * mask_ref[lvl + 1]
        return sts, outs, accs

    def output_stage(c, partial, stage, dec_ref):
        sts, outs, accs = partial
        rows = pl.ds(pl.multiple_of(c * HC, HC), HC)
        lns = [slice(hh * DK_C, (hh + 1) * DK_C) for hh in range(nh)]
        vhs = [i_ref[rows, ln] for ln in lns]
        for hh, ln in enumerate(lns):
            vt = vhs[hh].astype(F32).T.astype(BF16)
            s_ref[hh] = sts[hh] * dec_ref[:, ln] + _dot(vt, stage[_SLOT_KT, :, ln])
        normed = []
        for hh in range(nh):
            o = outs[hh] + _dot(accs[hh].astype(BF16), vhs[hh])
            normed.append(o * lax.rsqrt(jnp.mean(o * o, axis=-1, keepdims=True) + RMS_EPS))
        on = jnp.concatenate(normed, axis=1) * ng_ref[...]
        gx = g_ref[rows, :].astype(F32)
        o_ref[rows, :] = (on * _silu(gx)).astype(o_ref.dtype)

    last = n_steps - 1
    operand_stage(gate_stage(0), stage_a, dec_a)

    def pair(j, carry):
        c = 2 * j
        partial = level_stage(c, stage_a)
        operand_stage(gate_stage(c + 1), stage_b, dec_b)
        output_stage(c, partial, stage_a, dec_a)
        partial = level_stage(c + 1, stage_b)
        operand_stage(gate_stage(jnp.minimum(c + 2, last)), stage_a, dec_a)
        output_stage(c + 1, partial, stage_b, dec_b)
        return carry

    lax.fori_loop(0, n_steps // 2, pair, 0)
    if n_steps % 2:
        output_stage(last, level_stage(last, stage_a), stage_a, dec_a)


def hgrn2(proj, lb_params, norm_g):
    bsz, tp, _ = proj.shape
    mcat, masks = _hgrn_constants()
    nh = HGRN_HEADS_PER_STEP
    wide = nh * DK_C
    n_groups = H_C // nh
    col = lambda off: pl.BlockSpec((None, tp, wide), lambda b, h, off=off: (b, 0, off + h))
    return pl.pallas_call(
        functools.partial(_hgrn_kernel, n_steps=tp // HC, nh=nh),
        grid=(bsz, n_groups),
        in_specs=[
            col(0), col(n_groups), col(2 * n_groups), col(3 * n_groups),
            pl.BlockSpec((2, wide), lambda b, h: (0, h)),
            pl.BlockSpec((1, wide), lambda b, h: (0, h)),
            pl.BlockSpec(mcat.shape, lambda b, h: (0, 0)),
            pl.BlockSpec(masks.shape, lambda b, h: (0, 0, 0)),
        ],
        out_specs=pl.BlockSpec((None, tp, wide), lambda b, h: (b, 0, h)),
        out_shape=jax.ShapeDtypeStruct((bsz, tp, H_C * DK_C), BF16),
        scratch_shapes=[
            pltpu.VMEM((nh, DK_C, DK_C), F32),
            pltpu.VMEM((_N_SLOTS, HC, wide), BF16),
            pltpu.VMEM((_N_SLOTS, HC, wide), BF16),
            pltpu.VMEM((1, wide), F32),
            pltpu.VMEM((1, wide), F32),
        ],
        compiler_params=_cparams(("parallel", "parallel"), VMEM_MIXER),
        name="hgrn2",
    )(proj, proj, proj, proj, lb_params, norm_g.reshape(1, -1).astype(F32),
      jnp.asarray(mcat, BF16), jnp.asarray(masks, F32))


def _t5_bucket(rel):
    nb = N_BUCKETS // 2
    ret = jnp.where(rel > 0, nb, 0)
    n = jnp.abs(rel)
    max_exact = nb // 2
    nf = jnp.maximum(n, 1).astype(F32)
    large = max_exact + (jnp.log(nf / max_exact) / math.log(MAX_DISTANCE / max_exact)
                         * (nb - max_exact)).astype(I32)
    large = jnp.minimum(large, nb - 1)
    return ret + jnp.where(n < max_exact, n, large)


def _bias_table(rel_bias_table):
    v = jnp.arange(BIAS_ROWS, dtype=I32)[:, None]
    q = jnp.arange(LANES, dtype=I32)[None, :]
    onehot = jax.nn.one_hot(_t5_bucket(v - BIAS_ORIGIN - q), N_BUCKETS, dtype=F32)
    tab = jnp.einsum("vqb,bh->hvq", onehot, rel_bias_table.astype(F32), precision=lax.Precision.HIGHEST)
    return tab * LOG2_E


def _even_in_weight(w):
    glu_v, glu_g, gate_a, q, c, gate_b, qi, ki, wi = jnp.split(
        w, np.cumsum([D_A, D_A, D_A, H_B * DH_B, D_C, H_B * DH_B, H_IDX * D_IDX, D_IDX])[:8].tolist(), axis=1)
    pad = jnp.zeros((w.shape[0], E_NPAD - E_COL_WI - H_IDX), w.dtype)
    cols = [glu_v, glu_g, gate_a, q, gate_b, qi * (D_IDX ** -0.5), c, ki, ki, wi * (H_IDX ** -0.5), pad]
    return jnp.concatenate([col.astype(BF16) for col in cols], axis=1)


def kernel(x, meta_tokens, norm_gain, final_norm_gain, rel_bias_table, w_in_even, conv_w, conv_b,
           conv_ln_gain, conv_ln_bias, kv_norm_gain, w_uk, w_uv, w_out_even, w_in_odd, lb_logits,
           rec_norm_gain, w_out_odd):
    bsz, seq, d = x.shape
    assert d == D_MODEL and seq % LANES == 0
    depth = norm_gain.shape[0]
    tp = FRONT_PAD + N_META + seq
    topk = min(TOPK_MAX, seq // 4)
    tt_mm = tp // 2 if (tp // 2) % 16 == 0 else tp
    tt_out = tp // 4 if (tp // 4) % 16 == 0 else tp
    tt_last = seq // 4
    tt_conv = tp // 4 if (tp // 4) % CONV_RB == 0 else tp // 2 if (tp // 2) % CONV_RB == 0 else tp
    h, hn = embed(x, meta_tokens, norm_gain[0], tt=tt_out)

    lb_soft = jax.nn.softmax(lb_logits.astype(F32), axis=0)
    lower_bounds = jnp.cumsum(lb_soft, axis=0) - lb_soft[0]
    bias_tab = _bias_table(rel_bias_table)

    for layer in range(depth):
        last = layer == depth - 1
        if layer % 2 == 0:
            e = layer // 2
            w_in, tn = _even_in_weight(w_in_even[e]), E_NPAD // 2
        else:
            o = layer // 2
            w_in, tn = w_in_odd[o].astype(BF16), 2048
        proj = matmul(hn, w_in, tt=tt_mm, tn=tn)
        if layer % 2 == 0:
            a_out = conv_module(proj, conv_w[e], conv_b[e], conv_ln_gain[e], conv_ln_bias[e], tt=tt_conv)
            wuk_t = jnp.transpose(w_uk[e], (0, 2, 1)).astype(BF16)
            wuv_t = jnp.transpose(w_uv[e], (0, 2, 1)).astype(BF16)
            b_out = dsa_attention(proj, kv_norm_gain[e], wuk_t, wuv_t, bias_tab, topk=topk)
            mixed, w_out = [a_out, b_out], w_out_even[e].astype(BF16)
        else:
            lb = lower_bounds[layer]
            mixed = [hgrn2(proj, jnp.stack([lb, 1.0 - lb], axis=0), rec_norm_gain[o])]
            w_out = w_out_odd[o].astype(BF16)
        if last:
            return out_proj(mixed, w_out, h, final_norm_gain, tt=tt_last, final=True)
        h, hn = out_proj(mixed, w_out, h, norm_gain[layer + 1], tt=tt_out, final=False)
```

```python
import functools
import math

import numpy as np
import jax
import jax.numpy as jnp
from jax import lax
from jax.experimental import pallas as pl
from jax.experimental.pallas import tpu as pltpu

F32 = jnp.float32
BF16 = jnp.bfloat16
I32 = jnp.int32

D_MODEL = 2048
CHUNK_LOG2 = 6
N_META = 16
D_A = 1024
CONV_WIDTH = 31
H_B = 8
DH_B = 128
D_C = 256
H_IDX = 16
D_IDX = 64
TOPK_MAX = 256
N_BUCKETS = 32
MAX_DISTANCE = 128
H_C = 16
DK_C = 128
RMS_EPS = 1e-6
LN_EPS = 1e-5
NEG_INF = -1e30
INT_MIN = -(2 ** 31)
LOG2_E = 1.4426950408889634
F32_TINY = 1.1754943508222875e-38

I16_MIN = -(2 ** 15)

LANES = 128
SUBLANES = 8
MXU_N = 256

FRONT_PAD = LANES - N_META
QB = LANES
KB = MXU_N
HC = 128
N_LEVELS = 7

E_COL_Q = 3 * D_A
E_COL_GB = E_COL_Q + H_B * DH_B
E_COL_QI = E_COL_GB + H_B * DH_B
E_COL_C = E_COL_QI + H_IDX * D_IDX
E_COL_KK = E_COL_C + D_C
E_COL_WI = E_COL_KK + 2 * D_IDX
E_NPAD = E_COL_WI + LANES
assert 2 * D_IDX == LANES and E_NPAD % MXU_N == 0

MIB = 1024 * 1024
VMEM_PROJ = 56
VMEM_MIXER = 40


def _cparams(sem, vmem_mib):
    return pltpu.CompilerParams(dimension_semantics=sem, vmem_limit_bytes=vmem_mib * MIB)


def _sigmoid(x):
    return 1.0 / (1.0 + jnp.exp(-x))


def _silu(x):
    h = 0.5 * x
    return h + h * jnp.tanh(h)


_NN = (((1,), (0,)), ((), ()))
_NT = (((1,), (1,)), ((), ()))


def _dot(a, b, dims=_NN):
    return lax.dot_general(a, b, dims, preferred_element_type=F32)


def _rms_normed(x, gain):
    ms = jnp.mean(x * x, axis=-1, keepdims=True)
    return x * lax.rsqrt(ms + RMS_EPS) * gain


def _embed_kernel(x_ref, meta_ref, g_ref, h_ref, hn_ref, *, tt):
    first = FRONT_PAD + N_META

    @pl.when(pl.program_id(1) == 0)
    def _():
        d = h_ref.shape[-1]
        meta = meta_ref[...]
        x = x_ref[0, 0:tt - first, :]
        h_ref[0:FRONT_PAD, :] = jnp.zeros((FRONT_PAD, d), F32)
        h_ref[FRONT_PAD:first, :] = meta
        h_ref[first:, :] = x
        hn_ref[0:FRONT_PAD, :] = jnp.zeros((FRONT_PAD, d), BF16)
        hn_ref[FRONT_PAD:first, :] = _rms_normed(meta, g_ref[...]).astype(BF16)
        hn_ref[first:, :] = _rms_normed(x, g_ref[...]).astype(BF16)

    @pl.when(pl.program_id(1) > 0)
    def _():
        x = x_ref[0]
        h_ref[...] = x
        hn_ref[...] = _rms_normed(x, g_ref[...]).astype(BF16)


def embed(x, meta_tokens, gain, *, tt):
    bsz, seq, d = x.shape
    first = FRONT_PAD + N_META
    tp = first + seq
    assert tt > first and tp % tt == 0
    x_spec = pl.BlockSpec((pl.Element(1), pl.Element(tt), pl.Element(d)),
                          lambda b, j: (b, pl.multiple_of(jnp.maximum(j * tt - first, 0), SUBLANES), 0))
    spec = pl.BlockSpec((None, tt, d), lambda b, j: (b, j, 0))
    return pl.pallas_call(
        functools.partial(_embed_kernel, tt=tt),
        grid=(bsz, tp // tt),
        in_specs=[
            x_spec,
            pl.BlockSpec((N_META, d), lambda b, j: (0, 0)),
            pl.BlockSpec((1, d), lambda b, j: (0, 0)),
        ],
        out_specs=[spec, spec],
        out_shape=[jax.ShapeDtypeStruct((bsz, tp, d), F32), jax.ShapeDtypeStruct((bsz, tp, d), BF16)],
        compiler_params=_cparams(("parallel", "parallel"), VMEM_MIXER),
        name="embed",
    )(x.astype(F32), meta_tokens.astype(F32), gain.reshape(1, d).astype(F32))


def _matmul_kernel(x_ref, w_ref, o_ref):
    o_ref[...] = _dot(x_ref[...], w_ref[...]).astype(o_ref.dtype)


def matmul(hn, w, *, tt, tn):
    bsz, tp, d = hn.shape
    n = w.shape[1]
    return pl.pallas_call(
        _matmul_kernel,
        grid=(bsz, tp // tt, n // tn),
        in_specs=[
            pl.BlockSpec((None, tt, d), lambda b, i, j: (b, i, 0)),
            pl.BlockSpec((d, tn), lambda b, i, j: (0, j)),
        ],
        out_specs=pl.BlockSpec((None, tt, tn), lambda b, i, j: (b, i, j)),
        out_shape=jax.ShapeDtypeStruct((bsz, tp, n), BF16),
        compiler_params=_cparams(("parallel", "parallel", "parallel"), VMEM_PROJ),
        name="in_proj",
    )(hn, w)


def _out_proj_kernel(*refs, n_in, tt, final):
    xs = refs[:n_in]
    w_ref, h_ref, g_ref = refs[n_in], refs[n_in + 1], refs[n_in + 2]
    tile = lambda ref: ref[0] if len(ref.shape) == 3 else ref[...]
    y = None
    off = 0
    for x_ref in xs:
        kdim = x_ref.shape[-1]
        part = _dot(tile(x_ref), w_ref[off:off + kdim, :])
        y = part if y is None else y + part
        off += kdim
    h_new = tile(h_ref) + y
    if final:
        refs[-1][...] = _rms_normed(h_new, g_ref[...])
    else:
        row = pl.program_id(1) * tt + lax.broadcasted_iota(I32, (tt, 1), 0)
        h_new = jnp.where(row >= FRONT_PAD, h_new, 0.0)
        refs[-2][...] = h_new
        refs[-1][...] = _rms_normed(h_new, g_ref[...]).astype(BF16)


def out_proj(xs, w, h, gain, *, tt, final):
    bsz, tp, d = h.shape
    if final:
        first = FRONT_PAD + N_META
        rows_out = tp - first
        row_spec = lambda k: pl.BlockSpec((pl.Element(1), pl.Element(tt), pl.Element(k)),
                                          lambda b, i: (b, pl.multiple_of(first + i * tt, LANES), 0))
    else:
        rows_out = tp
        row_spec = lambda k: pl.BlockSpec((None, tt, k), lambda b, i: (b, i, 0))
    in_specs = [row_spec(x.shape[-1]) for x in xs]
    in_specs += [pl.BlockSpec(w.shape, lambda b, i: (0, 0)), row_spec(d), pl.BlockSpec((1, d), lambda b, i: (0, 0))]
    out_spec = pl.BlockSpec((None, tt, d), lambda b, i: (b, i, 0))
    out_f32 = jax.ShapeDtypeStruct((bsz, rows_out, d), F32)
    return pl.pallas_call(
        functools.partial(_out_proj_kernel, n_in=len(xs), tt=tt, final=final),
        grid=(bsz, rows_out // tt),
        in_specs=in_specs,
        out_specs=out_spec if final else [out_spec, out_spec],
        out_shape=out_f32 if final else [out_f32, jax.ShapeDtypeStruct((bsz, rows_out, d), BF16)],
        compiler_params=_cparams(("parallel", "parallel"), VMEM_PROJ),
        name="out_proj_final" if final else "out_proj",
    )(*xs, w, h, gain.reshape(1, d).astype(F32))


CONV_HALO = 32
CONV_RB = 16
CONV_CHAINS = 4


def _conv_kernel(v_ref, g_ref, ga_ref, w_ref, cb_ref, lg_ref, lb_ref, o_ref, us_ref, cv_ref, *, tt):
    j = pl.program_id(1)
    span = tt + CONV_HALO - SUBLANES
    n_rb = tt // CONV_RB
    for cb in range(D_A // LANES):
        lanes = slice(cb * LANES, (cb + 1) * LANES)

        @pl.when(j == 0)
        def _(cb=cb):
            us_ref[0, cb, 0:CONV_HALO, :] = jnp.zeros((CONV_HALO, LANES), F32)

        @pl.when(j > 0)
        def _(cb=cb):
            us_ref[0, cb, 0:CONV_HALO, :] = us_ref[0, cb, tt:tt + CONV_HALO, :]

        us_ref[0, cb, CONV_HALO:CONV_HALO + tt, :] = (
            v_ref[:, lanes].astype(F32) * _sigmoid(g_ref[:, lanes].astype(F32)))
        for r in range(1, SUBLANES):
            us_ref[r, cb, 0:span, :] = us_ref[0, cb, r:r + span, :]

        wb = [jnp.broadcast_to(w_ref[k:k + 1, lanes], (CONV_RB, LANES)) for k in range(CONV_WIDTH)]
        bias = jnp.broadcast_to(cb_ref[0:1, lanes], (CONV_RB, LANES))

        def rb_body(rb, carry, cb=cb, lanes=lanes, wb=wb, bias=bias):
            base = pl.multiple_of(rb * CONV_RB, CONV_RB)
            parts = [bias] + [None] * (CONV_CHAINS - 1)
            for k in range(CONV_WIDTH):
                a, r = divmod(k + CONV_HALO - (CONV_WIDTH - 1), SUBLANES)
                term = wb[k] * us_ref[r, cb, pl.ds(base + SUBLANES * a, CONV_RB), :]
                c = k % CONV_CHAINS
                parts[c] = term if parts[c] is None else parts[c] + term
            cv_ref[pl.ds(base, CONV_RB), lanes] = (parts[0] + parts[1]) + (parts[2] + parts[3])
            return carry

        lax.fori_loop(0, n_rb, rb_body, 0)

    ln_gain, ln_bias = lg_ref[...], lb_ref[...]
    ln_rows = tt // 2 if (tt // 2) % 16 == 0 else tt
    for lo in range(0, tt, ln_rows):
        x = cv_ref[lo:lo + ln_rows, :]
        xc = x - jnp.mean(x, axis=-1, keepdims=True)
        var = jnp.mean(xc * xc, axis=-1, keepdims=True)
        y = xc * lax.rsqrt(var + LN_EPS) * ln_gain + ln_bias
        ga = ga_ref[lo:lo + ln_rows, :].astype(F32)
        o_ref[lo:lo + ln_rows, :] = (_silu(y) * _silu(ga)).astype(o_ref.dtype)


def conv_module(proj, conv_w, conv_b, ln_g, ln_b, *, tt):
    bsz, tp, _ = proj.shape
    row = lambda a: a.reshape(1, D_A).astype(F32)
    return pl.pallas_call(
        functools.partial(_conv_kernel, tt=tt),
        grid=(bsz, tp // tt),
        in_specs=[
            pl.BlockSpec((None, tt, D_A), lambda b, j: (b, j, 0)),
            pl.BlockSpec((None, tt, D_A), lambda b, j: (b, j, 1)),
            pl.BlockSpec((None, tt, D_A), lambda b, j: (b, j, 2)),
            pl.BlockSpec((CONV_WIDTH, D_A), lambda b, j: (0, 0)),
            pl.BlockSpec((1, D_A), lambda b, j: (0, 0)),
            pl.BlockSpec((1, D_A), lambda b, j: (0, 0)),
            pl.BlockSpec((1, D_A), lambda b, j: (0, 0)),
        ],
        out_specs=pl.BlockSpec((None, tt, D_A), lambda b, j: (b, j, 0)),
        out_shape=jax.ShapeDtypeStruct((bsz, tp, D_A), BF16),
        scratch_shapes=[
            pltpu.VMEM((SUBLANES, D_A // LANES, tt + CONV_HALO, LANES), F32),
            pltpu.VMEM((tt, D_A), F32),
        ],
        compiler_params=_cparams(("parallel", "arbitrary"), VMEM_MIXER),
        name="conv_module",
    )(proj, proj, proj, conv_w.astype(F32), row(conv_b), row(ln_g), row(ln_b))


BIAS_ORIGIN = KB + QB
BIAS_ROWS = BIAS_ORIGIN + KB


def _attn_kernel(c_ref, kk_ref, wi_ref, q_ref, qi_ref, gb_ref, kvg_ref, wuk_ref, wuvt_ref, bias_ref, tri_ref,
                 o_ref, cn_s, cnt_s, kk_s, st_s, hi_s, lo_s, qlat_s, lg_a, lg_b, acc_s, m_s, l_s,
                 tau_s, ties_s, *, tp, topk):
    i = pl.program_id(1)
    n_kb = (i + 2) // 2
    n_kb_all = cn_s.shape[0]

    @pl.when(i == 0)
    def _prepare_keys():
        gain = kvg_ref[...]
        for kb in range(n_kb_all):
            lo = kb * KB
            nrow = min(KB, tp - lo)
            c = c_ref[lo:lo + nrow, :].astype(F32)
            cn = c * lax.rsqrt(jnp.mean(c * c, axis=-1, keepdims=True) + RMS_EPS) * gain
            cn_s[kb, 0:nrow, :] = cn.astype(BF16)
            kk = kk_ref[lo:lo + nrow, :]
            lane = lax.broadcasted_iota(I32, kk.shape, 1)
            kk_s[0, kb, 0:nrow, :] = jnp.where(lane < D_IDX, kk, jnp.zeros_like(kk))
            kk_s[1, kb, 0:nrow, :] = jnp.where(lane >= D_IDX, kk, jnp.zeros_like(kk))
            for rb in range(nrow // LANES):
                for cb in range(D_C // LANES):
                    tile = cn[rb * LANES:(rb + 1) * LANES, cb * LANES:(cb + 1) * LANES]
                    cnt_s[kb, cb * LANES:(cb + 1) * LANES, rb * LANES:(rb + 1) * LANES] = tile.T.astype(BF16)
            if nrow < KB:
                cn_s[kb, nrow:KB, :] = jnp.zeros((KB - nrow, D_C), BF16)
                kk_s[:, kb, nrow:KB, :] = jnp.zeros((2, KB - nrow, LANES), BF16)
                cnt_s[kb, :, nrow:KB] = jnp.zeros((D_C, KB - nrow), BF16)

    for h in range(H_B):
        qh = q_ref[:, h * DH_B:(h + 1) * DH_B]
        ql = _dot(qh, wuk_ref[h]) * (DH_B ** -0.5 * LOG2_E)
        qlat_s[h * QB:(h + 1) * QB, :] = ql.astype(BF16)

    wit = wi_ref[...].astype(F32).T

    def score_block(kb):
        k_even = kk_s[0, kb]
        k_odd = kk_s[1, kb]
        acc = jnp.zeros((KB, QB), F32)
        for p in range(H_IDX // 2):
            qp = qi_ref[:, p * LANES:(p + 1) * LANES]
            acc = acc + jnp.maximum(_dot(k_even, qp, _NT), 0.0) * wit[2 * p:2 * p + 1, :]
            acc = acc + jnp.maximum(_dot(k_odd, qp, _NT), 0.0) * wit[2 * p + 1:2 * p + 2, :]
        bits = lax.bitcast_convert_type(acc, I32)
        key = bits ^ ((bits >> 31) & 0x7FFFFFFF)
        s_idx = kb * KB + lax.broadcasted_iota(I32, (KB, QB), 0)
        t_idx = i * QB + lax.broadcasted_iota(I32, (KB, QB), 1)
        adm = (s_idx >= FRONT_PAD) & (jnp.maximum(s_idx >> CHUNK_LOG2, 1) <= jnp.maximum(t_idx >> CHUNK_LOG2, 1))
        key = jnp.where(adm, key, INT_MIN)
        st_s[kb] = key
        hi_s[kb] = key >> 16

    def score_pair(j, carry):
        score_block(2 * j)
        score_block(jnp.minimum(2 * j + 1, n_kb_all - 1))
        return carry

    lax.fori_loop(0, (n_kb + 1) // 2, score_pair, 0)

    def search16(ref, need, n_blocks):
        def bit_body(it, carry):
            t, above = carry
            cand = t + lax.shift_left(jnp.int32(1), jnp.int32(15) - it)
            cand8 = jnp.broadcast_to(cand, (SUBLANES, QB))[None]
            parts = []
            for kb in range(n_blocks):
                below = (ref[kb].reshape(KB // SUBLANES, SUBLANES, QB) - cand8) >> 31
                parts += [below[g] for g in range(KB // SUBLANES)]
            while len(parts) > 1:
                nxt = [parts[g] + parts[g + 1] for g in range(0, len(parts) - 1, 2)]
                parts = nxt + parts[len(parts) - len(parts) % 2:]
            cnt = n_blocks * KB + jnp.sum(parts[0], axis=0, keepdims=True)
            ok = cnt >= need
            return jnp.where(ok, cand, t), jnp.where(ok, above, cnt)

        init = (jnp.full((1, QB), I16_MIN, I32), jnp.zeros((1, QB), I32))
        return lax.fori_loop(0, 16, bit_body, init)

    st_s[n_kb] = jnp.full((KB, QB), INT_MIN, I32)
    hi_s[n_kb] = jnp.full((KB, QB), I16_MIN, I32)
    n_pairs = (n_kb + 1) // 2
    few_keys = N_META + QB * i <= topk

    @pl.when(few_keys)
    def _():
        tau_s[0:1, :] = jnp.full((1, QB), INT_MIN, I32)
        tau_s[1:3, :] = jnp.zeros((2, QB), I32)

    for pairs in range(1, (n_kb_all + 1) // 2 + 1):
        @pl.when(jnp.logical_and(n_pairs == pairs, jnp.logical_not(few_keys)))
        def _(n_blocks=2 * pairs):
            tau_hi, n_above = search16(hi_s, jnp.full((1, QB), topk, I32), n_blocks)
            for kb in range(n_blocks):
                key = st_s[kb]
                low = (key & 0xFFFF) + I16_MIN
                lo_s[kb] = jnp.where((key >> 16) == tau_hi, low, I16_MIN)
            tau_lo, n_above_lo = search16(lo_s, topk - n_above, n_blocks)
            tau_blk = lax.shift_left(tau_hi, jnp.int32(16)) | (tau_lo - I16_MIN)
            wanted = topk - n_above - n_above_lo
            n_tied = jnp.zeros((SUBLANES, QB), I32)
            for kb in range(n_blocks):
                tied = jnp.where(st_s[kb] == tau_blk, 1, 0).astype(I32)
                n_tied = n_tied + jnp.sum(tied.reshape(KB // SUBLANES, SUBLANES, QB), axis=0)
            tau_s[0:1, :] = tau_blk
            tau_s[1:2, :] = wanted
            tau_s[2:3, :] = jnp.sum(n_tied, axis=0, keepdims=True) - wanted

    tau = jnp.maximum(tau_s[0:1, :], INT_MIN + 1)
    ties_wanted = tau_s[1:2, :].astype(F32)
    surplus_ties = jnp.max(tau_s[2:3, :]) > 0

    m_s[...] = jnp.full(m_s.shape, NEG_INF, F32)
    l_s[...] = jnp.zeros(l_s.shape, F32)
    acc_s[...] = jnp.zeros(acc_s.shape, F32)
    ties_s[...] = jnp.zeros(ties_s.shape, F32)

    def logits_stage(kb, lg_ref):
        lg_ref[...] = _dot(cn_s[kb], qlat_s[...], _NT)

    def softmax_stage(kb, lg_ref, break_ties):
        key = st_s[kb]
        if break_ties:
            tie = key == tau
            rank = _dot(tri_ref[...], jnp.where(tie, 1.0, 0.0).astype(BF16)) + ties_s[...]
            ties_s[...] = rank[KB - 1:KB, :]
            tie_neg = jnp.where(rank <= ties_wanted, 0.0, NEG_INF)
            neg = jnp.where(key > tau, 0.0, jnp.where(tie, tie_neg, NEG_INF))
        else:
            neg = jnp.where(key >= tau, 0.0, NEG_INF)
        v0 = pl.multiple_of(jnp.maximum(KB * kb - QB * i + BIAS_ORIGIN, 0), LANES)
        ps, alphas = [], []
        for h in range(H_B):
            lh = lg_ref[:, h * QB:(h + 1) * QB] + bias_ref[h, pl.ds(v0, KB), :] + neg
            m_old = m_s[h:h + 1, :]
            m_new = jnp.maximum(m_old, jnp.max(lh, axis=0, keepdims=True))
            p = jnp.exp2(lh - m_new)
            alpha = jnp.exp2(m_old - m_new)
            l_s[h:h + 1, :] = alpha * l_s[h:h + 1, :] + jnp.sum(p, axis=0, keepdims=True)
            m_s[h:h + 1, :] = m_new
            ps.append(p.astype(BF16))
            alphas.append(alpha)
        pv = _dot(cnt_s[kb], jnp.concatenate(ps, axis=1))
        acc_s[...] = acc_s[...] * jnp.concatenate(alphas, axis=1) + pv

    def attend(break_ties):
        logits_stage(0, lg_a)

        def att_pair(j, carry):
            kb = 2 * j
            logits_stage(kb + 1, lg_b)
            softmax_stage(kb, lg_a, break_ties)
            logits_stage(jnp.minimum(kb + 2, n_kb - 1), lg_a)
            softmax_stage(kb + 1, lg_b, break_ties)
            return carry

        lax.fori_loop(0, n_kb // 2, att_pair, 0)

        @pl.when(n_kb % 2 == 1)
        def _():
            softmax_stage(n_kb - 1, lg_a, break_ties)

    pl.when(surplus_ties)(functools.partial(attend, True))
    pl.when(jnp.logical_not(surplus_ties))(functools.partial(attend, False))

    for h in range(H_B):
        cols = slice(h * QB, (h + 1) * QB)
        ot = (acc_s[:, cols] * (1.0 / l_s[h:h + 1, :])).astype(BF16)
        bt = _dot(wuvt_ref[h], ot)
        gate = gb_ref[:, cols].astype(F32)
        o_ref[:, cols] = (bt.T * _silu(gate)).astype(o_ref.dtype)


def dsa_attention(proj, kv_gain, wuk_t, wuv_t, bias_tab, *, topk):
    bsz, tp, _ = proj.shape
    n_qb = tp // QB
    n_kb_all = -(-tp // KB)
    return pl.pallas_call(
        functools.partial(_attn_kernel, tp=tp, topk=topk),
        grid=(bsz, n_qb),
        in_specs=[
            pl.BlockSpec((None, tp, D_C), lambda b, i: (b, 0, E_COL_C // D_C)),
            pl.BlockSpec((None, tp, LANES), lambda b, i: (b, 0, E_COL_KK // LANES)),
            pl.BlockSpec((None, QB, LANES), lambda b, i: (b, i, E_COL_WI // LANES)),
            pl.BlockSpec((None, QB, H_B * DH_B), lambda b, i: (b, i, E_COL_Q // (H_B * DH_B))),
            pl.BlockSpec((None, QB, H_IDX * D_IDX), lambda b, i: (b, i, E_COL_QI // (H_IDX * D_IDX))),
            pl.BlockSpec((None, QB, H_B * DH_B), lambda b, i: (b, i, E_COL_GB // (H_B * DH_B))),
            pl.BlockSpec((1, D_C), lambda b, i: (0, 0)),
            pl.BlockSpec((H_B, DH_B, D_C), lambda b, i: (0, 0, 0)),
            pl.BlockSpec((H_B, DH_B, D_C), lambda b, i: (0, 0, 0)),
            pl.BlockSpec((H_B, BIAS_ROWS, LANES), lambda b, i: (0, 0, 0)),
            pl.BlockSpec((KB, KB), lambda b, i: (0, 0)),
        ],
        out_specs=pl.BlockSpec((None, QB, H_B * DH_B), lambda b, i: (b, i, 0)),
        out_shape=jax.ShapeDtypeStruct((bsz, tp, H_B * DH_B), BF16),
        scratch_shapes=[
            pltpu.VMEM((n_kb_all, KB, D_C), BF16),
            pltpu.VMEM((n_kb_all, D_C, KB), BF16),
            pltpu.VMEM((2, n_kb_all, KB, LANES), BF16),
            pltpu.VMEM((n_kb_all + 1, KB, QB), I32),
            pltpu.VMEM((n_kb_all + 1, KB, QB), I32),
            pltpu.VMEM((n_kb_all + 1, KB, QB), I32),
            pltpu.VMEM((H_B * QB, D_C), BF16),
            pltpu.VMEM((KB, H_B * QB), F32),
            pltpu.VMEM((KB, H_B * QB), F32),
            pltpu.VMEM((D_C, H_B * QB), F32),
            pltpu.VMEM((H_B, QB), F32),
            pltpu.VMEM((H_B, QB), F32),
            pltpu.VMEM((3, QB), I32),
            pltpu.VMEM((1, QB), F32),
        ],
        compiler_params=_cparams(("parallel", "arbitrary"), VMEM_MIXER),
        name="dsa_attention",
    )(proj, proj, proj, proj, proj, proj, kv_gain.reshape(1, D_C).astype(F32), wuk_t, wuv_t, bias_tab,
      jnp.asarray(np.tril(np.ones((KB, KB), np.float32)), BF16))


def _hgrn_constants():
    n = HC
    t = np.arange(n)[:, None]
    j = np.arange(n)[None, :]
    mats = [(j <= t)]
    for lvl in range(N_LEVELS):
        upper = ((t >> lvl) & 1) == 1
        m_up = (t >> lvl) << lvl
        m_lo = ((t >> lvl) + 1) << lvl
        mats.append(np.where(upper, (j >= m_up) & (j <= t), (j > t) & (j < m_lo)))
    m = np.concatenate(mats, axis=0).astype(np.float32)
    mcat = np.concatenate([m, m], axis=1)
    x = t ^ j
    lev = np.where(j < t, np.floor(np.log2(np.maximum(x, 1))).astype(np.int32), np.where(j == t, -1, -2))
    masks = np.stack([(lev == lvl) for lvl in range(-1, N_LEVELS)]).astype(np.float32)
    return mcat, masks


HGRN_HEADS_PER_STEP = 4


def _role_select(qf, k, lvl, up_small):
    half = 1 << lvl
    if half >= SUBLANES:
        pieces = []
        for blk in range(HC // (2 * half)):
            lo = 2 * half * blk
            pieces += [k[lo:lo + half], qf[lo + half:lo + 2 * half]]
        return jnp.concatenate(pieces, axis=0)
    g, w = HC // SUBLANES, qf.shape[-1]
    sel = jnp.where(up_small[lvl], qf.reshape(g, SUBLANES, w), k.reshape(g, SUBLANES, w))
    return sel.reshape(HC, w)


_SLOT_QE, _SLOT_KT, _SLOT_QB, _SLOT_KB, _SLOT_Y0 = 0, 1, 2, 3, 4
_N_SLOTS = _SLOT_Y0 + N_LEVELS


def _hgrn_kernel(q_ref, fz_ref, i_ref, g_ref, lbp_ref, ng_ref, mcat_ref, mask_ref, o_ref,
                 s_ref, stage_a, stage_b, dec_a, dec_b, *, n_steps, nh):
    wide = nh * DK_C
    row8 = lax.broadcasted_iota(I32, (1, SUBLANES, wide), 1)
    up_small = [((row8 >> lvl) & 1) == 1 for lvl in range(3)]
    s_ref[...] = jnp.zeros(s_ref.shape, F32)

    def gate_stage(c):
        rows = pl.ds(pl.multiple_of(c * HC, HC), HC)
        lb = lbp_ref[0:1, :]
        one_m_lb = lbp_ref[1:2, :]
        qf = _silu(q_ref[rows, :].astype(F32))
        z = fz_ref[rows, :].astype(F32)
        w = one_m_lb * (0.5 + 0.5 * jnp.tanh(0.5 * z))
        k = one_m_lb - w
        lf2 = jnp.log(jnp.maximum(lb + w, F32_TINY)) * LOG2_E
        hi = lf2.astype(BF16)
        lo = (lf2 - hi.astype(F32)).astype(BF16)
        return qf, k, _dot(mcat_ref[...], jnp.concatenate([hi, lo], axis=0))

    def operand_stage(gates, stage, dec_ref):
        qf, k, gsum = gates
        b = gsum[0:HC]
        e_b = jnp.exp2(b)
        e_s = jnp.exp2(b[HC - 1:HC, :] - b)
        stage[_SLOT_QE] = (qf * e_b).astype(BF16)
        stage[_SLOT_KT] = (k * e_s).astype(BF16)
        stage[_SLOT_QB] = qf.astype(BF16)
        stage[_SLOT_KB] = k.astype(BF16)
        for lvl in range(N_LEVELS):
            y = _role_select(qf, k, lvl, up_small) * jnp.exp2(gsum[(1 + lvl) * HC:(2 + lvl) * HC])
            stage[_SLOT_Y0 + lvl] = y.astype(BF16)
        dec_ref[...] = e_b[HC - 1:HC, :]

    def level_stage(c, stage):
        lns = [slice(hh * DK_C, (hh + 1) * DK_C) for hh in range(nh)]
        sts = [s_ref[hh] for hh in range(nh)]
        outs = [_dot(stage[_SLOT_QE, :, ln], st.astype(BF16), _NT) for ln, st in zip(lns, sts)]
        accs = [_dot(stage[_SLOT_QB, :, ln], stage[_SLOT_KB, :, ln], _NT) * mask_ref[0] for ln in lns]
        for lvl in range(N_LEVELS):
            for hh, ln in enumerate(lns):
                y = stage[_SLOT_Y0 + lvl, :, ln]
                accs[hh] = accs[hh] + _dot(y, y, _NT) * mask_ref[lvl + 1]
        return sts, outs, accs

    def output_stage(c, partial, stage, dec_ref):
        sts, outs, accs = partial
        rows = pl.ds(pl.multiple_of(c * HC, HC), HC)
        lns = [slice(hh * DK_C, (hh + 1) * DK_C) for hh in range(nh)]
        vhs = [i_ref[rows, ln] for ln in lns]
        for hh, ln in enumerate(lns):
            vt = vhs[hh].astype(F32).T.astype(BF16)
            s_ref[hh] = sts[hh] * dec_ref[:, ln] + _dot(vt, stage[_SLOT_KT, :, ln])
        normed = []
        for hh in range(nh):
            o = outs[hh] + _dot(accs[hh].astype(BF16), vhs[hh])
            normed.append(o * lax.rsqrt(jnp.mean(o * o, axis=-1, keepdims=True) + RMS_EPS))
        on = jnp.concatenate(normed, axis=1) * ng_ref[...]
        gx = g_ref[rows, :].astype(F32)
        o_ref[rows, :] = (on * _silu(gx)).astype(o_ref.dtype)

    last = n_steps - 1
    operand_stage(gate_stage(0), stage_a, dec_a)

    def pair(j, carry):
        c = 2 * j
        partial = level_stage(c, stage_a)
        operand_stage(gate_stage(c + 1), stage_b, dec_b)
        output_stage(c, partial, stage_a, dec_a)
        partial = level_stage(c + 1, stage_b)
        operand_stage(gate_stage(jnp.minimum(c + 2, last)), stage_a, dec_a)
        output_stage(c + 1, partial, stage_b, dec_b)
        return carry

    lax.fori_loop(0, n_steps // 2, pair, 0)
    if n_steps % 2:
        output_stage(last, level_stage(last, stage_a), stage_a, dec_a)


def hgrn2(proj, lb_params, norm_g):
    bsz, tp, _ = proj.shape
    mcat, masks = _hgrn_constants()
    nh = HGRN_HEADS_PER_STEP
    wide = nh * DK_C
    n_groups = H_C // nh
    col = lambda off: pl.BlockSpec((None, tp, wide), lambda b, h, off=off: (b, 0, off + h))
    return pl.pallas_call(
        functools.partial(_hgrn_kernel, n_steps=tp // HC, nh=nh),
        grid=(bsz, n_groups),
        in_specs=[
            col(0), col(n_groups), col(2 * n_groups), col(3 * n_groups),
            pl.BlockSpec((2, wide), lambda b, h: (0, h)),
            pl.BlockSpec((1, wide), lambda b, h: (0, h)),
            pl.BlockSpec(mcat.shape, lambda b, h: (0, 0)),
            pl.BlockSpec(masks.shape, lambda b, h: (0, 0, 0)),
        ],
        out_specs=pl.BlockSpec((None, tp, wide), lambda b, h: (b, 0, h)),
        out_shape=jax.ShapeDtypeStruct((bsz, tp, H_C * DK_C), BF16),
        scratch_shapes=[
            pltpu.VMEM((nh, DK_C, DK_C), F32),
            pltpu.VMEM((_N_SLOTS, HC, wide), BF16),
            pltpu.VMEM((_N_SLOTS, HC, wide), BF16),
            pltpu.VMEM((1, wide), F32),
            pltpu.VMEM((1, wide), F32),
        ],
        compiler_params=_cparams(("parallel", "parallel"), VMEM_MIXER),
        name="hgrn2",
    )(proj, proj, proj, proj, lb_params, norm_g.reshape(1, -1).astype(F32),
      jnp.asarray(mcat, BF16), jnp.asarray(masks, F32))


def _t5_bucket(rel):
    nb = N_BUCKETS // 2
    ret = jnp.where(rel > 0, nb, 0)
    n = jnp.abs(rel)
    max_exact = nb // 2
    nf = jnp.maximum(n, 1).astype(F32)
    large = max_exact + (jnp.log(nf / max_exact) / math.log(MAX_DISTANCE / max_exact)
                         * (nb - max_exact)).astype(I32)
    large = jnp.minimum(large, nb - 1)
    return ret + jnp.where(n < max_exact, n, large)


def _bias_table(rel_bias_table):
    v = jnp.arange(BIAS_ROWS, dtype=I32)[:, None]
    q = jnp.arange(LANES, dtype=I32)[None, :]
    onehot = jax.nn.one_hot(_t5_bucket(v - BIAS_ORIGIN - q), N_BUCKETS, dtype=F32)
    tab = jnp.einsum("vqb,bh->hvq", onehot, rel_bias_table.astype(F32), precision=lax.Precision.HIGHEST)
    return tab * LOG2_E


def _even_in_weight(w):
    glu_v, glu_g, gate_a, q, c, gate_b, qi, ki, wi = jnp.split(
        w, np.cumsum([D_A, D_A, D_A, H_B * DH_B, D_C, H_B * DH_B, H_IDX * D_IDX, D_IDX])[:8].tolist(), axis=1)
    pad = jnp.zeros((w.shape[0], E_NPAD - E_COL_WI - H_IDX), w.dtype)
    cols = [glu_v, glu_g, gate_a, q, gate_b, qi * (D_IDX ** -0.5), c, ki, ki, wi * (H_IDX ** -0.5), pad]
    return jnp.concatenate([col.astype(BF16) for col in cols], axis=1)


def kernel(x, meta_tokens, norm_gain, final_norm_gain, rel_bias_table, w_in_even, conv_w, conv_b,
           conv_ln_gain, conv_ln_bias, kv_norm_gain, w_uk, w_uv, w_out_even, w_in_odd, lb_logits,
           rec_norm_gain, w_out_odd):
    bsz, seq, d = x.shape
    assert d == D_MODEL and seq % LANES == 0
    depth = norm_gain.shape[0]
    tp = FRONT_PAD + N_META + seq
    topk = min(TOPK_MAX, seq // 4)
    tt_mm = tp // 2 if (tp // 2) % 16 == 0 else tp
    tt_out = tp // 4 if (tp // 4) % 16 == 0 else tp
    tt_last = seq // 4
    tt_conv = tp // 4 if (tp // 4) % CONV_RB == 0 else tp // 2 if (tp // 2) % CONV_RB == 0 else tp
    h, hn = embed(x, meta_tokens, norm_gain[0], tt=tt_out)

    lb_soft = jax.nn.softmax(lb_logits.astype(F32), axis=0)
    lower_bounds = jnp.cumsum(lb_soft, axis=0) - lb_soft[0]
    bias_tab = _bias_table(rel_bias_table)

    for layer in range(depth):
        last = layer == depth - 1
        if layer % 2 == 0:
            e = layer // 2
            w_in, tn = _even_in_weight(w_in_even[e]), E_NPAD // 2
        else:
            o = layer // 2
            w_in, tn = w_in_odd[o].astype(BF16), 2048
        proj = matmul(hn, w_in, tt=tt_mm, tn=tn)
        if layer % 2 == 0:
            a_out = conv_module(proj, conv_w[e], conv_b[e], conv_ln_gain[e], conv_ln_bias[e], tt=tt_conv)
            wuk_t = jnp.transpose(w_uk[e], (0, 2, 1)).astype(BF16)
            wuv_t = jnp.transpose(w_uv[e], (0, 2, 1)).astype(BF16)
            b_out = dsa_attention(proj, kv_norm_gain[e], wuk_t, wuv_t, bias_tab, topk=topk)
            mixed, w_out = [a_out, b_out], w_out_even[e].astype(BF16)
        else:
            lb = lower_bounds[layer]
            mixed = [hgrn2(proj, jnp.stack([lb, 1.0 - lb], axis=0), rec_norm_gain[o])]
            w_out = w_out_odd[o].astype(BF16)
        if last:
            return out_proj(mixed, w_out, h, final_norm_gain, tt=tt_last, final=True)
        h, hn = out_proj(mixed, w_out, h, norm_gain[layer + 1], tt=tt_out, final=False)
```

```python
import functools
import math

import numpy as np
import jax
import jax.numpy as jnp
from jax import lax
from jax.experimental import pallas as pl
from jax.experimental.pallas import tpu as pltpu

F32 = jnp.float32
BF16 = jnp.bfloat16
I32 = jnp.int32

D_MODEL = 2048
CHUNK_LOG2 = 6
N_META = 16
D_A = 1024
CONV_WIDTH = 31
H_B = 8
DH_B = 128
D_C = 256
H_IDX = 16
D_IDX = 64
TOPK_MAX = 256
N_BUCKETS = 32
MAX_DISTANCE = 128
H_C = 16
DK_C = 128
RMS_EPS = 1e-6
LN_EPS = 1e-5
NEG_INF = -1e30
INT_MIN = -(2 ** 31)
LOG2_E = 1.4426950408889634
F32_TINY = 1.1754943508222875e-38

I16_MIN = -(2 ** 15)

LANES = 128
SUBLANES = 8
MXU_N = 256

FRONT_PAD = LANES - N_META
QB = LANES
KB = MXU_N
HC = 128
N_LEVELS = 7

E_COL_Q = 3 * D_A
E_COL_GB = E_COL_Q + H_B * DH_B
E_COL_QI = E_COL_GB + H_B * DH_B
E_COL_C = E_COL_QI + H_IDX * D_IDX
E_COL_KK = E_COL_C + D_C
E_COL_WI = E_COL_KK + 2 * D_IDX
E_NPAD = E_COL_WI + LANES
assert 2 * D_IDX == LANES and E_NPAD % MXU_N == 0

MIB = 1024 * 1024
VMEM_PROJ = 56
VMEM_MIXER = 40


def _cparams(sem, vmem_mib):
    return pltpu.CompilerParams(dimension_semantics=sem, vmem_limit_bytes=vmem_mib * MIB)


def _sigmoid(x):
    return 1.0 / (1.0 + jnp.exp(-x))


def _silu(x):
    h = 0.5 * x
    return h + h * jnp.tanh(h)


_NN = (((1,), (0,)), ((), ()))
_NT = (((1,), (1,)), ((), ()))


def _dot(a, b, dims=_NN):
    return lax.dot_general(a, b, dims, preferred_element_type=F32)


def _rms_normed(x, gain):
    ms = jnp.mean(x * x, axis=-1, keepdims=True)
    return x * lax.rsqrt(ms + RMS_EPS) * gain


def _embed_kernel(x_ref, meta_ref, g_ref, h_ref, hn_ref, *, tt):
    first = FRONT_PAD + N_META

    @pl.when(pl.program_id(1) == 0)
    def _():
        d = h_ref.shape[-1]
        meta = meta_ref[...]
        x = x_ref[0, 0:tt - first, :]
        h_ref[0:FRONT_PAD, :] = jnp.zeros((FRONT_PAD, d), F32)
        h_ref[FRONT_PAD:first, :] = meta
        h_ref[first:, :] = x
        hn_ref[0:FRONT_PAD, :] = jnp.zeros((FRONT_PAD, d), BF16)
        hn_ref[FRONT_PAD:first, :] = _rms_normed(meta, g_ref[...]).astype(BF16)
        hn_ref[first:, :] = _rms_normed(x, g_ref[...]).astype(BF16)

    @pl.when(pl.program_id(1) > 0)
    def _():
        x = x_ref[0]
        h_ref[...] = x
        hn_ref[...] = _rms_normed(x, g_ref[...]).astype(BF16)


def embed(x, meta_tokens, gain, *, tt):
    bsz, seq, d = x.shape
    first = FRONT_PAD + N_META
    tp = first + seq
    assert tt > first and tp % tt == 0
    x_spec = pl.BlockSpec((pl.Element(1), pl.Element(tt), pl.Element(d)),
                          lambda b, j: (b, pl.multiple_of(jnp.maximum(j * tt - first, 0), SUBLANES), 0))
    spec = pl.BlockSpec((None, tt, d), lambda b, j: (b, j, 0))
    return pl.pallas_call(
        functools.partial(_embed_kernel, tt=tt),
        grid=(bsz, tp // tt),
        in_specs=[
            x_spec,
            pl.BlockSpec((N_META, d), lambda b, j: (0, 0)),
            pl.BlockSpec((1, d), lambda b, j: (0, 0)),
        ],
        out_specs=[spec, spec],
        out_shape=[jax.ShapeDtypeStruct((bsz, tp, d), F32), jax.ShapeDtypeStruct((bsz, tp, d), BF16)],
        compiler_params=_cparams(("parallel", "parallel"), VMEM_MIXER),
        name="embed",
    )(x.astype(F32), meta_tokens.astype(F32), gain.reshape(1, d).astype(F32))


def _matmul_kernel(x_ref, w_ref, o_ref):
    o_ref[...] = _dot(x_ref[...], w_ref[...]).astype(o_ref.dtype)


def matmul(hn, w, *, tt, tn):
    bsz, tp, d = hn.shape
    n = w.shape[1]
    return pl.pallas_call(
        _matmul_kernel,
        grid=(bsz, tp // tt, n // tn),
        in_specs=[
            pl.BlockSpec((None, tt, d), lambda b, i, j: (b, i, 0)),
            pl.BlockSpec((d, tn), lambda b, i, j: (0, j)),
        ],
        out_specs=pl.BlockSpec((None, tt, tn), lambda b, i, j: (b, i, j)),
        out_shape=jax.ShapeDtypeStruct((bsz, tp, n), BF16),
        compiler_params=_cparams(("parallel", "parallel", "parallel"), VMEM_PROJ),
        name="in_proj",
    )(hn, w)


def _out_proj_kernel(*refs, n_in, tt, final):
    xs = refs[:n_in]
    w_ref, h_ref, g_ref = refs[n_in], refs[n_in + 1], refs[n_in + 2]
    tile = lambda ref: ref[0] if len(ref.shape) == 3 else ref[...]
    y = None
    off = 0
    for x_ref in xs:
        kdim = x_ref.shape[-1]
        part = _dot(tile(x_ref), w_ref[off:off + kdim, :])
        y = part if y is None else y + part
        off += kdim
    h_new = tile(h_ref) + y
    if final:
        refs[-1][...] = _rms_normed(h_new, g_ref[...])
    else:
        row = pl.program_id(1) * tt + lax.broadcasted_iota(I32, (tt, 1), 0)
        h_new = jnp.where(row >= FRONT_PAD, h_new, 0.0)
        refs[-2][...] = h_new
        refs[-1][...] = _rms_normed(h_new, g_ref[...]).astype(BF16)


def out_proj(xs, w, h, gain, *, tt, final):
    bsz, tp, d = h.shape
    if final:
        first = FRONT_PAD + N_META
        rows_out = tp - first
        row_spec = lambda k: pl.BlockSpec((pl.Element(1), pl.Element(tt), pl.Element(k)),
                                          lambda b, i: (b, pl.multiple_of(first + i * tt, LANES), 0))
    else:
        rows_out = tp
        row_spec = lambda k: pl.BlockSpec((None, tt, k), lambda b, i: (b, i, 0))
    in_specs = [row_spec(x.shape[-1]) for x in xs]
    in_specs += [pl.BlockSpec(w.shape, lambda b, i: (0, 0)), row_spec(d), pl.BlockSpec((1, d), lambda b, i: (0, 0))]
    out_spec = pl.BlockSpec((None, tt, d), lambda b, i: (b, i, 0))
    out_f32 = jax.ShapeDtypeStruct((bsz, rows_out, d), F32)
    return pl.pallas_call(
        functools.partial(_out_proj_kernel, n_in=len(xs), tt=tt, final=final),
        grid=(bsz, rows_out // tt),
        in_specs=in_specs,
        out_specs=out_spec if final else [out_spec, out_spec],
        out_shape=out_f32 if final else [out_f32, jax.ShapeDtypeStruct((bsz, rows_out, d), BF16)],
        compiler_params=_cparams(("parallel", "parallel"), VMEM_PROJ),
        name="out_proj_final" if final else "out_proj",
    )(*xs, w, h, gain.reshape(1, d).astype(F32))


CONV_HALO = 32
CONV_RB = 16
CONV_CHAINS = 4


def _conv_kernel(v_ref, g_ref, ga_ref, w_ref, cb_ref, lg_ref, lb_ref, o_ref, us_ref, cv_ref, *, tt):
    j = pl.program_id(1)
    span = tt + CONV_HALO - SUBLANES
    n_rb = tt // CONV_RB
    for cb in range(D_A // LANES):
        lanes = slice(cb * LANES, (cb + 1) * LANES)

        @pl.when(j == 0)
        def _(cb=cb):
            us_ref[0, cb, 0:CONV_HALO, :] = jnp.zeros((CONV_HALO, LANES), F32)

        @pl.when(j > 0)
        def _(cb=cb):
            us_ref[0, cb, 0:CONV_HALO, :] = us_ref[0, cb, tt:tt + CONV_HALO, :]

        us_ref[0, cb, CONV_HALO:CONV_HALO + tt, :] = (
            v_ref[:, lanes].astype(F32) * _sigmoid(g_ref[:, lanes].astype(F32)))
        for r in range(1, SUBLANES):
            us_ref[r, cb, 0:span, :] = us_ref[0, cb, r:r + span, :]

        wb = [jnp.broadcast_to(w_ref[k:k + 1, lanes], (CONV_RB, LANES)) for k in range(CONV_WIDTH)]
        bias = jnp.broadcast_to(cb_ref[0:1, lanes], (CONV_RB, LANES))

        def rb_body(rb, carry, cb=cb, lanes=lanes, wb=wb, bias=bias):
            base = pl.multiple_of(rb * CONV_RB, CONV_RB)
            parts = [bias] + [None] * (CONV_CHAINS - 1)
            for k in range(CONV_WIDTH):
                a, r = divmod(k + CONV_HALO - (CONV_WIDTH - 1), SUBLANES)
                term = wb[k] * us_ref[r, cb, pl.ds(base + SUBLANES * a, CONV_RB), :]
                c = k % CONV_CHAINS
                parts[c] = term if parts[c] is None else parts[c] + term
            cv_ref[pl.ds(base, CONV_RB), lanes] = (parts[0] + parts[1]) + (parts[2] + parts[3])
            return carry

        lax.fori_loop(0, n_rb, rb_body, 0)

    ln_gain, ln_bias = lg_ref[...], lb_ref[...]
    ln_rows = tt // 2 if (tt // 2) % 16 == 0 else tt
    for lo in range(0, tt, ln_rows):
        x = cv_ref[lo:lo + ln_rows, :]
        xc = x - jnp.mean(x, axis=-1, keepdims=True)
        var = jnp.mean(xc * xc, axis=-1, keepdims=True)
        y = xc * lax.rsqrt(var + LN_EPS) * ln_gain + ln_bias
        ga = ga_ref[lo:lo + ln_rows, :].astype(F32)
        o_ref[lo:lo + ln_rows, :] = (_silu(y) * _silu(ga)).astype(o_ref.dtype)


def conv_module(proj, conv_w, conv_b, ln_g, ln_b, *, tt):
    bsz, tp, _ = proj.shape
    row = lambda a: a.reshape(1, D_A).astype(F32)
    return pl.pallas_call(
        functools.partial(_conv_kernel, tt=tt),
        grid=(bsz, tp // tt),
        in_specs=[
            pl.BlockSpec((None, tt, D_A), lambda b, j: (b, j, 0)),
            pl.BlockSpec((None, tt, D_A), lambda b, j: (b, j, 1)),
            pl.BlockSpec((None, tt, D_A), lambda b, j: (b, j, 2)),
            pl.BlockSpec((CONV_WIDTH, D_A), lambda b, j: (0, 0)),
            pl.BlockSpec((1, D_A), lambda b, j: (0, 0)),
            pl.BlockSpec((1, D_A), lambda b, j: (0, 0)),
            pl.BlockSpec((1, D_A), lambda b, j: (0, 0)),
        ],
        out_specs=pl.BlockSpec((None, tt, D_A), lambda b, j: (b, j, 0)),
        out_shape=jax.ShapeDtypeStruct((bsz, tp, D_A), BF16),
        scratch_shapes=[
            pltpu.VMEM((SUBLANES, D_A // LANES, tt + CONV_HALO, LANES), F32),
            pltpu.VMEM((tt, D_A), F32),
        ],
        compiler_params=_cparams(("parallel", "arbitrary"), VMEM_MIXER),
        name="conv_module",
    )(proj, proj, proj, conv_w.astype(F32), row(conv_b), row(ln_g), row(ln_b))


BIAS_ORIGIN = KB + QB
BIAS_ROWS = BIAS_ORIGIN + KB


def _attn_kernel(c_ref, kk_ref, wi_ref, q_ref, qi_ref, gb_ref, kvg_ref, wuk_ref, wuvt_ref, bias_ref, tri_ref,
                 o_ref, cn_s, cnt_s, kk_s, st_s, hi_s, lo_s, qlat_s, lg_a, lg_b, acc_s, m_s, l_s,
                 tau_s, ties_s, *, tp, topk):
    i = pl.program_id(1)
    n_kb = (i + 2) // 2
    n_kb_all = cn_s.shape[0]

    @pl.when(i == 0)
    def _prepare_keys():
        gain = kvg_ref[...]
        for kb in range(n_kb_all):
            lo = kb * KB
            nrow = min(KB, tp - lo)
            c = c_ref[lo:lo + nrow, :].astype(F32)
            cn = c * lax.rsqrt(jnp.mean(c * c, axis=-1, keepdims=True) + RMS_EPS) * gain
            cn_s[kb, 0:nrow, :] = cn.astype(BF16)
            kk = kk_ref[lo:lo + nrow, :]
            lane = lax.broadcasted_iota(I32, kk.shape, 1)
            kk_s[0, kb, 0:nrow, :] = jnp.where(lane < D_IDX, kk, jnp.zeros_like(kk))
            kk_s[1, kb, 0:nrow, :] = jnp.where(lane >= D_IDX, kk, jnp.zeros_like(kk))
            for rb in range(nrow // LANES):
                for cb in range(D_C // LANES):
                    tile = cn[rb * LANES:(rb + 1) * LANES, cb * LANES:(cb + 1) * LANES]
                    cnt_s[kb, cb * LANES:(cb + 1) * LANES, rb * LANES:(rb + 1) * LANES] = tile.T.astype(BF16)
            if nrow < KB:
                cn_s[kb, nrow:KB, :] = jnp.zeros((KB - nrow, D_C), BF16)
                kk_s[:, kb, nrow:KB, :] = jnp.zeros((2, KB - nrow, LANES), BF16)
                cnt_s[kb, :, nrow:KB] = jnp.zeros((D_C, KB - nrow), BF16)

    for h in range(H_B):
        qh = q_ref[:, h * DH_B:(h + 1) * DH_B]
        ql = _dot(qh, wuk_ref[h]) * (DH_B ** -0.5 * LOG2_E)
        qlat_s[h * QB:(h + 1) * QB, :] = ql.astype(BF16)

    wit = wi_ref[...].astype(F32).T

    def score_block(kb):
        k_even = kk_s[0, kb]
        k_odd = kk_s[1, kb]
        acc = jnp.zeros((KB, QB), F32)
        for p in range(H_IDX // 2):
            qp = qi_ref[:, p * LANES:(p + 1) * LANES]
            acc = acc + jnp.maximum(_dot(k_even, qp, _NT), 0.0) * wit[2 * p:2 * p + 1, :]
            acc = acc + jnp.maximum(_dot(k_odd, qp, _NT), 0.0) * wit[2 * p + 1:2 * p + 2, :]
        bits = lax.bitcast_convert_type(acc, I32)
        key = bits ^ ((bits >> 31) & 0x7FFFFFFF)
        s_idx = kb * KB + lax.broadcasted_iota(I32, (KB, QB), 0)
        t_idx = i * QB + lax.broadcasted_iota(I32, (KB, QB), 1)
        adm = (s_idx >= FRONT_PAD) & (jnp.maximum(s_idx >> CHUNK_LOG2, 1) <= jnp.maximum(t_idx >> CHUNK_LOG2, 1))
        key = jnp.where(adm, key, INT_MIN)
        st_s[kb] = key
        hi_s[kb] = key >> 16

    def score_pair(j, carry):
        score_block(2 * j)
        score_block(jnp.minimum(2 * j + 1, n_kb_all - 1))
        return carry

    lax.fori_loop(0, (n_kb + 1) // 2, score_pair, 0)

    def search16(ref, need, n_blocks):
        def bit_body(it, carry):
            t, above, at = carry
            cand = t + lax.shift_left(jnp.int32(1), jnp.int32(15) - it)
            cand8 = jnp.broadcast_to(cand, (SUBLANES, QB))[None]
            parts = []
            for kb in range(n_blocks):
                below = (ref[kb].reshape(KB // SUBLANES, SUBLANES, QB) - cand8) >> 31
                parts += [below[g] for g in range(KB // SUBLANES)]
            while len(parts) > 1:
                nxt = [parts[g] + parts[g + 1] for g in range(0, len(parts) - 1, 2)]
                parts = nxt + parts[len(parts) - len(parts) % 2:]
            cnt = n_blocks * KB + jnp.sum(parts[0], axis=0, keepdims=True)
            ok = cnt >= need
            return jnp.where(ok, cand, t), jnp.where(ok, above, cnt), jnp.where(ok, cnt, at)

        init = (jnp.full((1, QB), I16_MIN, I32), jnp.zeros((1, QB), I32), jnp.full((1, QB), -1, I32))
        return lax.fori_loop(0, 16, bit_body, init)

    st_s[n_kb] = jnp.full((KB, QB), INT_MIN, I32)
    hi_s[n_kb] = jnp.full((KB, QB), I16_MIN, I32)
    n_pairs = (n_kb + 1) // 2
    few_keys = N_META + QB * i <= topk

    @pl.when(few_keys)
    def _():
        tau_s[0:1, :] = jnp.full((1, QB), INT_MIN, I32)
        tau_s[1:3, :] = jnp.zeros((2, QB), I32)

    for pairs in range(1, (n_kb_all + 1) // 2 + 1):
        @pl.when(jnp.logical_and(n_pairs == pairs, jnp.logical_not(few_keys)))
        def _(n_blocks=2 * pairs):
            tau_hi, n_above, at_hi = search16(hi_s, jnp.full((1, QB), topk, I32), n_blocks)
            for kb in range(n_blocks):
                key = st_s[kb]
                low = (key & 0xFFFF) + I16_MIN
                lo_s[kb] = jnp.where((key >> 16) == tau_hi, low, I16_MIN)
            tau_lo, n_above_lo, at_lo = search16(lo_s, topk - n_above, n_blocks)
            at_or_above = n_above + jnp.where(at_lo >= 0, at_lo, at_hi - n_above)
            tau_s[0:1, :] = lax.shift_left(tau_hi, jnp.int32(16)) | (tau_lo - I16_MIN)
            tau_s[1:2, :] = topk - n_above - n_above_lo
            tau_s[2:3, :] = jnp.where(at_hi >= 0, at_or_above - topk, 0)

    tau = jnp.maximum(tau_s[0:1, :], INT_MIN + 1)
    ties_wanted = tau_s[1:2, :].astype(F32)
    surplus_ties = jnp.max(tau_s[2:3, :]) > 0

    m_s[...] = jnp.full(m_s.shape, NEG_INF, F32)
    l_s[...] = jnp.zeros(l_s.shape, F32)
    acc_s[...] = jnp.zeros(acc_s.shape, F32)
    ties_s[...] = jnp.zeros(ties_s.shape, F32)

    def logits_stage(kb, lg_ref):
        lg_ref[...] = _dot(cn_s[kb], qlat_s[...], _NT)

    def softmax_stage(kb, lg_ref, break_ties):
        key = st_s[kb]
        if break_ties:
            tie = key == tau
            rank = _dot(tri_ref[...], jnp.where(tie, 1.0, 0.0).astype(BF16)) + ties_s[...]
            ties_s[...] = rank[KB - 1:KB, :]
            tie_neg = jnp.where(rank <= ties_wanted, 0.0, NEG_INF)
            neg = jnp.where(key > tau, 0.0, jnp.where(tie, tie_neg, NEG_INF))
        else:
            neg = jnp.where(key >= tau, 0.0, NEG_INF)
        v0 = pl.multiple_of(jnp.maximum(KB * kb - QB * i + BIAS_ORIGIN, 0), LANES)
        ps, alphas = [], []
        for h in range(H_B):
            lh = lg_ref[:, h * QB:(h + 1) * QB] + bias_ref[h, pl.ds(v0, KB), :] + neg
            m_old = m_s[h:h + 1, :]
            m_new = jnp.maximum(m_old, jnp.max(lh, axis=0, keepdims=True))
            p = jnp.exp2(lh - m_new)
            alpha = jnp.exp2(m_old - m_new)
            l_s[h:h + 1, :] = alpha * l_s[h:h + 1, :] + jnp.sum(p, axis=0, keepdims=True)
            m_s[h:h + 1, :] = m_new
            ps.append(p.astype(BF16))
            alphas.append(alpha)
        pv = _dot(cnt_s[kb], jnp.concatenate(ps, axis=1))
        acc_s[...] = acc_s[...] * jnp.concatenate(alphas, axis=1) + pv

    def attend(break_ties):
        logits_stage(0, lg_a)

        def att_pair(j, carry):
            kb = 2 * j
            logits_stage(kb + 1, lg_b)
            softmax_stage(kb, lg_a, break_ties)
            logits_stage(jnp.minimum(kb + 2, n_kb - 1), lg_a)
            softmax_stage(kb + 1, lg_b, break_ties)
            return carry

        lax.fori_loop(0, n_kb // 2, att_pair, 0)

        @pl.when(n_kb % 2 == 1)
        def _():
            softmax_stage(n_kb - 1, lg_a, break_ties)

    pl.when(surplus_ties)(functools.partial(attend, True))
    pl.when(jnp.logical_not(surplus_ties))(functools.partial(attend, False))

    for h in range(H_B):
        cols = slice(h * QB, (h + 1) * QB)
        ot = (acc_s[:, cols] * (1.0 / l_s[h:h + 1, :])).astype(BF16)
        bt = _dot(wuvt_ref[h], ot)
        gate = gb_ref[:, cols].astype(F32)
        o_ref[:, cols] = (bt.T * _silu(gate)).astype(o_ref.dtype)


def dsa_attention(proj, kv_gain, wuk_t, wuv_t, bias_tab, *, topk):
    bsz, tp, _ = proj.shape
    n_qb = tp // QB
    n_kb_all = -(-tp // KB)
    return pl.pallas_call(
        functools.partial(_attn_kernel, tp=tp, topk=topk),
        grid=(bsz, n_qb),
        in_specs=[
            pl.BlockSpec((None, tp, D_C), lambda b, i: (b, 0, E_COL_C // D_C)),
            pl.BlockSpec((None, tp, LANES), lambda b, i: (b, 0, E_COL_KK // LANES)),
            pl.BlockSpec((None, QB, LANES), lambda b, i: (b, i, E_COL_WI // LANES)),
            pl.BlockSpec((None, QB, H_B * DH_B), lambda b, i: (b, i, E_COL_Q // (H_B * DH_B))),
            pl.BlockSpec((None, QB, H_IDX * D_IDX), lambda b, i: (b, i, E_COL_QI // (H_IDX * D_IDX))),
            pl.BlockSpec((None, QB, H_B * DH_B), lambda b, i: (b, i, E_COL_GB // (H_B * DH_B))),
            pl.BlockSpec((1, D_C), lambda b, i: (0, 0)),
            pl.BlockSpec((H_B, DH_B, D_C), lambda b, i: (0, 0, 0)),
            pl.BlockSpec((H_B, DH_B, D_C), lambda b, i: (0, 0, 0)),
            pl.BlockSpec((H_B, BIAS_ROWS, LANES), lambda b, i: (0, 0, 0)),
            pl.BlockSpec((KB, KB), lambda b, i: (0, 0)),
        ],
        out_specs=pl.BlockSpec((None, QB, H_B * DH_B), lambda b, i: (b, i, 0)),
        out_shape=jax.ShapeDtypeStruct((bsz, tp, H_B * DH_B), BF16),
        scratch_shapes=[
            pltpu.VMEM((n_kb_all, KB, D_C), BF16),
            pltpu.VMEM((n_kb_all, D_C, KB), BF16),
            pltpu.VMEM((2, n_kb_all, KB, LANES), BF16),
            pltpu.VMEM((n_kb_all + 1, KB, QB), I32),
            pltpu.VMEM((n_kb_all + 1, KB, QB), I32),
            pltpu.VMEM((n_kb_all + 1, KB, QB), I32),
            pltpu.VMEM((H_B * QB, D_C), BF16),
            pltpu.VMEM((KB, H_B * QB), F32),
            pltpu.VMEM((KB, H_B * QB), F32),
            pltpu.VMEM((D_C, H_B * QB), F32),
            pltpu.VMEM((H_B, QB), F32),
            pltpu.VMEM((H_B, QB), F32),
            pltpu.VMEM((3, QB), I32),
            pltpu.VMEM((1, QB), F32),
        ],
        compiler_params=_cparams(("parallel", "arbitrary"), VMEM_MIXER),
        name="dsa_attention",
    )(proj, proj, proj, proj, proj, proj, kv_gain.reshape(1, D_C).astype(F32), wuk_t, wuv_t, bias_tab,
      jnp.asarray(np.tril(np.ones((KB, KB), np.float32)), BF16))


def _hgrn_constants():
    n = HC
    t = np.arange(n)[:, None]
    j = np.arange(n)[None, :]
    mats = [(j <= t)]
    for lvl in range(N_LEVELS):
        upper = ((t >> lvl) & 1) == 1
        m_up = (t >> lvl) << lvl
        m_lo = ((t >> lvl) + 1) << lvl
        mats.append(np.where(upper, (j >= m_up) & (j <= t), (j > t) & (j < m_lo)))
    m = np.concatenate(mats, axis=0).astype(np.float32)
    mcat = np.concatenate([m, m], axis=1)
    x = t ^ j
    lev = np.where(j < t, np.floor(np.log2(np.maximum(x, 1))).astype(np.int32), np.where(j == t, -1, -2))
    masks = np.stack([(lev == lvl) for lvl in range(-1, N_LEVELS)]).astype(np.float32)
    return mcat, masks


HGRN_HEADS_PER_STEP = 4


def _role_select(qf, k, lvl, up_small):
    half = 1 << lvl
    if half >= SUBLANES:
        pieces = []
        for blk in range(HC // (2 * half)):
            lo = 2 * half * blk
            pieces += [k[lo:lo + half], qf[lo + half:lo + 2 * half]]
        return jnp.concatenate(pieces, axis=0)
    g, w = HC // SUBLANES, qf.shape[-1]
    sel = jnp.where(up_small[lvl], qf.reshape(g, SUBLANES, w), k.reshape(g, SUBLANES, w))
    return sel.reshape(HC, w)


_SLOT_QE, _SLOT_KT, _SLOT_QB, _SLOT_KB, _SLOT_Y0 = 0, 1, 2, 3, 4
_N_SLOTS = _SLOT_Y0 + N_LEVELS


def _hgrn_kernel(q_ref, fz_ref, i_ref, g_ref, lbp_ref, ng_ref, mcat_ref, mask_ref, o_ref,
                 s_ref, stage_a, stage_b, dec_a, dec_b, *, n_steps, nh):
    wide = nh * DK_C
    row8 = lax.broadcasted_iota(I32, (1, SUBLANES, wide), 1)
    up_small = [((row8 >> lvl) & 1) == 1 for lvl in range(3)]
    s_ref[...] = jnp.zeros(s_ref.shape, F32)

    def gate_stage(c):
        rows = pl.ds(pl.multiple_of(c * HC, HC), HC)
        lb = lbp_ref[0:1, :]
        one_m_lb = lbp_ref[1:2, :]
        qf = _silu(q_ref[rows, :].astype(F32))
        z = fz_ref[rows, :].astype(F32)
        w = one_m_lb * (0.5 + 0.5 * jnp.tanh(0.5 * z))
        k = one_m_lb - w
        lf2 = jnp.log(jnp.maximum(lb + w, F32_TINY)) * LOG2_E
        hi = lf2.astype(BF16)
        lo = (lf2 - hi.astype(F32)).astype(BF16)
        return qf, k, _dot(mcat_ref[...], jnp.concatenate([hi, lo], axis=0))

    def operand_stage(gates, stage, dec_ref):
        qf, k, gsum = gates
        b = gsum[0:HC]
        e_b = jnp.exp2(b)
        e_s = jnp.exp2(b[HC - 1:HC, :] - b)
        stage[_SLOT_QE] = (qf * e_b).astype(BF16)
        stage[_SLOT_KT] = (k * e_s).astype(BF16)
        stage[_SLOT_QB] = qf.astype(BF16)
        stage[_SLOT_KB] = k.astype(BF16)
        for lvl in range(N_LEVELS):
            y = _role_select(qf, k, lvl, up_small) * jnp.exp2(gsum[(1 + lvl) * HC:(2 + lvl) * HC])
            stage[_SLOT_Y0 + lvl] = y.astype(BF16)
        dec_ref[...] = e_b[HC - 1:HC, :]

    def level_stage(c, stage):
        lns = [slice(hh * DK_C, (hh + 1) * DK_C) for hh in range(nh)]
        sts = [s_ref[hh] for hh in range(nh)]
        outs = [_dot(stage[_SLOT_QE, :, ln], st.astype(BF16), _NT) for ln, st in zip(lns, sts)]
        accs = [_dot(stage[_SLOT_QB, :, ln], stage[_SLOT_KB, :, ln], _NT) * mask_ref[0] for ln in lns]
        for lvl in range(N_LEVELS):
            for hh, ln in enumerate(lns):
                y = stage[_SLOT_Y0 + lvl, :, ln]
                accs[hh] = accs[hh] + _dot(y, y, _NT) * mask_ref[lvl + 1]
        return sts, outs, accs

    def output_stage(c, partial, stage, dec_ref):
        sts, outs, accs = partial
        rows = pl.ds(pl.multiple_of(c * HC, HC), HC)
        lns = [slice(hh * DK_C, (hh + 1) * DK_C) for hh in range(nh)]
        vhs = [i_ref[rows, ln] for ln in lns]
        for hh, ln in enumerate(lns):
            vt = vhs[hh].astype(F32).T.astype(BF16)
            s_ref[hh] = sts[hh] * dec_ref[:, ln] + _dot(vt, stage[_SLOT_KT, :, ln])
        normed = []
        for hh in range(nh):
            o = outs[hh] + _dot(accs[hh].astype(BF16), vhs[hh])
            normed.append(o * lax.rsqrt(jnp.mean(o * o, axis=-1, keepdims=True) + RMS_EPS))
        on = jnp.concatenate(normed, axis=1) * ng_ref[...]
        gx = g_ref[rows, :].astype(F32)
        o_ref[rows, :] = (on * _silu(gx)).astype(o_ref.dtype)

    last = n_steps - 1
    operand_stage(gate_stage(0), stage_a, dec_a)

    def pair(j, carry):
        c = 2 * j
        partial = level_stage(c, stage_a)
        operand_stage(gate_stage(c + 1), stage_b, dec_b)
        output_stage(c, partial, stage_a, dec_a)
        partial = level_stage(c + 1, stage_b)
        operand_stage(gate_stage(jnp.minimum(c + 2, last)), stage_a, dec_a)
        output_stage(c + 1, partial, stage_b, dec_b)
        return carry

    lax.fori_loop(0, n_steps // 2, pair, 0)
    if n_steps % 2:
        output_stage(last, level_stage(last, stage_a), stage_a, dec_a)


def hgrn2(proj, lb_params, norm_g):
    bsz, tp, _ = proj.shape
    mcat, masks = _hgrn_constants()
    nh = HGRN_HEADS_PER_STEP
    wide = nh * DK_C
    n_groups = H_C // nh
    col = lambda off: pl.BlockSpec((None, tp, wide), lambda b, h, off=off: (b, 0, off + h))
    return pl.pallas_call(
        functools.partial(_hgrn_kernel, n_steps=tp // HC, nh=nh),
        grid=(bsz, n_groups),
        in_specs=[
            col(0), col(n_groups), col(2 * n_groups), col(3 * n_groups),
            pl.BlockSpec((2, wide), lambda b, h: (0, h)),
            pl.BlockSpec((1, wide), lambda b, h: (0, h)),
            pl.BlockSpec(mcat.shape, lambda b, h: (0, 0)),
            pl.BlockSpec(masks.shape, lambda b, h: (0, 0, 0)),
        ],
        out_specs=pl.BlockSpec((None, tp, wide), lambda b, h: (b, 0, h)),
        out_shape=jax.ShapeDtypeStruct((bsz, tp, H_C * DK_C), BF16),
        scratch_shapes=[
            pltpu.VMEM((nh, DK_C, DK_C), F32),
            pltpu.VMEM((_N_SLOTS, HC, wide), BF16),
            pltpu.VMEM((_N_SLOTS, HC, wide), BF16),
            pltpu.VMEM((1, wide), F32),
            pltpu.VMEM((1, wide), F32),
        ],
        compiler_params=_cparams(("parallel", "parallel"), VMEM_MIXER),
        name="hgrn2",
    )(proj, proj, proj, proj, lb_params, norm_g.reshape(1, -1).astype(F32),
      jnp.asarray(mcat, BF16), jnp.asarray(masks, F32))


def _t5_bucket(rel):
    nb = N_BUCKETS // 2
    ret = jnp.where(rel > 0, nb, 0)
    n = jnp.abs(rel)
    max_exact = nb // 2
    nf = jnp.maximum(n, 1).astype(F32)
    large = max_exact + (jnp.log(nf / max_exact) / math.log(MAX_DISTANCE / max_exact)
                         * (nb - max_exact)).astype(I32)
    large = jnp.minimum(large, nb - 1)
    return ret + jnp.where(n < max_exact, n, large)


def _bias_table(rel_bias_table):
    v = jnp.arange(BIAS_ROWS, dtype=I32)[:, None]
    q = jnp.arange(LANES, dtype=I32)[None, :]
    onehot = jax.nn.one_hot(_t5_bucket(v - BIAS_ORIGIN - q), N_BUCKETS, dtype=F32)
    tab = jnp.einsum("vqb,bh->hvq", onehot, rel_bias_table.astype(F32), precision=lax.Precision.HIGHEST)
    return tab * LOG2_E


def _even_in_weight(w):
    glu_v, glu_g, gate_a, q, c, gate_b, qi, ki, wi = jnp.split(
        w, np.cumsum([D_A, D_A, D_A, H_B * DH_B, D_C, H_B * DH_B, H_IDX * D_IDX, D_IDX])[:8].tolist(), axis=1)
    pad = jnp.zeros((w.shape[0], E_NPAD - E_COL_WI - H_IDX), w.dtype)
    cols = [glu_v, glu_g, gate_a, q, gate_b, qi * (D_IDX ** -0.5), c, ki, ki, wi * (H_IDX ** -0.5), pad]
    return jnp.concatenate([col.astype(BF16) for col in cols], axis=1)


def kernel(x, meta_tokens, norm_gain, final_norm_gain, rel_bias_table, w_in_even, conv_w, conv_b,
           conv_ln_gain, conv_ln_bias, kv_norm_gain, w_uk, w_uv, w_out_even, w_in_odd, lb_logits,
           rec_norm_gain, w_out_odd):
    bsz, seq, d = x.shape
    assert d == D_MODEL and seq % LANES == 0
    depth = norm_gain.shape[0]
    tp = FRONT_PAD + N_META + seq
    topk = min(TOPK_MAX, seq // 4)
    tt_mm = tp // 2 if (tp // 2) % 16 == 0 else tp
    tt_out = tp // 4 if (tp // 4) % 16 == 0 else tp
    tt_last = seq // 4
    tt_conv = tp // 4 if (tp // 4) % CONV_RB == 0 else tp // 2 if (tp // 2) % CONV_RB == 0 else tp
    h, hn = embed(x, meta_tokens, norm_gain[0], tt=tt_out)

    lb_soft = jax.nn.softmax(lb_logits.astype(F32), axis=0)
    lower_bounds = jnp.cumsum(lb_soft, axis=0) - lb_soft[0]
    bias_tab = _bias_table(rel_bias_table)

    for layer in range(depth):
        last = layer == depth - 1
        if layer % 2 == 0:
            e = layer // 2
            w_in, tn = _even_in_weight(w_in_even[e]), E_NPAD // 2
        else:
            o = layer // 2
            w_in, tn = w_in_odd[o].astype(BF16), 2048
        proj = matmul(hn, w_in, tt=tt_mm, tn=tn)
        if layer % 2 == 0:
            a_out = conv_module(proj, conv_w[e], conv_b[e], conv_ln_gain[e], conv_ln_bias[e], tt=tt_conv)
            wuk_t = jnp.transpose(w_uk[e], (0, 2, 1)).astype(BF16)
            wuv_t = jnp.transpose(w_uv[e], (0, 2, 1)).astype(BF16)
            b_out = dsa_attention(proj, kv_norm_gain[e], wuk_t, wuv_t, bias_tab, topk=topk)
            mixed, w_out = [a_out, b_out], w_out_even[e].astype(BF16)
        else:
            lb = lower_bounds[layer]
            mixed = [hgrn2(proj, jnp.stack([lb, 1.0 - lb], axis=0), rec_norm_gain[o])]
            w_out = w_out_odd[o].astype(BF16)
        if last:
            return out_proj(mixed, w_out, h, final_norm_gain, tt=tt_last, final=True)
        h, hn = out_proj(mixed, w_out, h, norm_gain[layer + 1], tt=tt_out, final=False)
```
